```python
import jax, jax.numpy as jnp
from jax import lax
import numpy as np

D_MODEL = 1024
BATCH = 8
SEQ = 2048
DEPTH = 4

N_A = DEPTH // 2
N_B = DEPTH - N_A
HEAD_DIM = 64
N_MIX_HEADS = 12
N_KV_GROUPS = 2
HEADS_PER_GROUP = N_MIX_HEADS // N_KV_GROUPS
N_MEM_HEADS = 4
N_MEM = 256
L_CMP = 32
CMP_STRIDE = 16
CMP_HIDDEN = 256
L_SEL = 64
TOP_N = 16
WINDOW = 512
Q_BLOCK = 128
D_FF = 2816
CONV_WIDTH = 3
ROPE_THETA = 10000.0
EPS = 1e-6
NEG = -1e30
FORCE_BONUS = 1e4

MIX_W = N_MIX_HEADS * HEAD_DIM
MEM_W = N_MEM_HEADS * HEAD_DIM
BRANCH_KV_W = 6 * N_KV_GROUPS * HEAD_DIM
GATE_W = 3 * N_MIX_HEADS
A_IN = MIX_W + BRANCH_KV_W + GATE_W + MEM_W
B_IN = MIX_W + MEM_W
SHARED_W = 2 * MIX_W + N_MIX_HEADS

kernel_name = "yoco_nsa_fox_hybrid"


def rms_norm(x, g):
    xf = x.astype(jnp.float32)
    y = xf * lax.rsqrt(jnp.mean(xf * xf, axis=-1, keepdims=True) + EPS)
    return (y * g.astype(jnp.float32)).astype(x.dtype)


def rope(x, pos):
    half = HEAD_DIM // 2
    inv = ROPE_THETA ** (-jnp.arange(half, dtype=jnp.float32) / half)
    ang = pos.astype(jnp.float32)[:, None] * inv[None, :]
    shape = (1, pos.shape[0]) + (1,) * (x.ndim - 3) + (half,)
    cos = jnp.cos(ang).reshape(shape)
    sin = jnp.sin(ang).reshape(shape)
    xf = x.astype(jnp.float32)
    x1, x2 = xf[..., :half], xf[..., half:]
    return jnp.concatenate([x1 * cos - x2 * sin, x2 * cos + x1 * sin], axis=-1).astype(x.dtype)


def masked_softmax(s, mask):
    s = jnp.where(mask, s, NEG)
    s = s - jnp.max(s, axis=-1, keepdims=True)
    e = jnp.exp(s) * mask
    return e / jnp.maximum(jnp.sum(e, axis=-1, keepdims=True), 1e-30)


def compress_blocks(u, pos_emb, w1, b1, w2, b2):
    B, S, G, dh = u.shape
    n_cmp = (S - L_CMP) // CMP_STRIDE + 1
    idx = jnp.arange(n_cmp)[:, None] * CMP_STRIDE + jnp.arange(L_CMP)[None, :]
    blocks = u[:, idx] + pos_emb[:, None, :]
    flat = blocks.transpose(0, 1, 3, 2, 4).reshape(B, n_cmp, G, L_CMP * dh)
    return jax.nn.gelu(flat @ w1 + b1) @ w2 + b2


def nsa_attention(q, kc, vc, k_slc, v_slc, k_win, v_win, gates):
    B, S, G, HG, dh = q.shape
    n_cmp = kc.shape[1]
    n_sel = S // L_SEL
    n_top = min(TOP_N, n_sel)
    n_qb = S // Q_BLOCK
    scale = HEAD_DIM ** -0.5
    cmp_start = jnp.arange(n_cmp) * CMP_STRIDE
    cmp_end = cmp_start + L_CMP - 1
    sel_start = jnp.arange(n_sel) * L_SEL
    overlap = ((cmp_start[:, None] < sel_start[None, :] + L_SEL)
               & (cmp_start[:, None] + L_CMP > sel_start[None, :])).astype(jnp.float32)
    kb = k_slc.reshape(B, n_sel, L_SEL, G, dh).transpose(0, 3, 1, 2, 4)
    vb = v_slc.reshape(B, n_sel, L_SEL, G, dh).transpose(0, 3, 1, 2, 4)
    kw_pad = jnp.pad(k_win, ((0, 0), (WINDOW, 0), (0, 0), (0, 0)))
    vw_pad = jnp.pad(v_win, ((0, 0), (WINDOW, 0), (0, 0), (0, 0)))
    bi = jnp.arange(B)[:, None, None, None]
    gi = jnp.arange(G)[None, :, None, None]
    j_sel = jnp.arange(n_sel)

    def block(args):
        c, qc, gc = args
        t = c * Q_BLOCK + jnp.arange(Q_BLOCK)
        s = jnp.einsum('btghd,bngd->bghtn', qc, kc).astype(jnp.float32) * scale
        p_cmp = masked_softmax(s, cmp_end[None, :] <= t[:, None])
        o_cmp = jnp.einsum('bghtn,bngd->btghd', p_cmp.astype(vc.dtype), vc)
        imp = jnp.einsum('bghtn,nj->bgtj', p_cmp, overlap)
        blk_t = t // L_SEL
        forced = (j_sel[None, :] == 0) | (j_sel[None, :] == blk_t[:, None]) | (j_sel[None, :] == blk_t[:, None] - 1)
        valid = j_sel[None, :] <= blk_t[:, None]
        score = jnp.where(valid, imp + FORCE_BONUS * forced.astype(jnp.float32), NEG)
        _, idx = lax.top_k(score, n_top)
        kg = kb[bi, gi, idx]
        vg = vb[bi, gi, idx]
        key_pos = idx[..., None] * L_SEL + jnp.arange(L_SEL)
        m_sel = (key_pos <= t[None, None, :, None, None]).reshape(B, G, 1, Q_BLOCK, n_top * L_SEL)
        s = jnp.einsum('btghd,bgtnld->bghtnl', qc, kg).astype(jnp.float32) * scale
        p = masked_softmax(s.reshape(B, G, HG, Q_BLOCK, n_top * L_SEL), m_sel)
        o_slc = jnp.einsum('bghtk,bgtkd->btghd', p.astype(vg.dtype),
                           vg.reshape(B, G, Q_BLOCK, n_top * L_SEL, dh))
        kw = lax.dynamic_slice_in_dim(kw_pad, c * Q_BLOCK, WINDOW + Q_BLOCK, axis=1)
        vw = lax.dynamic_slice_in_dim(vw_pad, c * Q_BLOCK, WINDOW + Q_BLOCK, axis=1)
        s_pos = c * Q_BLOCK - WINDOW + jnp.arange(WINDOW + Q_BLOCK)
        m_win = (s_pos[None, :] >= 0) & (s_pos[None, :] <= t[:, None]) & (t[:, None] - s_pos[None, :] < WINDOW)
        s = jnp.einsum('btghd,bsgd->bghts', qc, kw).astype(jnp.float32) * scale
        p = masked_softmax(s, m_win)
        o_win = jnp.einsum('bghts,bsgd->btghd', p.astype(vw.dtype), vw)
        g = gc.reshape(B, Q_BLOCK, G, HG, 3)
        return g[..., 0:1] * o_cmp + g[..., 1:2] * o_slc + g[..., 2:3] * o_win

    q_chunks = jnp.moveaxis(q.reshape(B, n_qb, Q_BLOCK, G, HG, dh), 1, 0)
    g_chunks = jnp.moveaxis(gates.reshape(B, n_qb, Q_BLOCK, N_MIX_HEADS, 3), 1, 0)
    out = lax.map(block, (jnp.arange(n_qb), q_chunks, g_chunks))
    return jnp.moveaxis(out, 0, 1).reshape(B, S, MIX_W)


def nsa_mixer(h, w_in, gate_b, cmp_pos, cmp_w1, cmp_b1, cmp_w2, cmp_b2, pos):
    B, S, _ = h.shape
    proj = h @ w_in
    q = proj[..., :MIX_W].reshape(B, S, N_MIX_HEADS, HEAD_DIM)
    kv = proj[..., MIX_W:MIX_W + BRANCH_KV_W].reshape(B, S, 6, N_KV_GROUPS, HEAD_DIM)
    gate_logit = proj[..., MIX_W + BRANCH_KV_W:MIX_W + BRANCH_KV_W + GATE_W]
    q_mem = proj[..., MIX_W + BRANCH_KV_W + GATE_W:].reshape(B, S, N_MEM_HEADS, HEAD_DIM)
    q = rope(q, pos).reshape(B, S, N_KV_GROUPS, HEADS_PER_GROUP, HEAD_DIM)
    k_cmp, v_cmp, k_slc, v_slc, k_win, v_win = (kv[:, :, i] for i in range(6))
    n_cmp = (S - L_CMP) // CMP_STRIDE + 1
    cmp_end = jnp.arange(n_cmp) * CMP_STRIDE + L_CMP - 1
    kc = rope(compress_blocks(k_cmp, cmp_pos[0], cmp_w1[0], cmp_b1[0], cmp_w2[0], cmp_b2[0]), cmp_end)
    vc = compress_blocks(v_cmp, cmp_pos[1], cmp_w1[1], cmp_b1[1], cmp_w2[1], cmp_b2[1])
    gates = jax.nn.sigmoid(gate_logit + gate_b).reshape(B, S, N_MIX_HEADS, 3)
    o = nsa_attention(q, kc, vc, rope(k_slc, pos), v_slc, rope(k_win, pos), v_win, gates)
    return o, q_mem


def forgetting_attention(q, k, v, dcum):
    B, S, H, dh = q.shape
    scale = HEAD_DIM ** -0.5
    outs = []
    for c in range(S // Q_BLOCK):
        lo, hi = c * Q_BLOCK, (c + 1) * Q_BLOCK
        t = lo + jnp.arange(Q_BLOCK)
        s = jnp.einsum('bthd,bshd->bhts', q[:, lo:hi], k[:, :hi]).astype(jnp.float32) * scale
        s = s + dcum[:, :, lo:hi, None] - dcum[:, :, None, :hi]
        p = masked_softmax(s, jnp.arange(hi)[None, :] <= t[:, None])
        outs.append(jnp.einsum('bhts,bshd->bthd', p.astype(v.dtype), v[:, :hi]))
    return jnp.concatenate(outs, axis=1).reshape(B, S, MIX_W)


def memory_attention(q_mem, mem, g_mem, w_mem_kv):
    B, M, _ = mem.shape
    kv = (rms_norm(mem, g_mem) @ w_mem_kv).reshape(B, M, 2, N_MEM_HEADS, HEAD_DIM)
    s = jnp.einsum('bthd,bmhd->bhtm', q_mem, kv[:, :, 0]).astype(jnp.float32) * HEAD_DIM ** -0.5
    p = jax.nn.softmax(s, axis=-1).astype(kv.dtype)
    o = jnp.einsum('bhtm,bmhd->bthd', p, kv[:, :, 1])
    return o.reshape(q_mem.shape[0], q_mem.shape[1], MEM_W)


def conv_ffn(h, w_up, conv_w, conv_b, w_down):
    u = h @ w_up
    C = u.shape[-1]
    u = lax.conv_general_dilated(u, conv_w[:, None, :], window_strides=(1,),
                                 padding=[(CONV_WIDTH - 1, 0)],
                                 dimension_numbers=('NWC', 'WIO', 'NWC'),
                                 feature_group_count=C) + conv_b
    a, b = u[..., :D_FF], u[..., D_FF:]
    return (jax.nn.silu(a) * b) @ w_down


def setup_inputs(seed: int = 0) -> dict:
    key = jax.random.key(seed)
    ks = jax.random.split(key, 24)
    f32 = jnp.float32
    nrm = lambda k, shape, scale: jax.random.normal(k, shape, f32) * scale
    return {
        "x": nrm(ks[0], (BATCH, SEQ, D_MODEL), 1.0),
        "mem": nrm(ks[1], (BATCH, N_MEM, D_MODEL), 1.0),
        "attn_norm": 1.0 + nrm(ks[2], (DEPTH, D_MODEL), 0.02),
        "ffn_norm": 1.0 + nrm(ks[3], (DEPTH, D_MODEL), 0.02),
        "mem_norm": 1.0 + nrm(ks[4], (DEPTH, D_MODEL), 0.02),
        "w_mem_kv": nrm(ks[5], (DEPTH, D_MODEL, 2 * MEM_W), D_MODEL ** -0.5),
        "w_o": nrm(ks[6], (DEPTH, MIX_W + MEM_W, D_MODEL), (MIX_W + MEM_W) ** -0.5),
        "w_up": nrm(ks[7], (DEPTH, D_MODEL, 2 * D_FF), D_MODEL ** -0.5),
        "conv_w": nrm(ks[8], (DEPTH, CONV_WIDTH, 2 * D_FF), CONV_WIDTH ** -0.5),
        "conv_b": nrm(ks[9], (DEPTH, 2 * D_FF), 0.01),
        "w_down": nrm(ks[10], (DEPTH, D_FF, D_MODEL), D_FF ** -0.5),
        "a_w_in": nrm(ks[11], (N_A, D_MODEL, A_IN), D_MODEL ** -0.5),
        "a_gate_b": nrm(ks[12], (N_A, GATE_W), 0.01),
        "a_cmp_pos": nrm(ks[13], (N_A, 2, L_CMP, HEAD_DIM), 0.1),
        "a_cmp_w1": nrm(ks[14], (N_A, 2, L_CMP * HEAD_DIM, CMP_HIDDEN), (L_CMP * HEAD_DIM) ** -0.5),
        "a_cmp_b1": nrm(ks[15], (N_A, 2, CMP_HIDDEN), 0.01),
        "a_cmp_w2": nrm(ks[16], (N_A, 2, CMP_HIDDEN, HEAD_DIM), CMP_HIDDEN ** -0.5),
        "a_cmp_b2": nrm(ks[17], (N_A, 2, HEAD_DIM), 0.01),
        "b_w_in": nrm(ks[18], (N_B, D_MODEL, B_IN), D_MODEL ** -0.5),
        "kv_norm": 1.0 + nrm(ks[19], (D_MODEL,), 0.02),
        "w_kv_shared": nrm(ks[20], (D_MODEL, SHARED_W), D_MODEL ** -0.5),
        "b_fgate": 3.0 + nrm(ks[21], (N_MIX_HEADS,), 0.1),
        "final_norm": 1.0 + nrm(ks[22], (D_MODEL,), 0.02),
    }


def reference(x, mem, attn_norm, ffn_norm, mem_norm, w_mem_kv, w_o, w_up, conv_w, conv_b, w_down,
              a_w_in, a_gate_b, a_cmp_pos, a_cmp_w1, a_cmp_b1, a_cmp_w2, a_cmp_b2,
              b_w_in, kv_norm, w_kv_shared, b_fgate, final_norm):
    B, S, _ = x.shape
    pos = jnp.arange(S)
    k_sh = v_sh = dcum = None
    for l in range(DEPTH):
        h = rms_norm(x, attn_norm[l])
        if l < N_A:
            o_mix, q_mem = nsa_mixer(h, a_w_in[l], a_gate_b[l], a_cmp_pos[l], a_cmp_w1[l], a_cmp_b1[l],
                                     a_cmp_w2[l], a_cmp_b2[l], pos)
        else:
            if l == N_A:
                hs = rms_norm(x, kv_norm) @ w_kv_shared
                k_sh = hs[..., :MIX_W].reshape(B, S, N_MIX_HEADS, HEAD_DIM)
                v_sh = hs[..., MIX_W:2 * MIX_W].reshape(B, S, N_MIX_HEADS, HEAD_DIM)
                log_f = jax.nn.log_sigmoid(hs[..., 2 * MIX_W:].astype(jnp.float32) + b_fgate.astype(jnp.float32))
                dcum = jnp.cumsum(log_f, axis=1).transpose(0, 2, 1)
            proj = h @ b_w_in[l - N_A]
            q = proj[..., :MIX_W].reshape(B, S, N_MIX_HEADS, HEAD_DIM)
            q_mem = proj[..., MIX_W:].reshape(B, S, N_MEM_HEADS, HEAD_DIM)
            o_mix = forgetting_attention(q, k_sh, v_sh, dcum)
        o_mem = memory_attention(q_mem, mem, mem_norm[l], w_mem_kv[l])
        x = x + jnp.concatenate([o_mix, o_mem], axis=-1) @ w_o[l]
        x = x + conv_ffn(rms_norm(x, ffn_norm[l]), w_up[l], conv_w[l], conv_b[l], w_down[l])
    return rms_norm(x, final_norm)
```

```python
import functools

import numpy as np
import jax
import jax.numpy as jnp
from jax import lax
from jax.experimental import pallas as pl
from jax.experimental.pallas import tpu as pltpu

D_MODEL = 1024
HEAD_DIM = 64
N_MIX_HEADS = 12
N_KV_GROUPS = 2
HEADS_PER_GROUP = N_MIX_HEADS // N_KV_GROUPS
N_MEM_HEADS = 4
L_CMP = 32
CMP_STRIDE = 16
CMP_HIDDEN = 256
L_SEL = 64
TOP_N = 16
WINDOW = 512
D_FF = 2816
CONV_WIDTH = 3
ROPE_THETA = 10000.0
EPS = 1e-6
NEG = -1e30
FORCE_BONUS = 1e4
MIX_W = N_MIX_HEADS * HEAD_DIM
MEM_W = N_MEM_HEADS * HEAD_DIM
GATE_W = 3 * N_MIX_HEADS
N_A = 2
DEPTH = 4
Q_SCALE = HEAD_DIM ** -0.5

LANES = 128
TM = 256
QB = 128
FOX_TQ = 512
FOX_KC = 256
MEM_TQ = 256
FF_CHUNK = 256
GATE_ROWS = 40
VMEM_LIMIT = 56 * 1024 * 1024

F32 = jnp.float32
BF16 = jnp.bfloat16


def _params(sem):
    return pltpu.CompilerParams(dimension_semantics=sem, vmem_limit_bytes=VMEM_LIMIT)


def _const_spec(shape):
    n = len(shape)
    return pl.BlockSpec(shape, lambda *_: (0,) * n, pipeline_mode=pl.Buffered(1))


def _rms(x, g):
    return x * lax.rsqrt(jnp.mean(x * x, axis=-1, keepdims=True) + EPS) * g


def _dot(a, b):
    return jnp.dot(a, b, preferred_element_type=F32)


def _rope_lanes(x, cos, sin_signed):
    lane = lax.broadcasted_iota(jnp.int32, x.shape, 1)
    first_half = (lane & (HEAD_DIM - 1)) < (HEAD_DIM // 2)
    partner = jnp.where(first_half, pltpu.roll(x, LANES - HEAD_DIM // 2, 1), pltpu.roll(x, HEAD_DIM // 2, 1))
    return x * cos + partner * sin_signed


def _split3(x):
    a = x.astype(BF16)
    r = x - a.astype(F32)
    b = r.astype(BF16)
    c = (r - b.astype(F32)).astype(BF16)
    return a, b, c


def _proj_nsa_kernel(x_ref, g_ref, wq_ref, wkv_ref, wg_ref, gb_ref, wqm_ref, cosT_ref, sinT_ref, rc_ref, rs_ref,
                     qT_ref, kcmp_ref, vcmp_ref, kslc_ref, kwin_ref, vslcT_ref, vwinT_ref, gT_ref, qmT_ref):
    h = _rms(x_ref[0], g_ref[...]).astype(BF16)
    qT = (_dot(h, wq_ref[...]) * Q_SCALE).T
    cos = cosT_ref[...]
    sin = sinT_ref[...]
    half = HEAD_DIM // 2
    for hh in range(N_MIX_HEADS):
        x1 = qT[hh * HEAD_DIM:hh * HEAD_DIM + half]
        x2 = qT[hh * HEAD_DIM + half:(hh + 1) * HEAD_DIM]
        qT_ref[0, hh * HEAD_DIM:hh * HEAD_DIM + half, :] = (x1 * cos - x2 * sin).astype(BF16)
        qT_ref[0, hh * HEAD_DIM + half:(hh + 1) * HEAD_DIM, :] = (x2 * cos + x1 * sin).astype(BF16)
    kv = _dot(h, wkv_ref[...])
    rc = rc_ref[...]
    rs = rs_ref[...]
    kcmp_ref[0] = kv[:, 0:128]
    vcmp_ref[0] = kv[:, 128:256]
    kslc_ref[0] = _rope_lanes(kv[:, 256:384], rc, rs).astype(BF16)
    kwin_ref[0] = _rope_lanes(kv[:, 512:640], rc, rs).astype(BF16)
    vsT = kv[:, 384:512].T
    vwT = kv[:, 640:768].T
    for j in range(TM // QB):
        vslcT_ref[0, j] = vsT[:, j * QB:(j + 1) * QB].astype(BF16)
        vwinT_ref[0, j] = vwT[:, j * QB:(j + 1) * QB].astype(BF16)
    gates = jax.nn.sigmoid(_dot(h, wg_ref[...]) + gb_ref[...])
    gT_ref[0] = gates.T[0:GATE_ROWS]
    qmT_ref[0] = (_dot(h, wqm_ref[...]) * Q_SCALE).T.astype(BF16)


def _proj_nsa(x, g, wq, wkv, wg, gb, wqm, cosT, sinT, rc, rs):
    B, S, _ = x.shape
    nt = S // TM
    tok = lambda w: pl.BlockSpec((1, TM, w), lambda b, j: (b, j, 0))
    featT = lambda r: pl.BlockSpec((1, r, TM), lambda b, j: (b, 0, j))
    vT = pl.BlockSpec((1, TM // QB, LANES, QB), lambda b, j: (b, j, 0, 0))
    return pl.pallas_call(
        _proj_nsa_kernel,
        grid=(B, nt),
        in_specs=[tok(D_MODEL), _const_spec((1, D_MODEL)), _const_spec(wq.shape), _const_spec(wkv.shape),
                  _const_spec(wg.shape), _const_spec(gb.shape), _const_spec(wqm.shape),
                  pl.BlockSpec((HEAD_DIM // 2, TM), lambda b, j: (0, j)),
                  pl.BlockSpec((HEAD_DIM // 2, TM), lambda b, j: (0, j)),
                  pl.BlockSpec((TM, LANES), lambda b, j: (j, 0)),
                  pl.BlockSpec((TM, LANES), lambda b, j: (j, 0))],
        out_specs=[featT(MIX_W), tok(LANES), tok(LANES), tok(LANES), tok(LANES), vT, vT, featT(GATE_ROWS),
                   featT(MEM_W)],
        out_shape=[jax.ShapeDtypeStruct((B, MIX_W, S), BF16),
                   jax.ShapeDtypeStruct((B, S, LANES), F32), jax.ShapeDtypeStruct((B, S, LANES), F32),
                   jax.ShapeDtypeStruct((B, S, LANES), BF16), jax.ShapeDtypeStruct((B, S, LANES), BF16),
                   jax.ShapeDtypeStruct((B, S // QB, LANES, QB), BF16),
                   jax.ShapeDtypeStruct((B, S // QB, LANES, QB), BF16),
                   jax.ShapeDtypeStruct((B, GATE_ROWS, S), F32),
                   jax.ShapeDtypeStruct((B, MEM_W, S), BF16)],
        compiler_params=_params(("arbitrary", "arbitrary")),
        name="proj_nsa",
    )(x, g, wq, wkv, wg, gb, wqm, cosT, sinT, rc, rs)


def _proj_fox_kernel(x_ref, g_ref, wq_ref, wqm_ref, qT_ref, qmT_ref):
    h = _rms(x_ref[0], g_ref[...]).astype(BF16)
    qT_ref[0] = (_dot(h, wq_ref[...]) * Q_SCALE).T.astype(BF16)
    qmT_ref[0] = (_dot(h, wqm_ref[...]) * Q_SCALE).T.astype(BF16)


def _proj_fox(x, g, wq, wqm):
    B, S, _ = x.shape
    featT = lambda r: pl.BlockSpec((1, r, TM), lambda b, j: (b, 0, j))
    return pl.pallas_call(
        _proj_fox_kernel,
        grid=(B, S // TM),
        in_specs=[pl.BlockSpec((1, TM, D_MODEL), lambda b, j: (b, j, 0)), _const_spec((1, D_MODEL)),
                  _const_spec(wq.shape), _const_spec(wqm.shape)],
        out_specs=[featT(MIX_W), featT(MEM_W)],
        out_shape=[jax.ShapeDtypeStruct((B, MIX_W, S), BF16), jax.ShapeDtypeStruct((B, MEM_W, S), BF16)],
        compiler_params=_params(("arbitrary", "arbitrary")),
        name="proj_fox",
    )(x, g, wq, wqm)


def _kv_shared_kernel(x_ref, g_ref, wk_ref, wv_ref, wf_ref, bf_ref, k_ref, vT_ref, dcum_ref, dcumT_ref, carry_ref):
    j = pl.program_id(1)
    h = _rms(x_ref[0], g_ref[...]).astype(BF16)
    k_ref[0] = _dot(h, wk_ref[...]).astype(BF16)
    vT_ref[0, 0] = _dot(h, wv_ref[...]).T.astype(BF16)
    z = _dot(h, wf_ref[...]) + bf_ref[...]
    logf = jnp.minimum(z, 0.0) - jnp.log1p(jnp.exp(-jnp.abs(z)))
    row = lax.broadcasted_iota(jnp.int32, (TM, TM), 0)
    col = lax.broadcasted_iota(jnp.int32, (TM, TM), 1)
    tri = (col <= row).astype(BF16)
    a, b, c = _split3(logf)
    @pl.when(j == 0)
    def _():
        carry_ref[...] = jnp.zeros(carry_ref.shape, F32)

    cs = _dot(tri, a) + _dot(tri, b) + _dot(tri, c) + carry_ref[7:8, :]
    carry_ref[...] = cs[TM - 8:TM]
    dcum_ref[0] = cs
    dcumT_ref[0] = cs.T[0:16]


def _kv_shared(x, g, wk, wv, wf, bf):
    B, S, _ = x.shape
    assert TM == FOX_KC
    return pl.pallas_call(
        _kv_shared_kernel,
        grid=(B, S // TM),
        in_specs=[pl.BlockSpec((1, TM, D_MODEL), lambda b, j: (b, j, 0)), _const_spec((1, D_MODEL)),
                  _const_spec(wk.shape), _const_spec(wv.shape), _const_spec(wf.shape), _const_spec(bf.shape)],
        out_specs=[pl.BlockSpec((1, TM, MIX_W), lambda b, j: (b, j, 0)),
                   pl.BlockSpec((1, 1, MIX_W, FOX_KC), lambda b, j: (b, j, 0, 0)),
                   pl.BlockSpec((1, TM, LANES), lambda b, j: (b, j, 0)),
                   pl.BlockSpec((1, 16, TM), lambda b, j: (b, 0, j))],
        out_shape=[jax.ShapeDtypeStruct((B, S, MIX_W), BF16),
                   jax.ShapeDtypeStruct((B, S // FOX_KC, MIX_W, FOX_KC), BF16),
                   jax.ShapeDtypeStruct((B, S, LANES), F32),
                   jax.ShapeDtypeStruct((B, 16, S), F32)],
        scratch_shapes=[pltpu.VMEM((8, LANES), F32)],
        compiler_params=_params(("arbitrary", "arbitrary")),
        name="kv_shared",
    )(x, g, wk, wv, wf, bf)


def _mem_kv_kernel(m_ref, g_ref, w_ref, k_ref, vT_ref):
    h = _rms(m_ref[0], g_ref[...]).astype(BF16)
    kv = _dot(h, w_ref[...])
    k_ref[0] = kv[:, 0:MEM_W].astype(BF16)
    vT_ref[0] = kv[:, MEM_W:2 * MEM_W].T.astype(BF16)


def _mem_kv(mem, g, w):
    B, M, _ = mem.shape
    return pl.pallas_call(
        _mem_kv_kernel,
        grid=(B,),
        in_specs=[pl.BlockSpec((1, M, D_MODEL), lambda b: (b, 0, 0)), _const_spec((1, D_MODEL)),
                  _const_spec(w.shape)],
        out_specs=[pl.BlockSpec((1, M, MEM_W), lambda b: (b, 0, 0)), pl.BlockSpec((1, MEM_W, M), lambda b: (b, 0, 0))],
        out_shape=[jax.ShapeDtypeStruct((B, M, MEM_W), BF16), jax.ShapeDtypeStruct((B, MEM_W, M), BF16)],
        compiler_params=_params(("arbitrary",)),
        name="mem_kv",
    )(mem, g, w)


def _compress_kernel(kx_ref, vx_ref, wtop_ref, wbot_ref, ptop_ref, pbot_ref, b1_ref, w2_ref, b2_ref, rc_ref, rs_ref,
                     kc_ref, vcT_ref):
    for kvi, x_ref in enumerate((kx_ref, vx_ref)):
        x = x_ref[0]
        xt = (x + ptop_ref[kvi]).astype(BF16)
        xb = (x + pbot_ref[kvi]).astype(BF16)
        out = b2_ref[kvi]
        for g in range(N_KV_GROUPS):
            top = _dot(xt, wtop_ref[kvi, g])
            bot = _dot(xb, wbot_ref[kvi, g])
            hid = top + pltpu.roll(bot, bot.shape[0] - 1, 0) + b1_ref[kvi]
            out = out + _dot(jax.nn.gelu(hid).astype(BF16), w2_ref[kvi, g])
        if kvi == 0:
            kc_ref[0] = _rope_lanes(out, rc_ref[...], rs_ref[...]).astype(BF16)
        else:
            vcT_ref[0] = out.T.astype(BF16)


def _compress(kx, vx, wtop, wbot, ptop, pbot, b1, w2, b2, rc, rs):
    B, NC, W = kx.shape
    xs = pl.BlockSpec((1, NC, W), lambda b: (b, 0, 0))
    return pl.pallas_call(
        _compress_kernel,
        grid=(B,),
        in_specs=[xs, xs] + [_const_spec(a.shape) for a in (wtop, wbot, ptop, pbot, b1, w2, b2, rc, rs)],
        out_specs=[pl.BlockSpec((1, NC, LANES), lambda b: (b, 0, 0)), pl.BlockSpec((1, LANES, NC), lambda b: (b, 0, 0))],
        out_shape=[jax.ShapeDtypeStruct((B, NC, LANES), BF16), jax.ShapeDtypeStruct((B, LANES, NC), BF16)],
        compiler_params=_params(("arbitrary",)),
        name="nsa_compress",
    )(kx, vx, wtop, wbot, ptop, pbot, b1, w2, b2, rc, rs)


def _online_update(s, valid, hh, m_ref, l_ref):
    cols = slice(hh * QB, (hh + 1) * QB)
    s = jnp.where(valid, s, NEG)
    m_old = m_ref[:, cols]
    m_new = jnp.maximum(m_old, jnp.max(s, axis=0, keepdims=True))
    alpha = jnp.exp(m_old - m_new)
    e = jnp.exp(s - m_new) * valid.astype(F32)
    l_ref[:, cols] = alpha * l_ref[:, cols] + jnp.sum(e, axis=0, keepdims=True)
    m_ref[:, cols] = m_new
    return e.astype(BF16), alpha


def _group_q(qT_ref, g):
    q6 = jnp.concatenate([qT_ref[0, (HEADS_PER_GROUP * g + hh) * HEAD_DIM:(HEADS_PER_GROUP * g + hh + 1) * HEAD_DIM, :]
                          for hh in range(HEADS_PER_GROUP)], axis=1)
    z = jnp.zeros_like(q6)
    return jnp.concatenate([q6, z] if g == 0 else [z, q6], axis=0)


def _nsa_attn_kernel(qT_ref, gT_ref, kc_ref, vcT_ref, kslc_ref, kwin_ref, vslcT_ref, vwinT_ref, ovT_ref,
                     o_ref, sel_ref, m_ref, l_ref, acc_ref):
    c = pl.program_id(1)
    n_sel = sel_ref.shape[0]
    n_cmp = kc_ref.shape[1]
    t_row = c * QB + lax.broadcasted_iota(jnp.int32, (1, QB), 1)
    key_off = lax.broadcasted_iota(jnp.int32, (QB, QB), 0)
    gates = gT_ref[0]
    GW = HEADS_PER_GROUP * HEAD_DIM

    def reset():
        m_ref[...] = jnp.full(m_ref.shape, NEG, F32)
        l_ref[...] = jnp.zeros(l_ref.shape, F32)
        acc_ref[...] = jnp.zeros(acc_ref.shape, F32)

    def chunk(k, vT, qpad, valid):
        s = _dot(k, qpad)
        ps, alphas = [], []
        for hh in range(HEADS_PER_GROUP):
            p, a = _online_update(s[:, hh * QB:(hh + 1) * QB], valid, hh, m_ref, l_ref)
            ps.append(p)
            alphas.append(a)
        acc_ref[...] = acc_ref[...] * jnp.concatenate(alphas, axis=1) + _dot(vT, jnp.concatenate(ps, axis=1))

    def finish():
        return acc_ref[...] / jnp.maximum(l_ref[...], 1e-30)

    for g in range(N_KV_GROUPS):
        qpad = _group_q(qT_ref, g)
        rows = slice(g * HEAD_DIM, (g + 1) * HEAD_DIM)

        cmp_end = lax.broadcasted_iota(jnp.int32, (n_cmp, QB), 0) * CMP_STRIDE + (L_CMP - 1)
        valid_c = cmp_end <= t_row
        s = _dot(kc_ref[0], qpad)
        psum = jnp.zeros((n_cmp, QB), F32)
        ps = []
        for hh in range(HEADS_PER_GROUP):
            sh = jnp.where(valid_c, s[:, hh * QB:(hh + 1) * QB], NEG)
            e = jnp.exp(sh - jnp.max(sh, axis=0, keepdims=True)) * valid_c.astype(F32)
            p = e / jnp.maximum(jnp.sum(e, axis=0, keepdims=True), 1e-30)
            psum = psum + p
            ps.append(p.astype(BF16))
        o_cmp = _dot(vcT_ref[0, rows, :], jnp.concatenate(ps, axis=1))
        p_hi = psum.astype(BF16)
        p_lo = (psum - p_hi.astype(F32)).astype(BF16)
        imp = _dot(ovT_ref[...], p_hi) + _dot(ovT_ref[...], p_lo)

        j_idx = lax.broadcasted_iota(jnp.int32, (n_sel, QB), 0)
        blk_t = lax.shift_right_logical(t_row, 6)
        forced = (j_idx == 0) | (j_idx == blk_t) | (j_idx == blk_t - 1)
        score = jnp.where(j_idx <= blk_t, imp + FORCE_BONUS * forced.astype(F32), NEG)
        cnt = jnp.zeros((n_sel, QB), F32)
        for i in range(n_sel):
            si = score[i:i + 1, :]
            beats = (si > score) | ((si == score) & (j_idx > i))
            cnt = cnt + beats.astype(F32)
        sel = (cnt < float(min(TOP_N, n_sel))).astype(F32)
        for i in range(n_sel):
            sel_ref[i] = jnp.broadcast_to(sel[i:i + 1, :], (8, QB))

        reset()

        def slc_body(kb, carry):
            k = kslc_ref[0, pl.ds(pl.multiple_of(kb * QB, QB), QB), :]
            key_pos = kb * QB + key_off
            lo = sel_ref[2 * kb][0:1, :]
            hi = sel_ref[2 * kb + 1][0:1, :]
            picked = jnp.where(key_off < L_SEL, lo, hi) > 0.5
            chunk(k, vslcT_ref[0, kb, rows, :], qpad, picked & (key_pos <= t_row))
            return carry

        lax.fori_loop(0, c + 1, slc_body, 0)
        o_slc = finish()

        reset()

        def win_body(kb, carry):
            k = kwin_ref[0, pl.ds(pl.multiple_of(kb * QB, QB), QB), :]
            key_pos = kb * QB + key_off
            chunk(k, vwinT_ref[0, kb, rows, :], qpad, (key_pos <= t_row) & (t_row - key_pos < WINDOW))
            return carry

        lax.fori_loop(jnp.maximum(c - WINDOW // QB, 0), c + 1, win_body, 0)
        o_win = finish()

        outs = []
        for hh in range(HEADS_PER_GROUP):
            r = 3 * (HEADS_PER_GROUP * g + hh)
            cols = slice(hh * QB, (hh + 1) * QB)
            outs.append(gates[r:r + 1] * o_cmp[:, cols] + gates[r + 1:r + 2] * o_slc[:, cols]
                        + gates[r + 2:r + 3] * o_win[:, cols])
        oT = jnp.concatenate(outs, axis=0)
        o_ref[0, :, g * GW:(g + 1) * GW] = oT.T.astype(BF16)


def _nsa_attn(qT, gT, kc, vcT, kslc, kwin, vslcT, vwinT, ovT):
    B, _, S = qT.shape
    n_cmp = kc.shape[1]
    n_sel = S // L_SEL
    W6 = HEADS_PER_GROUP * QB
    per_b3 = lambda a: pl.BlockSpec((1,) + a.shape[1:], lambda b, c: (b, 0, 0))
    per_b4 = lambda a: pl.BlockSpec((1,) + a.shape[1:], lambda b, c: (b, 0, 0, 0))
    return pl.pallas_call(
        _nsa_attn_kernel,
        grid=(B, S // QB),
        in_specs=[pl.BlockSpec((1, MIX_W, QB), lambda b, c: (b, 0, c)),
                  pl.BlockSpec((1, GATE_ROWS, QB), lambda b, c: (b, 0, c)),
                  per_b3(kc), per_b3(vcT), per_b3(kslc), per_b3(kwin), per_b4(vslcT), per_b4(vwinT),
                  _const_spec(ovT.shape)],
        out_specs=pl.BlockSpec((1, QB, MIX_W), lambda b, c: (b, c, 0)),
        out_shape=jax.ShapeDtypeStruct((B, S, MIX_W), BF16),
        scratch_shapes=[pltpu.VMEM((n_sel, 8, QB), F32), pltpu.VMEM((1, W6), F32), pltpu.VMEM((1, W6), F32),
                        pltpu.VMEM((HEAD_DIM, W6), F32)],
        compiler_params=_params(("arbitrary", "arbitrary")),
        name="nsa_attn",
    )(qT, gT, kc, vcT, kslc, kwin, vslcT, vwinT, ovT)


def _pair_q(q2, width):
    z = jnp.zeros((HEAD_DIM, width), q2.dtype)
    return jnp.concatenate([jnp.concatenate([q2[0:HEAD_DIM], z], axis=1),
                            jnp.concatenate([z, q2[HEAD_DIM:2 * HEAD_DIM]], axis=1)], axis=0)


def _fox_attn_kernel(qT_ref, k_ref, vT_ref, dcum_ref, dcumT_ref, o_ref, db_ref, m_ref, l_ref, acc_ref):
    p = pl.program_id(1)
    c = pl.program_id(2)
    nq = FOX_TQ // QB

    @pl.when(c == 0)
    def _():
        parts = _split3(dcum_ref[0])
        lane_row = lax.broadcasted_iota(jnp.int32, (LANES, LANES), 0)
        for hh in range(2):
            pick = (lane_row == 2 * p + hh).astype(BF16)
            db_ref[hh] = _dot(parts[0], pick) + _dot(parts[1], pick) + _dot(parts[2], pick)

    qbd = _pair_q(qT_ref[0], FOX_TQ)
    m_ref[...] = jnp.full(m_ref.shape, NEG, F32)
    l_ref[...] = jnp.zeros(l_ref.shape, F32)
    acc_ref[...] = jnp.zeros(acc_ref.shape, F32)
    key_off = lax.broadcasted_iota(jnp.int32, (FOX_KC, QB), 0)
    lane_t = lax.broadcasted_iota(jnp.int32, (1, QB), 1)
    dq = [dcumT_ref[0, pl.ds(2 * p + hh, 1), :] for hh in range(2)]

    def body(kb, carry):
        start = pl.multiple_of(kb * FOX_KC, FOX_KC)
        s = _dot(k_ref[0, pl.ds(start, FOX_KC), :], qbd)
        key_pos = kb * FOX_KC + key_off
        for hh in range(2):
            dk = db_ref[hh, pl.ds(start, FOX_KC), :]
            ps, alphas = [], []
            for qs in range(nq):
                slot = hh * nq + qs
                t_row = c * FOX_TQ + qs * QB + lane_t
                sh = s[:, slot * QB:(slot + 1) * QB] + dq[hh][:, qs * QB:(qs + 1) * QB] - dk
                pr, a = _online_update(sh, key_pos <= t_row, slot, m_ref, l_ref)
                ps.append(pr)
                alphas.append(a)
            cols = slice(hh * FOX_TQ, (hh + 1) * FOX_TQ)
            acc_ref[:, cols] = (acc_ref[:, cols] * jnp.concatenate(alphas, axis=1)
                                + _dot(vT_ref[0, kb, hh * HEAD_DIM:(hh + 1) * HEAD_DIM, :], jnp.concatenate(ps, axis=1)))
        return carry

    lax.fori_loop(0, (c + 1) * (FOX_TQ // FOX_KC), body, 0)
    o = acc_ref[...] / jnp.maximum(l_ref[...], 1e-30)
    oT = jnp.concatenate([o[:, 0:FOX_TQ], o[:, FOX_TQ:2 * FOX_TQ]], axis=0)
    o_ref[0] = oT.T.astype(BF16)


def _fox_attn(qT, k, vT, dcum, dcumT):
    B, _, S = qT.shape
    n_pairs = N_MIX_HEADS // 2
    return pl.pallas_call(
        _fox_attn_kernel,
        grid=(B, n_pairs, S // FOX_TQ),
        in_specs=[pl.BlockSpec((1, LANES, FOX_TQ), lambda b, p, c: (b, p, c)),
                  pl.BlockSpec((1, S, LANES), lambda b, p, c: (b, 0, p)),
                  pl.BlockSpec((1, S // FOX_KC, LANES, FOX_KC), lambda b, p, c: (b, 0, p, 0)),
                  pl.BlockSpec((1, S, LANES), lambda b, p, c: (b, 0, 0)),
                  pl.BlockSpec((1, 16, FOX_TQ), lambda b, p, c: (b, 0, c))],
        out_specs=pl.BlockSpec((1, FOX_TQ, LANES), lambda b, p, c: (b, c, p)),
        out_shape=jax.ShapeDtypeStruct((B, S, MIX_W), BF16),
        scratch_shapes=[pltpu.VMEM((2, S, LANES), F32), pltpu.VMEM((1, 2 * FOX_TQ), F32),
                        pltpu.VMEM((1, 2 * FOX_TQ), F32), pltpu.VMEM((HEAD_DIM, 2 * FOX_TQ), F32)],
        compiler_params=_params(("arbitrary", "arbitrary", "arbitrary")),
        name="fox_attn",
    )(qT, k, vT, dcum, dcumT)


def _mem_attn_kernel(qT_ref, k_ref, vT_ref, o_ref):
    outs = []
    for pr in range(N_MEM_HEADS // 2):
        qbd = _pair_q(qT_ref[0, pr * LANES:(pr + 1) * LANES, :], MEM_TQ)
        s = _dot(k_ref[0, :, pr * LANES:(pr + 1) * LANES], qbd)
        for hh in range(2):
            ps = []
            for qs in range(MEM_TQ // QB):
                col = hh * MEM_TQ + qs * QB
                sh = s[:, col:col + QB]
                e = jnp.exp(sh - jnp.max(sh, axis=0, keepdims=True))
                ps.append((e / jnp.sum(e, axis=0, keepdims=True)).astype(BF16))
            h = 2 * pr + hh
            outs.append(_dot(vT_ref[0, h * HEAD_DIM:(h + 1) * HEAD_DIM, :], jnp.concatenate(ps, axis=1)))
    o_ref[0] = jnp.concatenate(outs, axis=0).T.astype(BF16)


def _mem_attn(qmT, kmem, vmemT):
    B, _, S = qmT.shape
    M = kmem.shape[1]
    return pl.pallas_call(
        _mem_attn_kernel,
        grid=(B, S // MEM_TQ),
        in_specs=[pl.BlockSpec((1, MEM_W, MEM_TQ), lambda b, c: (b, 0, c)),
                  pl.BlockSpec((1, M, MEM_W), lambda b, c: (b, 0, 0)),
                  pl.BlockSpec((1, MEM_W, M), lambda b, c: (b, 0, 0))],
        out_specs=pl.BlockSpec((1, MEM_TQ, MEM_W), lambda b, c: (b, c, 0)),
        out_shape=jax.ShapeDtypeStruct((B, S, MEM_W), BF16),
        compiler_params=_params(("arbitrary", "arbitrary")),
        name="mem_attn",
    )(qmT, kmem, vmemT)


def _ffn_kernel(x_ref, omix_ref, omem_ref, wo1_ref, wo2_ref, g_ref, wa_ref, wb_ref, cwa_ref, cwb_ref, cba_ref,
                cbb_ref, wd_ref, gfin_ref, o_ref, ca_ref, cb_ref, *, final):
    j = pl.program_id(1)
    x1 = x_ref[0] + _dot(omix_ref[0], wo1_ref[...]) + _dot(omem_ref[0], wo2_ref[...])
    h = _rms(x1, g_ref[...]).astype(BF16)
    o_ref[0] = x1
    row = lax.broadcasted_iota(jnp.int32, (TM, FF_CHUNK), 0)

    @pl.when(j == 0)
    def _():
        ca_ref[...] = jnp.zeros(ca_ref.shape, F32)
        cb_ref[...] = jnp.zeros(cb_ref.shape, F32)

    def conv(u, prev8, w, bias):
        p1 = prev8[7:8]
        p2 = prev8[6:7]
        u1 = jnp.where(row == 0, p1, pltpu.roll(u, 1, 0))
        u2 = jnp.where(row == 0, p2, jnp.where(row == 1, p1, pltpu.roll(u, 2, 0)))
        return w[0:1] * u2 + w[1:2] * u1 + w[2:3] * u + bias

    for ci in range(D_FF // FF_CHUNK):
        sl = slice(ci * FF_CHUNK, (ci + 1) * FF_CHUNK)
        a = _dot(h, wa_ref[:, sl])
        b = _dot(h, wb_ref[:, sl])
        pa = ca_ref[:, sl]
        pb = cb_ref[:, sl]
        ca_ref[:, sl] = a[TM - 8:TM]
        cb_ref[:, sl] = b[TM - 8:TM]
        ac = conv(a, pa, cwa_ref[:, sl], cba_ref[:, sl])
        bc = conv(b, pb, cwb_ref[:, sl], cbb_ref[:, sl])
        gated = (ac * jax.nn.sigmoid(ac) * bc).astype(BF16)
        o_ref[0] += _dot(gated, wd_ref[sl, :])
    if final:
        o_ref[0] = _rms(o_ref[0], gfin_ref[...])


def _ffn(x, omix, omem, wo1, wo2, g, wa, wb, cwa, cwb, cba, cbb, wd, gfin, final):
    B, S, _ = x.shape
    tok = lambda w: pl.BlockSpec((1, TM, w), lambda b, j: (b, j, 0))
    consts = (wo1, wo2, g, wa, wb, cwa, cwb, cba, cbb, wd, gfin)
    return pl.pallas_call(
        functools.partial(_ffn_kernel, final=final),
        grid=(B, S // TM),
        in_specs=[tok(D_MODEL), tok(MIX_W), tok(MEM_W)] + [_const_spec(a.shape) for a in consts],
        out_specs=tok(D_MODEL),
        out_shape=jax.ShapeDtypeStruct((B, S, D_MODEL), F32),
        scratch_shapes=[pltpu.VMEM((8, D_FF), F32), pltpu.VMEM((8, D_FF), F32)],
        compiler_params=_params(("arbitrary", "arbitrary")),
        name="outproj_convffn",
    )(x, omix, omem, wo1, wo2, g, wa, wb, cwa, cwb, cba, cbb, wd, gfin)


def _rope_tables(pos):
    half = HEAD_DIM // 2
    inv = ROPE_THETA ** (-jnp.arange(half, dtype=F32) / half)
    ang = pos.astype(F32)[:, None] * inv[None, :]
    cos, sin = jnp.cos(ang), jnp.sin(ang)
    reps = LANES // half
    signs = jnp.tile(jnp.concatenate([-jnp.ones((half,), F32), jnp.ones((half,), F32)]), LANES // HEAD_DIM)
    return cos.T, sin.T, jnp.tile(cos, (1, reps)), jnp.tile(sin, (1, reps)) * signs[None, :]


def _overlap_T(S, n_cmp_pad):
    n_cmp = (S - L_CMP) // CMP_STRIDE + 1
    n_sel = S // L_SEL
    cs = np.arange(n_cmp_pad) * CMP_STRIDE
    ss = np.arange(n_sel) * L_SEL
    ov = (cs[None, :] < ss[:, None] + L_SEL) & (cs[None, :] + L_CMP > ss[:, None]) & (np.arange(n_cmp_pad)[None, :] < n_cmp)
    return jnp.asarray(ov, dtype=BF16)


def _compress_params(pos, w1, b1, w2, b2):
    half = L_CMP // 2
    w1r = w1.reshape(2, L_CMP, HEAD_DIM, CMP_HIDDEN)

    def spread(w):
        z = jnp.zeros((2, N_KV_GROUPS, half, N_KV_GROUPS, HEAD_DIM, CMP_HIDDEN), F32)
        for g in range(N_KV_GROUPS):
            z = z.at[:, g, :, g].set(w)
        return z.reshape(2, N_KV_GROUPS, half * LANES, CMP_HIDDEN).astype(BF16)

    def spread_pos(p):
        return jnp.tile(p[:, :, None, :], (1, 1, N_KV_GROUPS, 1)).reshape(2, 1, half * LANES)

    w2p = jnp.zeros((2, N_KV_GROUPS, CMP_HIDDEN, N_KV_GROUPS, HEAD_DIM), F32)
    for g in range(N_KV_GROUPS):
        w2p = w2p.at[:, g, :, g].set(w2)
    return (spread(w1r[:, :half]), spread(w1r[:, half:]), spread_pos(pos[:, :half]), spread_pos(pos[:, half:]),
            b1[:, None, :], w2p.reshape(2, N_KV_GROUPS, CMP_HIDDEN, LANES).astype(BF16),
            jnp.tile(b2, (1, N_KV_GROUPS))[:, None, :])


def _pad_cols(w, width):
    return jnp.pad(w, ((0, 0), (0, width - w.shape[1])))


def kernel(x, mem, attn_norm, ffn_norm, mem_norm, w_mem_kv, w_o, w_up, conv_w, conv_b, w_down, a_w_in, a_gate_b,
           a_cmp_pos, a_cmp_w1, a_cmp_b1, a_cmp_w2, a_cmp_b2, b_w_in, kv_norm, w_kv_shared, b_fgate, final_norm):
    B, S, _ = x.shape
    assert S % FOX_TQ == 0 and S % (CMP_STRIDE * 8) == 0
    n_chunks = S // CMP_STRIDE
    cosT, sinT, rc, rs = _rope_tables(jnp.arange(S))
    _, _, rc_cmp, rs_cmp = _rope_tables(jnp.arange(n_chunks) * CMP_STRIDE + L_CMP - 1)
    ovT = _overlap_T(S, n_chunks)
    row = lambda v: v.reshape(1, -1).astype(F32)
    k_sh = vT_sh = dcum = dcumT = None
    for l in range(DEPTH):
        kmem, vmemT = _mem_kv(mem, row(mem_norm[l]), w_mem_kv[l].astype(BF16))
        if l < N_A:
            w = a_w_in[l]
            wq, wkv = w[:, :MIX_W].astype(BF16), w[:, MIX_W:2 * MIX_W].astype(BF16)
            wg = _pad_cols(w[:, 2 * MIX_W:2 * MIX_W + GATE_W], LANES).astype(BF16)
            gb = _pad_cols(a_gate_b[l][None, :], LANES)
            wqm = w[:, 2 * MIX_W + GATE_W:].astype(BF16)
            qT, kcmp, vcmp, kslc, kwin, vslcT, vwinT, gT, qmT = _proj_nsa(
                x, row(attn_norm[l]), wq, wkv, wg, gb, wqm, cosT, sinT, rc, rs)
            kc, vcT = _compress(kcmp.reshape(B, n_chunks, CMP_STRIDE * LANES), vcmp.reshape(B, n_chunks, CMP_STRIDE * LANES),
                                *_compress_params(a_cmp_pos[l], a_cmp_w1[l], a_cmp_b1[l], a_cmp_w2[l], a_cmp_b2[l]),
                                rc_cmp, rs_cmp)
            omix = _nsa_attn(qT, gT, kc, vcT, kslc, kwin, vslcT, vwinT, ovT)
        else:
            if l == N_A:
                wf = _pad_cols(w_kv_shared[:, 2 * MIX_W:], LANES).astype(BF16)
                k_sh, vT_sh, dcum, dcumT = _kv_shared(
                    x, row(kv_norm), w_kv_shared[:, :MIX_W].astype(BF16), w_kv_shared[:, MIX_W:2 * MIX_W].astype(BF16),
                    wf, _pad_cols(b_fgate[None, :].astype(F32), LANES))
            w = b_w_in[l - N_A]
            qT, qmT = _proj_fox(x, row(attn_norm[l]), w[:, :MIX_W].astype(BF16), w[:, MIX_W:].astype(BF16))
            omix = _fox_attn(qT, k_sh, vT_sh, dcum, dcumT)
        omem = _mem_attn(qmT, kmem, vmemT)
        x = _ffn(x, omix, omem, w_o[l][:MIX_W].astype(BF16), w_o[l][MIX_W:].astype(BF16), row(ffn_norm[l]),
                 w_up[l][:, :D_FF].astype(BF16), w_up[l][:, D_FF:].astype(BF16),
                 conv_w[l][:, :D_FF], conv_w[l][:, D_FF:], conv_b[l][None, :D_FF], conv_b[l][None, D_FF:],
                 w_down[l].astype(BF16), row(final_norm), final=(l == DEPTH - 1))
    return x
```

```python
import functools

import numpy as np
import jax
import jax.numpy as jnp
from jax import lax
from jax.experimental import pallas as pl
from jax.experimental.pallas import tpu as pltpu

D_MODEL = 1024
HEAD_DIM = 64
N_MIX_HEADS = 12
N_KV_GROUPS = 2
HEADS_PER_GROUP = N_MIX_HEADS // N_KV_GROUPS
N_MEM_HEADS = 4
L_CMP = 32
CMP_STRIDE = 16
CMP_HIDDEN = 256
L_SEL = 64
TOP_N = 16
WINDOW = 512
D_FF = 2816
CONV_WIDTH = 3
ROPE_THETA = 10000.0
EPS = 1e-6
NEG = -1e30
FORCE_BONUS = 1e4
MIX_W = N_MIX_HEADS * HEAD_DIM
MEM_W = N_MEM_HEADS * HEAD_DIM
GATE_W = 3 * N_MIX_HEADS
N_A = 2
DEPTH = 4
Q_SCALE = HEAD_DIM ** -0.5

LANES = 128
TM = 256
QB = 128
NSA_KC = 256
MASKED = 2.0 * NEG
FOX_TQ = 512
FOX_KC = 256
MEM_TQ = 256
FF_CHUNK = 256
GATE_ROWS = 40
VMEM_LIMIT = 56 * 1024 * 1024

F32 = jnp.float32
BF16 = jnp.bfloat16


def _params(sem):
    return pltpu.CompilerParams(dimension_semantics=sem, vmem_limit_bytes=VMEM_LIMIT)


def _const_spec(shape):
    n = len(shape)
    return pl.BlockSpec(shape, lambda *_: (0,) * n, pipeline_mode=pl.Buffered(1))


def _rms(x, g):
    return x * lax.rsqrt(jnp.mean(x * x, axis=-1, keepdims=True) + EPS) * g


def _dot(a, b):
    return jnp.dot(a, b, preferred_element_type=F32)


def _rope_lanes(x, cos, sin_signed):
    lane = lax.broadcasted_iota(jnp.int32, x.shape, 1)
    first_half = (lane & (HEAD_DIM - 1)) < (HEAD_DIM // 2)
    partner = jnp.where(first_half, pltpu.roll(x, LANES - HEAD_DIM // 2, 1), pltpu.roll(x, HEAD_DIM // 2, 1))
    return x * cos + partner * sin_signed


def _split3(x):
    a = x.astype(BF16)
    r = x - a.astype(F32)
    b = r.astype(BF16)
    c = (r - b.astype(F32)).astype(BF16)
    return a, b, c


def _proj_nsa_kernel(x_ref, g_ref, wq_ref, wkv_ref, wg_ref, gb_ref, wqm_ref, cosT_ref, sinT_ref, rc_ref, rs_ref,
                     qT_ref, kcmp_ref, vcmp_ref, kslc_ref, kwin_ref, vslcT_ref, vwinT_ref, gT_ref, qmT_ref):
    h = _rms(x_ref[0], g_ref[...]).astype(BF16)
    qT = (_dot(h, wq_ref[...]) * Q_SCALE).T
    cos = cosT_ref[...]
    sin = sinT_ref[...]
    half = HEAD_DIM // 2
    for hh in range(N_MIX_HEADS):
        x1 = qT[hh * HEAD_DIM:hh * HEAD_DIM + half]
        x2 = qT[hh * HEAD_DIM + half:(hh + 1) * HEAD_DIM]
        qT_ref[0, hh * HEAD_DIM:hh * HEAD_DIM + half, :] = (x1 * cos - x2 * sin).astype(BF16)
        qT_ref[0, hh * HEAD_DIM + half:(hh + 1) * HEAD_DIM, :] = (x2 * cos + x1 * sin).astype(BF16)
    kv = _dot(h, wkv_ref[...])
    rc = rc_ref[...]
    rs = rs_ref[...]
    kcmp_ref[0] = kv[:, 0:128]
    vcmp_ref[0] = kv[:, 128:256]
    kslc_ref[0] = _rope_lanes(kv[:, 256:384], rc, rs).astype(BF16)
    kwin_ref[0] = _rope_lanes(kv[:, 512:640], rc, rs).astype(BF16)
    vslcT_ref[0, 0] = kv[:, 384:512].T.astype(BF16)
    vwinT_ref[0, 0] = kv[:, 640:768].T.astype(BF16)
    gates = jax.nn.sigmoid(_dot(h, wg_ref[...]) + gb_ref[...])
    gT_ref[0] = gates.T[0:GATE_ROWS]
    qmT_ref[0] = (_dot(h, wqm_ref[...]) * Q_SCALE).T.astype(BF16)


def _proj_nsa(x, g, wq, wkv, wg, gb, wqm, cosT, sinT, rc, rs):
    B, S, _ = x.shape
    nt = S // TM
    tok = lambda w: pl.BlockSpec((1, TM, w), lambda b, j: (b, j, 0))
    featT = lambda r: pl.BlockSpec((1, r, TM), lambda b, j: (b, 0, j))
    assert TM == NSA_KC
    vT = pl.BlockSpec((1, 1, LANES, NSA_KC), lambda b, j: (b, j, 0, 0))
    return pl.pallas_call(
        _proj_nsa_kernel,
        grid=(B, nt),
        in_specs=[tok(D_MODEL), _const_spec((1, D_MODEL)), _const_spec(wq.shape), _const_spec(wkv.shape),
                  _const_spec(wg.shape), _const_spec(gb.shape), _const_spec(wqm.shape),
                  pl.BlockSpec((HEAD_DIM // 2, TM), lambda b, j: (0, j)),
                  pl.BlockSpec((HEAD_DIM // 2, TM), lambda b, j: (0, j)),
                  pl.BlockSpec((TM, LANES), lambda b, j: (j, 0)),
                  pl.BlockSpec((TM, LANES), lambda b, j: (j, 0))],
        out_specs=[featT(MIX_W), tok(LANES), tok(LANES), tok(LANES), tok(LANES), vT, vT, featT(GATE_ROWS),
                   featT(MEM_W)],
        out_shape=[jax.ShapeDtypeStruct((B, MIX_W, S), BF16),
                   jax.ShapeDtypeStruct((B, S, LANES), F32), jax.ShapeDtypeStruct((B, S, LANES), F32),
                   jax.ShapeDtypeStruct((B, S, LANES), BF16), jax.ShapeDtypeStruct((B, S, LANES), BF16),
                   jax.ShapeDtypeStruct((B, S // NSA_KC, LANES, NSA_KC), BF16),
                   jax.ShapeDtypeStruct((B, S // NSA_KC, LANES, NSA_KC), BF16),
                   jax.ShapeDtypeStruct((B, GATE_ROWS, S), F32),
                   jax.ShapeDtypeStruct((B, MEM_W, S), BF16)],
        compiler_params=_params(("arbitrary", "arbitrary")),
        name="proj_nsa",
    )(x, g, wq, wkv, wg, gb, wqm, cosT, sinT, rc, rs)


def _proj_fox_kernel(x_ref, g_ref, wq_ref, wqm_ref, qT_ref, qmT_ref):
    h = _rms(x_ref[0], g_ref[...]).astype(BF16)
    qT_ref[0] = (_dot(h, wq_ref[...]) * Q_SCALE).T.astype(BF16)
    qmT_ref[0] = (_dot(h, wqm_ref[...]) * Q_SCALE).T.astype(BF16)


def _proj_fox(x, g, wq, wqm):
    B, S, _ = x.shape
    featT = lambda r: pl.BlockSpec((1, r, TM), lambda b, j: (b, 0, j))
    return pl.pallas_call(
        _proj_fox_kernel,
        grid=(B, S // TM),
        in_specs=[pl.BlockSpec((1, TM, D_MODEL), lambda b, j: (b, j, 0)), _const_spec((1, D_MODEL)),
                  _const_spec(wq.shape), _const_spec(wqm.shape)],
        out_specs=[featT(MIX_W), featT(MEM_W)],
        out_shape=[jax.ShapeDtypeStruct((B, MIX_W, S), BF16), jax.ShapeDtypeStruct((B, MEM_W, S), BF16)],
        compiler_params=_params(("arbitrary", "arbitrary")),
        name="proj_fox",
    )(x, g, wq, wqm)


def _kv_shared_kernel(x_ref, g_ref, wk_ref, wv_ref, wf_ref, bf_ref, k_ref, vT_ref, dcum_ref, dcumT_ref, carry_ref):
    j = pl.program_id(1)
    h = _rms(x_ref[0], g_ref[...]).astype(BF16)
    k_ref[0] = _dot(h, wk_ref[...]).astype(BF16)
    vT_ref[0, 0] = _dot(h, wv_ref[...]).T.astype(BF16)
    z = _dot(h, wf_ref[...]) + bf_ref[...]
    logf = jnp.minimum(z, 0.0) - jnp.log1p(jnp.exp(-jnp.abs(z)))
    row = lax.broadcasted_iota(jnp.int32, (TM, TM), 0)
    col = lax.broadcasted_iota(jnp.int32, (TM, TM), 1)
    tri = (col <= row).astype(BF16)
    a, b, c = _split3(logf)
    @pl.when(j == 0)
    def _():
        carry_ref[...] = jnp.zeros(carry_ref.shape, F32)

    cs = _dot(tri, a) + _dot(tri, b) + _dot(tri, c) + carry_ref[7:8, :]
    carry_ref[...] = cs[TM - 8:TM]
    dcum_ref[0] = cs
    dcumT_ref[0] = cs.T[0:16]


def _kv_shared(x, g, wk, wv, wf, bf):
    B, S, _ = x.shape
    assert TM == FOX_KC
    return pl.pallas_call(
        _kv_shared_kernel,
        grid=(B, S // TM),
        in_specs=[pl.BlockSpec((1, TM, D_MODEL), lambda b, j: (b, j, 0)), _const_spec((1, D_MODEL)),
                  _const_spec(wk.shape), _const_spec(wv.shape), _const_spec(wf.shape), _const_spec(bf.shape)],
        out_specs=[pl.BlockSpec((1, TM, MIX_W), lambda b, j: (b, j, 0)),
                   pl.BlockSpec((1, 1, MIX_W, FOX_KC), lambda b, j: (b, j, 0, 0)),
                   pl.BlockSpec((1, TM, LANES), lambda b, j: (b, j, 0)),
                   pl.BlockSpec((1, 16, TM), lambda b, j: (b, 0, j))],
        out_shape=[jax.ShapeDtypeStruct((B, S, MIX_W), BF16),
                   jax.ShapeDtypeStruct((B, S // FOX_KC, MIX_W, FOX_KC), BF16),
                   jax.ShapeDtypeStruct((B, S, LANES), F32),
                   jax.ShapeDtypeStruct((B, 16, S), F32)],
        scratch_shapes=[pltpu.VMEM((8, LANES), F32)],
        compiler_params=_params(("arbitrary", "arbitrary")),
        name="kv_shared",
    )(x, g, wk, wv, wf, bf)


def _mem_kv_kernel(m_ref, g_ref, w_ref, k_ref, vT_ref):
    h = _rms(m_ref[0], g_ref[...]).astype(BF16)
    kv = _dot(h, w_ref[...])
    k_ref[0] = kv[:, 0:MEM_W].astype(BF16)
    vT_ref[0] = kv[:, MEM_W:2 * MEM_W].T.astype(BF16)


def _mem_kv(mem, g, w):
    B, M, _ = mem.shape
    return pl.pallas_call(
        _mem_kv_kernel,
        grid=(B,),
        in_specs=[pl.BlockSpec((1, M, D_MODEL), lambda b: (b, 0, 0)), _const_spec((1, D_MODEL)),
                  _const_spec(w.shape)],
        out_specs=[pl.BlockSpec((1, M, MEM_W), lambda b: (b, 0, 0)), pl.BlockSpec((1, MEM_W, M), lambda b: (b, 0, 0))],
        out_shape=[jax.ShapeDtypeStruct((B, M, MEM_W), BF16), jax.ShapeDtypeStruct((B, MEM_W, M), BF16)],
        compiler_params=_params(("arbitrary",)),
        name="mem_kv",
    )(mem, g, w)


def _compress_kernel(kx_ref, vx_ref, wtop_ref, wbot_ref, ptop_ref, pbot_ref, b1_ref, w2_ref, b2_ref, rc_ref, rs_ref,
                     kc_ref, vcT_ref):
    for kvi, x_ref in enumerate((kx_ref, vx_ref)):
        x = x_ref[0]
        xt = (x + ptop_ref[kvi]).astype(BF16)
        xb = (x + pbot_ref[kvi]).astype(BF16)
        out = b2_ref[kvi]
        for g in range(N_KV_GROUPS):
            top = _dot(xt, wtop_ref[kvi, g])
            bot = _dot(xb, wbot_ref[kvi, g])
            hid = top + pltpu.roll(bot, bot.shape[0] - 1, 0) + b1_ref[kvi]
            out = out + _dot(jax.nn.gelu(hid).astype(BF16), w2_ref[kvi, g])
        if kvi == 0:
            kc_ref[0] = _rope_lanes(out, rc_ref[...], rs_ref[...]).astype(BF16)
        else:
            vcT_ref[0] = out.T.astype(BF16)


def _compress(kx, vx, wtop, wbot, ptop, pbot, b1, w2, b2, rc, rs):
    B, NC, W = kx.shape
    xs = pl.BlockSpec((1, NC, W), lambda b: (b, 0, 0))
    return pl.pallas_call(
        _compress_kernel,
        grid=(B,),
        in_specs=[xs, xs] + [_const_spec(a.shape) for a in (wtop, wbot, ptop, pbot, b1, w2, b2, rc, rs)],
        out_specs=[pl.BlockSpec((1, NC, LANES), lambda b: (b, 0, 0)), pl.BlockSpec((1, LANES, NC), lambda b: (b, 0, 0))],
        out_shape=[jax.ShapeDtypeStruct((B, NC, LANES), BF16), jax.ShapeDtypeStruct((B, LANES, NC), BF16)],
        compiler_params=_params(("arbitrary",)),
        name="nsa_compress",
    )(kx, vx, wtop, wbot, ptop, pbot, b1, w2, b2, rc, rs)


def _online_tile(s, m_ref, l_ref, lead, cols):
    m_old = m_ref[lead, :, cols]
    m_new = jnp.maximum(m_old, jnp.max(s, axis=0, keepdims=True))
    alpha = jnp.exp(m_old - m_new)
    e = jnp.exp(s - m_new)
    l_ref[lead, :, cols] = alpha * l_ref[lead, :, cols] + jnp.sum(e, axis=0, keepdims=True)
    m_ref[lead, :, cols] = m_new
    return e.astype(BF16), alpha


def _reset_state(m_ref, l_ref, acc_ref):
    m_ref[...] = jnp.full(m_ref.shape, NEG, F32)
    l_ref[...] = jnp.zeros(l_ref.shape, F32)
    acc_ref[...] = jnp.zeros(acc_ref.shape, F32)


def _nsa_attn_kernel(qT_ref, gT_ref, kc_ref, vcT_ref, kslc_ref, kwin_ref, vslcT_ref, vwinT_ref, ovT_ref,
                     o_ref, sel_ref, m_ref, l_ref, acc_ref):
    c = pl.program_id(1)
    n_sel = sel_ref.shape[1]
    n_cmp = kc_ref.shape[1]
    GQ = HEADS_PER_GROUP * QB
    GW = HEADS_PER_GROUP * HEAD_DIM
    t_row = c * QB + lax.broadcasted_iota(jnp.int32, (1, QB), 1)
    gates = gT_ref[0]

    q6 = [jnp.concatenate([qT_ref[0, (HEADS_PER_GROUP * g + hh) * HEAD_DIM:(HEADS_PER_GROUP * g + hh + 1) * HEAD_DIM, :]
                           for hh in range(HEADS_PER_GROUP)], axis=1) for g in range(N_KV_GROUPS)]
    z = jnp.zeros_like(q6[0])
    qbd = jnp.concatenate([jnp.concatenate([q6[0], z], axis=1), jnp.concatenate([z, q6[1]], axis=1)], axis=0)

    cmp_end = lax.broadcasted_iota(jnp.int32, (n_cmp, QB), 0) * CMP_STRIDE + (L_CMP - 1)
    valid_c = cmp_end <= t_row
    valid_cf = valid_c.astype(F32)
    s_cmp = _dot(kc_ref[0], qbd)
    j_idx = lax.broadcasted_iota(jnp.int32, (n_sel, QB), 0)
    blk_t = lax.shift_right_logical(t_row, 6)
    forced = ((j_idx == 0) | (j_idx == blk_t) | (j_idx == blk_t - 1)).astype(F32)
    o_cmp = []
    for g in range(N_KV_GROUPS):
        psum = jnp.zeros((n_cmp, QB), F32)
        ps = []
        for hh in range(HEADS_PER_GROUP):
            col = g * GQ + hh * QB
            sh = jnp.where(valid_c, s_cmp[:, col:col + QB], NEG)
            e = jnp.exp(sh - jnp.max(sh, axis=0, keepdims=True)) * valid_cf
            p = e / jnp.maximum(jnp.sum(e, axis=0, keepdims=True), 1e-30)
            psum = psum + p
            ps.append(p.astype(BF16))
        o_cmp.append(_dot(vcT_ref[0, g * HEAD_DIM:(g + 1) * HEAD_DIM, :], jnp.concatenate(ps, axis=1)))
        p_hi = psum.astype(BF16)
        p_lo = (psum - p_hi.astype(F32)).astype(BF16)
        imp = _dot(ovT_ref[...], p_hi) + _dot(ovT_ref[...], p_lo)
        score = jnp.where(j_idx <= blk_t, imp + FORCE_BONUS * forced, NEG)
        cnt = jnp.zeros((n_sel, QB), F32)
        for i in range(n_sel):
            si = score[i:i + 1, :]
            cnt = cnt + ((si > score) | ((si == score) & (j_idx > i))).astype(F32)
        sel = (cnt < float(min(TOP_N, n_sel))).astype(F32)
        for i in range(n_sel):
            sel_ref[g, i] = jnp.broadcast_to(sel[i:i + 1, :], (8, QB))

    _reset_state(m_ref, l_ref, acc_ref)
    key_off = lax.broadcasted_iota(jnp.int32, (NSA_KC, QB), 0)

    def sweep(br, k_ref, vT_ref, kb, masks):
        s = _dot(k_ref[0, pl.ds(pl.multiple_of(kb * NSA_KC, NSA_KC), NSA_KC), :], qbd)
        for g in range(N_KV_GROUPS):
            ps, alphas = [], []
            for hh in range(HEADS_PER_GROUP):
                col = g * GQ + hh * QB
                p, a = _online_tile(jnp.where(masks[g], s[:, col:col + QB], MASKED), m_ref, l_ref, br,
                                    slice(col, col + QB))
                ps.append(p)
                alphas.append(a)
            gc = slice(g * GQ, (g + 1) * GQ)
            acc_ref[br, :, gc] = (acc_ref[br, :, gc] * jnp.concatenate(alphas, axis=1)
                                  + _dot(vT_ref[0, kb, g * HEAD_DIM:(g + 1) * HEAD_DIM, :], jnp.concatenate(ps, axis=1)))

    def slc_masks(kb, causal):
        per_chunk = NSA_KC // L_SEL
        masks = []
        for g in range(N_KV_GROUPS):
            picked = sel_ref[g, per_chunk * kb + per_chunk - 1][0:1, :]
            for i in reversed(range(per_chunk - 1)):
                picked = jnp.where(key_off < (i + 1) * L_SEL, sel_ref[g, per_chunk * kb + i][0:1, :], picked)
            mk = picked > 0.5
            masks.append(mk & (kb * NSA_KC + key_off <= t_row) if causal else mk)
        return masks

    last = lax.shift_right_logical(c * QB, NSA_KC.bit_length() - 1)
    first_win = jnp.maximum(last - WINDOW // NSA_KC, 0)

    def far_body(kb, carry):
        sweep(0, kslc_ref, vslcT_ref, kb, slc_masks(kb, False))
        return carry

    def near_body(kb, carry):
        sweep(0, kslc_ref, vslcT_ref, kb, slc_masks(kb, True))
        key_pos = kb * NSA_KC + key_off
        wm = (key_pos <= t_row) & (t_row - key_pos < WINDOW)
        sweep(1, kwin_ref, vwinT_ref, kb, [wm, wm])
        return carry

    lax.fori_loop(0, first_win, far_body, 0)
    lax.fori_loop(first_win, last + 1, near_body, 0)
    o_slc = acc_ref[0] / jnp.maximum(l_ref[0], 1e-30)
    o_win = acc_ref[1] / jnp.maximum(l_ref[1], 1e-30)

    for g in range(N_KV_GROUPS):
        outs = []
        for hh in range(HEADS_PER_GROUP):
            r = 3 * (HEADS_PER_GROUP * g + hh)
            cols = slice(g * GQ + hh * QB, g * GQ + (hh + 1) * QB)
            outs.append(gates[r:r + 1] * o_cmp[g][:, hh * QB:(hh + 1) * QB] + gates[r + 1:r + 2] * o_slc[:, cols]
                        + gates[r + 2:r + 3] * o_win[:, cols])
        o_ref[0, :, g * GW:(g + 1) * GW] = jnp.concatenate(outs, axis=0).T.astype(BF16)


def _nsa_attn(qT, gT, kc, vcT, kslc, kwin, vslcT, vwinT, ovT):
    B, _, S = qT.shape
    n_sel = S // L_SEL
    W12 = N_MIX_HEADS * QB
    per_b3 = lambda a: pl.BlockSpec((1,) + a.shape[1:], lambda b, c: (b, 0, 0))
    per_b4 = lambda a: pl.BlockSpec((1,) + a.shape[1:], lambda b, c: (b, 0, 0, 0))
    return pl.pallas_call(
        _nsa_attn_kernel,
        grid=(B, S // QB),
        in_specs=[pl.BlockSpec((1, MIX_W, QB), lambda b, c: (b, 0, c)),
                  pl.BlockSpec((1, GATE_ROWS, QB), lambda b, c: (b, 0, c)),
                  per_b3(kc), per_b3(vcT), per_b3(kslc), per_b3(kwin), per_b4(vslcT), per_b4(vwinT),
                  _const_spec(ovT.shape)],
        out_specs=pl.BlockSpec((1, QB, MIX_W), lambda b, c: (b, c, 0)),
        out_shape=jax.ShapeDtypeStruct((B, S, MIX_W), BF16),
        scratch_shapes=[pltpu.VMEM((N_KV_GROUPS, n_sel, 8, QB), F32), pltpu.VMEM((2, 1, W12), F32),
                        pltpu.VMEM((2, 1, W12), F32), pltpu.VMEM((2, HEAD_DIM, W12), F32)],
        compiler_params=_params(("arbitrary", "arbitrary")),
        name="nsa_attn",
    )(qT, gT, kc, vcT, kslc, kwin, vslcT, vwinT, ovT)


def _pair_q(q2, width):
    z = jnp.zeros((HEAD_DIM, width), q2.dtype)
    return jnp.concatenate([jnp.concatenate([q2[0:HEAD_DIM], z], axis=1),
                            jnp.concatenate([z, q2[HEAD_DIM:2 * HEAD_DIM]], axis=1)], axis=0)


def _fox_attn_kernel(qT_ref, k_ref, vT_ref, dcum_ref, dcumT_ref, o_ref, kbias_ref, m_ref, l_ref, acc_ref):
    p = pl.program_id(1)
    c = pl.program_id(2)
    nq = FOX_TQ // QB
    per_q = FOX_TQ // FOX_KC

    @pl.when(c == 0)
    def _():
        parts = _split3(dcum_ref[0])
        src = lax.broadcasted_iota(jnp.int32, (LANES, LANES), 0)
        dst = lax.broadcasted_iota(jnp.int32, (LANES, LANES), 1)
        lane = lax.broadcasted_iota(jnp.int32, (1, LANES), 1)
        acc = ((lane >= 3) & (lane < 6)).astype(F32)
        for r in range(3):
            place = ((src == 2 * p) & (dst == r)) | ((src == 2 * p + 1) & (dst == 6 + r))
            acc = acc + _dot(parts[r], place.astype(BF16))
        kbias_ref[...] = acc.astype(BF16)

    rowi = lax.broadcasted_iota(jnp.int32, (16, 2 * FOX_TQ), 0)
    head1 = lax.broadcasted_iota(jnp.int32, (16, 2 * FOX_TQ), 1) >= FOX_TQ
    bias = jnp.where(((rowi < 3) & ~head1) | ((rowi >= 6) & (rowi < 9) & head1), -1.0, 0.0)
    dq = [_split3(dcumT_ref[0, pl.ds(2 * p + hh, 1), :]) for hh in range(2)]
    for r in range(3):
        bias = jnp.where(rowi == 3 + r, jnp.concatenate([dq[0][r], dq[1][r]], axis=1).astype(F32), bias)
    qaug = jnp.concatenate([_pair_q(qT_ref[0], FOX_TQ), bias.astype(BF16),
                            jnp.zeros((LANES - 16, 2 * FOX_TQ), BF16)], axis=0)

    _reset_state(m_ref, l_ref, acc_ref)
    lane_t = lax.broadcasted_iota(jnp.int32, (1, QB), 1)

    def sweep(kb, n_kc, qs_lo, masked):
        size = n_kc * FOX_KC
        start = pl.multiple_of(kb * FOX_KC, FOX_KC)
        kk = jnp.concatenate([k_ref[0, pl.ds(start, size), :], kbias_ref[pl.ds(start, size), :]], axis=1)
        if masked:
            key_pos = kb * FOX_KC + lax.broadcasted_iota(jnp.int32, (size, QB), 0)
        live = FOX_TQ - qs_lo * QB
        if qs_lo == 0:
            s2 = _dot(kk, qaug)
            s = [s2[:, 0:FOX_TQ], s2[:, FOX_TQ:2 * FOX_TQ]]
        else:
            s = [_dot(kk, qaug[:, (hh + 1) * FOX_TQ - live:(hh + 1) * FOX_TQ]) for hh in range(2)]
        for hh in range(2):
            cols = slice((hh + 1) * FOX_TQ - live, (hh + 1) * FOX_TQ)
            ps, alphas = [], []
            for qs in range(qs_lo, nq):
                tile = s[hh][:, (qs - qs_lo) * QB:(qs - qs_lo + 1) * QB]
                if masked:
                    tile = jnp.where(key_pos <= c * FOX_TQ + qs * QB + lane_t, tile, MASKED)
                col = hh * FOX_TQ + qs * QB
                pr, a = _online_tile(tile, m_ref, l_ref, 0, slice(col, col + QB))
                ps.append(pr)
                alphas.append(a)
            vT = jnp.concatenate([vT_ref[0, kb + i, hh * HEAD_DIM:(hh + 1) * HEAD_DIM, :] for i in range(n_kc)], axis=1)
            acc_ref[0, :, cols] = (acc_ref[0, :, cols] * jnp.concatenate(alphas, axis=1)
                                   + _dot(vT, jnp.concatenate(ps, axis=1)))

    def full_body(i, carry):
        sweep(i * per_q, per_q, 0, False)
        return carry

    lax.fori_loop(0, c, full_body, 0)
    for i in range(per_q):
        sweep(c * per_q + i, 1, i * FOX_KC // QB, True)
    o = acc_ref[0] / jnp.maximum(l_ref[0], 1e-30)
    oT = jnp.concatenate([o[:, 0:FOX_TQ], o[:, FOX_TQ:2 * FOX_TQ]], axis=0)
    o_ref[0] = oT.T.astype(BF16)


def _fox_attn(qT, k, vT, dcum, dcumT):
    B, _, S = qT.shape
    n_pairs = N_MIX_HEADS // 2
    return pl.pallas_call(
        _fox_attn_kernel,
        grid=(B, n_pairs, S // FOX_TQ),
        in_specs=[pl.BlockSpec((1, LANES, FOX_TQ), lambda b, p, c: (b, p, c)),
                  pl.BlockSpec((1, S, LANES), lambda b, p, c: (b, 0, p)),
                  pl.BlockSpec((1, S // FOX_KC, LANES, FOX_KC), lambda b, p, c: (b, 0, p, 0)),
                  pl.BlockSpec((1, S, LANES), lambda b, p, c: (b, 0, 0)),
                  pl.BlockSpec((1, 16, FOX_TQ), lambda b, p, c: (b, 0, c))],
        out_specs=pl.BlockSpec((1, FOX_TQ, LANES), lambda b, p, c: (b, c, p)),
        out_shape=jax.ShapeDtypeStruct((B, S, MIX_W), BF16),
        scratch_shapes=[pltpu.VMEM((S, LANES), BF16), pltpu.VMEM((1, 1, 2 * FOX_TQ), F32),
                        pltpu.VMEM((1, 1, 2 * FOX_TQ), F32), pltpu.VMEM((1, HEAD_DIM, 2 * FOX_TQ), F32)],
        compiler_params=_params(("arbitrary", "arbitrary", "arbitrary")),
        name="fox_attn",
    )(qT, k, vT, dcum, dcumT)


def _mem_attn_kernel(qT_ref, k_ref, vT_ref, o_ref):
    outs = []
    for pr in range(N_MEM_HEADS // 2):
        qbd = _pair_q(qT_ref[0, pr * LANES:(pr + 1) * LANES, :], MEM_TQ)
        s = _dot(k_ref[0, :, pr * LANES:(pr + 1) * LANES], qbd)
        for hh in range(2):
            ps = []
            for qs in range(MEM_TQ // QB):
                col = hh * MEM_TQ + qs * QB
                sh = s[:, col:col + QB]
                e = jnp.exp(sh - jnp.max(sh, axis=0, keepdims=True))
                ps.append((e / jnp.sum(e, axis=0, keepdims=True)).astype(BF16))
            h = 2 * pr + hh
            outs.append(_dot(vT_ref[0, h * HEAD_DIM:(h + 1) * HEAD_DIM, :], jnp.concatenate(ps, axis=1)))
    o_ref[0] = jnp.concatenate(outs, axis=0).T.astype(BF16)


def _mem_attn(qmT, kmem, vmemT):
    B, _, S = qmT.shape
    M = kmem.shape[1]
    return pl.pallas_call(
        _mem_attn_kernel,
        grid=(B, S // MEM_TQ),
        in_specs=[pl.BlockSpec((1, MEM_W, MEM_TQ), lambda b, c: (b, 0, c)),
                  pl.BlockSpec((1, M, MEM_W), lambda b, c: (b, 0, 0)),
                  pl.BlockSpec((1, MEM_W, M), lambda b, c: (b, 0, 0))],
        out_specs=pl.BlockSpec((1, MEM_TQ, MEM_W), lambda b, c: (b, c, 0)),
        out_shape=jax.ShapeDtypeStruct((B, S, MEM_W), BF16),
        compiler_params=_params(("arbitrary", "arbitrary")),
        name="mem_attn",
    )(qmT, kmem, vmemT)


def _ffn_kernel(x_ref, omix_ref, omem_ref, wo1_ref, wo2_ref, g_ref, wa_ref, wb_ref, cwa_ref, cwb_ref, cba_ref,
                cbb_ref, wd_ref, gfin_ref, o_ref, ca_ref, cb_ref, *, final):
    j = pl.program_id(1)
    x1 = x_ref[0] + _dot(omix_ref[0], wo1_ref[...]) + _dot(omem_ref[0], wo2_ref[...])
    h = _rms(x1, g_ref[...]).astype(BF16)
    o_ref[0] = x1
    row = lax.broadcasted_iota(jnp.int32, (TM, FF_CHUNK), 0)

    @pl.when(j == 0)
    def _():
        ca_ref[...] = jnp.zeros(ca_ref.shape, F32)
        cb_ref[...] = jnp.zeros(cb_ref.shape, F32)

    def conv(u, prev8, w, bias):
        p1 = prev8[7:8]
        p2 = prev8[6:7]
        u1 = jnp.where(row == 0, p1, pltpu.roll(u, 1, 0))
        u2 = jnp.where(row == 0, p2, jnp.where(row == 1, p1, pltpu.roll(u, 2, 0)))
        return w[0:1] * u2 + w[1:2] * u1 + w[2:3] * u + bias

    for ci in range(D_FF // FF_CHUNK):
        sl = slice(ci * FF_CHUNK, (ci + 1) * FF_CHUNK)
        a = _dot(h, wa_ref[:, sl])
        b = _dot(h, wb_ref[:, sl])
        pa = ca_ref[:, sl]
        pb = cb_ref[:, sl]
        ca_ref[:, sl] = a[TM - 8:TM]
        cb_ref[:, sl] = b[TM - 8:TM]
        ac = conv(a, pa, cwa_ref[:, sl], cba_ref[:, sl])
        bc = conv(b, pb, cwb_ref[:, sl], cbb_ref[:, sl])
        gated = (ac * jax.nn.sigmoid(ac) * bc).astype(BF16)
        o_ref[0] += _dot(gated, wd_ref[sl, :])
    if final:
        o_ref[0] = _rms(o_ref[0], gfin_ref[...])


def _ffn(x, omix, omem, wo1, wo2, g, wa, wb, cwa, cwb, cba, cbb, wd, gfin, final):
    B, S, _ = x.shape
    tok = lambda w: pl.BlockSpec((1, TM, w), lambda b, j: (b, j, 0))
    consts = (wo1, wo2, g, wa, wb, cwa, cwb, cba, cbb, wd, gfin)
    return pl.pallas_call(
        functools.partial(_ffn_kernel, final=final),
        grid=(B, S // TM),
        in_specs=[tok(D_MODEL), tok(MIX_W), tok(MEM_W)] + [_const_spec(a.shape) for a in consts],
        out_specs=tok(D_MODEL),
        out_shape=jax.ShapeDtypeStruct((B, S, D_MODEL), F32),
        scratch_shapes=[pltpu.VMEM((8, D_FF), F32), pltpu.VMEM((8, D_FF), F32)],
        compiler_params=_params(("arbitrary", "arbitrary")),
        name="outproj_convffn",
    )(x, omix, omem, wo1, wo2, g, wa, wb, cwa, cwb, cba, cbb, wd, gfin)


def _rope_tables(pos):
    half = HEAD_DIM // 2
    inv = ROPE_THETA ** (-jnp.arange(half, dtype=F32) / half)
    ang = pos.astype(F32)[:, None] * inv[None, :]
    cos, sin = jnp.cos(ang), jnp.sin(ang)
    reps = LANES // half
    signs = jnp.tile(jnp.concatenate([-jnp.ones((half,), F32), jnp.ones((half,), F32)]), LANES // HEAD_DIM)
    return cos.T, sin.T, jnp.tile(cos, (1, reps)), jnp.tile(sin, (1, reps)) * signs[None, :]


def _overlap_T(S, n_cmp_pad):
    n_cmp = (S - L_CMP) // CMP_STRIDE + 1
    n_sel = S // L_SEL
    cs = np.arange(n_cmp_pad) * CMP_STRIDE
    ss = np.arange(n_sel) * L_SEL
    ov = (cs[None, :] < ss[:, None] + L_SEL) & (cs[None, :] + L_CMP > ss[:, None]) & (np.arange(n_cmp_pad)[None, :] < n_cmp)
    return jnp.asarray(ov, dtype=BF16)


def _compress_params(pos, w1, b1, w2, b2):
    half = L_CMP // 2
    w1r = w1.reshape(2, L_CMP, HEAD_DIM, CMP_HIDDEN)

    def spread(w):
        z = jnp.zeros((2, N_KV_GROUPS, half, N_KV_GROUPS, HEAD_DIM, CMP_HIDDEN), F32)
        for g in range(N_KV_GROUPS):
            z = z.at[:, g, :, g].set(w)
        return z.reshape(2, N_KV_GROUPS, half * LANES, CMP_HIDDEN).astype(BF16)

    def spread_pos(p):
        return jnp.tile(p[:, :, None, :], (1, 1, N_KV_GROUPS, 1)).reshape(2, 1, half * LANES)

    w2p = jnp.zeros((2, N_KV_GROUPS, CMP_HIDDEN, N_KV_GROUPS, HEAD_DIM), F32)
    for g in range(N_KV_GROUPS):
        w2p = w2p.at[:, g, :, g].set(w2)
    return (spread(w1r[:, :half]), spread(w1r[:, half:]), spread_pos(pos[:, :half]), spread_pos(pos[:, half:]),
            b1[:, None, :], w2p.reshape(2, N_KV_GROUPS, CMP_HIDDEN, LANES).astype(BF16),
            jnp.tile(b2, (1, N_KV_GROUPS))[:, None, :])


def _pad_cols(w, width):
    return jnp.pad(w, ((0, 0), (0, width - w.shape[1])))


def kernel(x, mem, attn_norm, ffn_norm, mem_norm, w_mem_kv, w_o, w_up, conv_w, conv_b, w_down, a_w_in, a_gate_b,
           a_cmp_pos, a_cmp_w1, a_cmp_b1, a_cmp_w2, a_cmp_b2, b_w_in, kv_norm, w_kv_shared, b_fgate, final_norm):
    B, S, _ = x.shape
    assert S % FOX_TQ == 0 and S % (CMP_STRIDE * 8) == 0
    n_chunks = S // CMP_STRIDE
    cosT, sinT, rc, rs = _rope_tables(jnp.arange(S))
    _, _, rc_cmp, rs_cmp = _rope_tables(jnp.arange(n_chunks) * CMP_STRIDE + L_CMP - 1)
    ovT = _overlap_T(S, n_chunks)
    row = lambda v: v.reshape(1, -1).astype(F32)
    k_sh = vT_sh = dcum = dcumT = None
    for l in range(DEPTH):
        kmem, vmemT = _mem_kv(mem, row(mem_norm[l]), w_mem_kv[l].astype(BF16))
        if l < N_A:
            w = a_w_in[l]
            wq, wkv = w[:, :MIX_W].astype(BF16), w[:, MIX_W:2 * MIX_W].astype(BF16)
            wg = _pad_cols(w[:, 2 * MIX_W:2 * MIX_W + GATE_W], LANES).astype(BF16)
            gb = _pad_cols(a_gate_b[l][None, :], LANES)
            wqm = w[:, 2 * MIX_W + GATE_W:].astype(BF16)
            qT, kcmp, vcmp, kslc, kwin, vslcT, vwinT, gT, qmT = _proj_nsa(
                x, row(attn_norm[l]), wq, wkv, wg, gb, wqm, cosT, sinT, rc, rs)
            kc, vcT = _compress(kcmp.reshape(B, n_chunks, CMP_STRIDE * LANES), vcmp.reshape(B, n_chunks, CMP_STRIDE * LANES),
                                *_compress_params(a_cmp_pos[l], a_cmp_w1[l], a_cmp_b1[l], a_cmp_w2[l], a_cmp_b2[l]),
                                rc_cmp, rs_cmp)
            omix = _nsa_attn(qT, gT, kc, vcT, kslc, kwin, vslcT, vwinT, ovT)
        else:
            if l == N_A:
                wf = _pad_cols(w_kv_shared[:, 2 * MIX_W:], LANES).astype(BF16)
                k_sh, vT_sh, dcum, dcumT = _kv_shared(
                    x, row(kv_norm), w_kv_shared[:, :MIX_W].astype(BF16), w_kv_shared[:, MIX_W:2 * MIX_W].astype(BF16),
                    wf, _pad_cols(b_fgate[None, :].astype(F32), LANES))
            w = b_w_in[l - N_A]
            qT, qmT = _proj_fox(x, row(attn_norm[l]), w[:, :MIX_W].astype(BF16), w[:, MIX_W:].astype(BF16))
            omix = _fox_attn(qT, k_sh, vT_sh, dcum, dcumT)
        omem = _mem_attn(qmT, kmem, vmemT)
        x = _ffn(x, omix, omem, w_o[l][:MIX_W].astype(BF16), w_o[l][MIX_W:].astype(BF16), row(ffn_norm[l]),
                 w_up[l][:, :D_FF].astype(BF16), w_up[l][:, D_FF:].astype(BF16),
                 conv_w[l][:, :D_FF], conv_w[l][:, D_FF:], conv_b[l][None, :D_FF], conv_b[l][None, D_FF:],
                 w_down[l].astype(BF16), row(final_norm), final=(l == DEPTH - 1))
    return x
```

```python
import functools

import numpy as np
import jax
import jax.numpy as jnp
from jax import lax
from jax.experimental import pallas as pl
from jax.experimental.pallas import tpu as pltpu

D_MODEL = 1024
HEAD_DIM = 64
N_MIX_HEADS = 12
N_KV_GROUPS = 2
HEADS_PER_GROUP = N_MIX_HEADS // N_KV_GROUPS
N_MEM_HEADS = 4
L_CMP = 32
CMP_STRIDE = 16
CMP_HIDDEN = 256
L_SEL = 64
TOP_N = 16
WINDOW = 512
D_FF = 2816
CONV_WIDTH = 3
ROPE_THETA = 10000.0
EPS = 1e-6
NEG = -1e30
FORCE_BONUS = 1e4
MIX_W = N_MIX_HEADS * HEAD_DIM
MEM_W = N_MEM_HEADS * HEAD_DIM
GATE_W = 3 * N_MIX_HEADS
N_A = 2
DEPTH = 4
LOG2E = 1.4426950408889634
Q_MUL = HEAD_DIM ** -0.5 * LOG2E
ACC_ROWS = HEAD_DIM + 16

LANES = 128
TM = 512
QB = 128
NSA_KC = 256
MASKED = 2.0 * NEG
FOX_TQ = 512
FOX_KC = 256
MEM_TQ = 512
FF_TM = 512
FF_CHUNK = 256
GATE_ROWS = 40
VMEM_LIMIT = 56 * 1024 * 1024

F32 = jnp.float32
BF16 = jnp.bfloat16


def _params(sem):
    return pltpu.CompilerParams(dimension_semantics=sem, vmem_limit_bytes=VMEM_LIMIT)


def _const_spec(shape):
    n = len(shape)
    return pl.BlockSpec(shape, lambda *_: (0,) * n, pipeline_mode=pl.Buffered(1))


def _w(arr, block=None, *idx):
    if block is None:
        block, idx = arr.shape, (0,) * arr.ndim
    return arr, pl.BlockSpec(block, lambda *_: idx, pipeline_mode=pl.Buffered(1))


def _split_params(params):
    return [a for a, _ in params], [s for _, s in params]


def _rms(x, g):
    return x * lax.rsqrt(jnp.mean(x * x, axis=-1, keepdims=True) + EPS) * g


def _dot(a, b):
    return jnp.dot(a, b, preferred_element_type=F32)


def _rope_lanes(x, cos, sin_signed):
    lane = lax.broadcasted_iota(jnp.int32, x.shape, 1)
    first_half = (lane & (HEAD_DIM - 1)) < (HEAD_DIM // 2)
    partner = jnp.where(first_half, pltpu.roll(x, LANES - HEAD_DIM // 2, 1), pltpu.roll(x, HEAD_DIM // 2, 1))
    return x * cos + partner * sin_signed


def _split3(x):
    a = x.astype(BF16)
    r = x - a.astype(F32)
    b = r.astype(BF16)
    c = (r - b.astype(F32)).astype(BF16)
    return a, b, c


def _proj_nsa_kernel(x_ref, g_ref, wq_ref, wkv_ref, wg_ref, gb_ref, wqm_ref, cosT_ref, sinT_ref, rc_ref, rs_ref,
                     qT_ref, kcmp_ref, vcmp_ref, kslc_ref, kwin_ref, vslcT_ref, vwinT_ref, gT_ref, qmT_ref):
    h = _rms(x_ref[0], g_ref[...]).astype(BF16)
    qT = (_dot(h, wq_ref[...]) * Q_MUL).T
    cos = cosT_ref[...]
    sin = sinT_ref[...]
    half = HEAD_DIM // 2
    for hh in range(N_MIX_HEADS):
        x1 = qT[hh * HEAD_DIM:hh * HEAD_DIM + half]
        x2 = qT[hh * HEAD_DIM + half:(hh + 1) * HEAD_DIM]
        qT_ref[0, hh * HEAD_DIM:hh * HEAD_DIM + half, :] = (x1 * cos - x2 * sin).astype(BF16)
        qT_ref[0, hh * HEAD_DIM + half:(hh + 1) * HEAD_DIM, :] = (x2 * cos + x1 * sin).astype(BF16)
    kv = _dot(h, wkv_ref[...])
    rc = rc_ref[...]
    rs = rs_ref[...]
    kcmp_ref[0] = kv[:, 0:128]
    vcmp_ref[0] = kv[:, 128:256]
    kslc_ref[0] = _rope_lanes(kv[:, 256:384], rc, rs).astype(BF16)
    kwin_ref[0] = _rope_lanes(kv[:, 512:640], rc, rs).astype(BF16)
    vsT = kv[:, 384:512].T.astype(BF16)
    vwT = kv[:, 640:768].T.astype(BF16)
    for j in range(TM // NSA_KC):
        vslcT_ref[0, j] = vsT[:, j * NSA_KC:(j + 1) * NSA_KC]
        vwinT_ref[0, j] = vwT[:, j * NSA_KC:(j + 1) * NSA_KC]
    gates = jax.nn.sigmoid(_dot(h, wg_ref[...]) + gb_ref[...])
    gT_ref[0] = gates.T[0:GATE_ROWS]
    qmT_ref[0] = (_dot(h, wqm_ref[...]) * Q_MUL).T.astype(BF16)


def _proj_nsa(x, params, cosT, sinT, rc, rs):
    B, S, _ = x.shape
    nt = S // TM
    arrs, specs = _split_params(params)
    tok = lambda w: pl.BlockSpec((1, TM, w), lambda b, j: (b, j, 0))
    featT = lambda r: pl.BlockSpec((1, r, TM), lambda b, j: (b, 0, j))
    vT = pl.BlockSpec((1, TM // NSA_KC, LANES, NSA_KC), lambda b, j: (b, j, 0, 0))
    return pl.pallas_call(
        _proj_nsa_kernel,
        grid=(B, nt),
        in_specs=[tok(D_MODEL)] + specs + [
                  pl.BlockSpec((HEAD_DIM // 2, TM), lambda b, j: (0, j)),
                  pl.BlockSpec((HEAD_DIM // 2, TM), lambda b, j: (0, j)),
                  pl.BlockSpec((TM, LANES), lambda b, j: (j, 0)),
                  pl.BlockSpec((TM, LANES), lambda b, j: (j, 0))],
        out_specs=[featT(MIX_W), tok(LANES), tok(LANES), tok(LANES), tok(LANES), vT, vT, featT(GATE_ROWS),
                   featT(MEM_W)],
        out_shape=[jax.ShapeDtypeStruct((B, MIX_W, S), BF16),
                   jax.ShapeDtypeStruct((B, S, LANES), F32), jax.ShapeDtypeStruct((B, S, LANES), F32),
                   jax.ShapeDtypeStruct((B, S, LANES), BF16), jax.ShapeDtypeStruct((B, S, LANES), BF16),
                   jax.ShapeDtypeStruct((B, S // NSA_KC, LANES, NSA_KC), BF16),
                   jax.ShapeDtypeStruct((B, S // NSA_KC, LANES, NSA_KC), BF16),
                   jax.ShapeDtypeStruct((B, GATE_ROWS, S), F32),
                   jax.ShapeDtypeStruct((B, MEM_W, S), BF16)],
        compiler_params=_params(("arbitrary", "arbitrary")),
        name="proj_nsa",
    )(x, *arrs, cosT, sinT, rc, rs)


def _proj_fox_kernel(x_ref, g_ref, wq_ref, wqm_ref, qT_ref, qmT_ref):
    h = _rms(x_ref[0], g_ref[...]).astype(BF16)
    qT_ref[0] = (_dot(h, wq_ref[...]) * Q_MUL).T.astype(BF16)
    qmT_ref[0] = (_dot(h, wqm_ref[...]) * Q_MUL).T.astype(BF16)


def _proj_fox(x, params):
    B, S, _ = x.shape
    arrs, specs = _split_params(params)
    featT = lambda r: pl.BlockSpec((1, r, TM), lambda b, j: (b, 0, j))
    return pl.pallas_call(
        _proj_fox_kernel,
        grid=(B, S // TM),
        in_specs=[pl.BlockSpec((1, TM, D_MODEL), lambda b, j: (b, j, 0))] + specs,
        out_specs=[featT(MIX_W), featT(MEM_W)],
        out_shape=[jax.ShapeDtypeStruct((B, MIX_W, S), BF16), jax.ShapeDtypeStruct((B, MEM_W, S), BF16)],
        compiler_params=_params(("arbitrary", "arbitrary")),
        name="proj_fox",
    )(x, *arrs)


def _kv_shared_kernel(x_ref, g_ref, wk_ref, wv_ref, wf_ref, bf_ref, k_ref, vT_ref, dcum_ref, dcumT_ref, carry_ref):
    j = pl.program_id(1)
    h = _rms(x_ref[0], g_ref[...]).astype(BF16)
    k_ref[0] = _dot(h, wk_ref[...]).astype(BF16)
    vT = _dot(h, wv_ref[...]).T.astype(BF16)
    for i in range(TM // FOX_KC):
        vT_ref[0, i] = vT[:, i * FOX_KC:(i + 1) * FOX_KC]
    z = _dot(h, wf_ref[...]) + bf_ref[...]
    logf = jnp.minimum(z, 0.0) - jnp.log1p(jnp.exp(-jnp.abs(z)))
    row = lax.broadcasted_iota(jnp.int32, (TM, TM), 0)
    col = lax.broadcasted_iota(jnp.int32, (TM, TM), 1)
    tri = (col <= row).astype(BF16)
    a, b, c = _split3(logf)
    @pl.when(j == 0)
    def _():
        carry_ref[...] = jnp.zeros(carry_ref.shape, F32)

    cs = _dot(tri, a) + _dot(tri, b) + _dot(tri, c) + carry_ref[7:8, :]
    carry_ref[...] = cs[TM - 8:TM]
    dcum_ref[0] = cs
    dcumT_ref[0] = cs.T[0:16]


def _kv_shared(x, params):
    B, S, _ = x.shape
    arrs, specs = _split_params(params)
    return pl.pallas_call(
        _kv_shared_kernel,
        grid=(B, S // TM),
        in_specs=[pl.BlockSpec((1, TM, D_MODEL), lambda b, j: (b, j, 0))] + specs,
        out_specs=[pl.BlockSpec((1, TM, MIX_W), lambda b, j: (b, j, 0)),
                   pl.BlockSpec((1, TM // FOX_KC, MIX_W, FOX_KC), lambda b, j: (b, j, 0, 0)),
                   pl.BlockSpec((1, TM, LANES), lambda b, j: (b, j, 0)),
                   pl.BlockSpec((1, 16, TM), lambda b, j: (b, 0, j))],
        out_shape=[jax.ShapeDtypeStruct((B, S, MIX_W), BF16),
                   jax.ShapeDtypeStruct((B, S // FOX_KC, MIX_W, FOX_KC), BF16),
                   jax.ShapeDtypeStruct((B, S, LANES), F32),
                   jax.ShapeDtypeStruct((B, 16, S), F32)],
        scratch_shapes=[pltpu.VMEM((8, LANES), F32)],
        compiler_params=_params(("arbitrary", "arbitrary")),
        name="kv_shared",
    )(x, *arrs)


def _mem_kv_kernel(m_ref, g_ref, w_ref, k_ref, vT_ref):
    h = _rms(m_ref[0], g_ref[...]).astype(BF16)
    kv = _dot(h, w_ref[...])
    k_ref[0] = kv[:, 0:MEM_W].astype(BF16)
    vT_ref[0] = kv[:, MEM_W:2 * MEM_W].T.astype(BF16)


def _mem_kv(mem, params):
    B, M, _ = mem.shape
    (g, w), specs = _split_params(params)
    return pl.pallas_call(
        _mem_kv_kernel,
        grid=(B,),
        in_specs=[pl.BlockSpec((1, M, D_MODEL), lambda b: (b, 0, 0))] + specs,
        out_specs=[pl.BlockSpec((1, M, MEM_W), lambda b: (b, 0, 0)), pl.BlockSpec((1, MEM_W, M), lambda b: (b, 0, 0))],
        out_shape=[jax.ShapeDtypeStruct((B, M, MEM_W), BF16), jax.ShapeDtypeStruct((B, MEM_W, M), BF16)],
        compiler_params=_params(("arbitrary",)),
        name="mem_kv",
    )(mem, g, w)


def _compress_kernel(kx_ref, vx_ref, wtop_ref, wbot_ref, ptop_ref, pbot_ref, b1_ref, w2_ref, b2_ref, rc_ref, rs_ref,
                     kc_ref, vcT_ref):
    for kvi, x_ref in enumerate((kx_ref, vx_ref)):
        x = x_ref[0]
        xt = (x + ptop_ref[kvi]).astype(BF16)
        xb = (x + pbot_ref[kvi]).astype(BF16)
        out = b2_ref[kvi]
        for g in range(N_KV_GROUPS):
            top = _dot(xt, wtop_ref[kvi, g])
            bot = _dot(xb, wbot_ref[kvi, g])
            hid = top + pltpu.roll(bot, bot.shape[0] - 1, 0) + b1_ref[kvi]
            out = out + _dot(jax.nn.gelu(hid).astype(BF16), w2_ref[kvi, g])
        if kvi == 0:
            kc_ref[0] = _rope_lanes(out, rc_ref[...], rs_ref[...]).astype(BF16)
        else:
            vcT_ref[0] = out.T.astype(BF16)


def _compress(kx, vx, params, rc, rs):
    B, NC, W = kx.shape
    xs = pl.BlockSpec((1, NC, W), lambda b: (b, 0, 0))
    arrs, specs = _split_params(params)
    return pl.pallas_call(
        _compress_kernel,
        grid=(B,),
        in_specs=[xs, xs] + specs + [_const_spec(rc.shape), _const_spec(rs.shape)],
        out_specs=[pl.BlockSpec((1, NC, LANES), lambda b: (b, 0, 0)), pl.BlockSpec((1, LANES, NC), lambda b: (b, 0, 0))],
        out_shape=[jax.ShapeDtypeStruct((B, NC, LANES), BF16), jax.ShapeDtypeStruct((B, LANES, NC), BF16)],
        compiler_params=_params(("arbitrary",)),
        name="nsa_compress",
    )(kx, vx, *arrs, rc, rs)


def _online_tile(s, m_ref, lead, cols):
    m_old = m_ref[lead, :, cols]
    m_new = jnp.maximum(m_old, jnp.max(s, axis=0, keepdims=True))
    m_ref[lead, :, cols] = m_new
    return jnp.exp2(s - m_new).astype(BF16), jnp.exp2(m_old - m_new)


def _reset_state(m_ref, acc_ref):
    m_ref[...] = jnp.full(m_ref.shape, NEG, F32)
    acc_ref[...] = jnp.zeros(acc_ref.shape, F32)


def _with_ones(vT):
    return jnp.concatenate([vT, jnp.ones((ACC_ROWS - HEAD_DIM, vT.shape[1]), vT.dtype)], axis=0)


def _normalise(acc):
    return acc[0:HEAD_DIM] / jnp.maximum(acc[HEAD_DIM:HEAD_DIM + 1], 1e-30)


def _nsa_attn_kernel(qT_ref, gT_ref, kc_ref, vcT_ref, kslc_ref, kwin_ref, vslcT_ref, vwinT_ref, ovT_ref,
                     o_ref, sel_ref, m_ref, acc_ref):
    c = pl.program_id(1)
    n_sel = sel_ref.shape[1]
    n_cmp = kc_ref.shape[1]
    GQ = HEADS_PER_GROUP * QB
    GW = HEADS_PER_GROUP * HEAD_DIM
    t_row = c * QB + lax.broadcasted_iota(jnp.int32, (1, QB), 1)
    gates = gT_ref[0]

    q6 = [jnp.concatenate([qT_ref[0, (HEADS_PER_GROUP * g + hh) * HEAD_DIM:(HEADS_PER_GROUP * g + hh + 1) * HEAD_DIM, :]
                           for hh in range(HEADS_PER_GROUP)], axis=1) for g in range(N_KV_GROUPS)]
    z = jnp.zeros_like(q6[0])
    qbd = jnp.concatenate([jnp.concatenate([q6[0], z], axis=1), jnp.concatenate([z, q6[1]], axis=1)], axis=0)

    cmp_end = lax.broadcasted_iota(jnp.int32, (n_cmp, QB), 0) * CMP_STRIDE + (L_CMP - 1)
    valid_c = cmp_end <= t_row
    valid_cf = valid_c.astype(F32)
    s_cmp = _dot(kc_ref[0], qbd)
    j_idx = lax.broadcasted_iota(jnp.int32, (n_sel, QB), 0)
    blk_t = lax.shift_right_logical(t_row, 6)
    forced = ((j_idx == 0) | (j_idx == blk_t) | (j_idx == blk_t - 1)).astype(F32)
    o_cmp = []
    for g in range(N_KV_GROUPS):
        psum = jnp.zeros((n_cmp, QB), F32)
        ps = []
        for hh in range(HEADS_PER_GROUP):
            col = g * GQ + hh * QB
            sh = jnp.where(valid_c, s_cmp[:, col:col + QB], NEG)
            e = jnp.exp2(sh - jnp.max(sh, axis=0, keepdims=True)) * valid_cf
            p = e / jnp.maximum(jnp.sum(e, axis=0, keepdims=True), 1e-30)
            psum = psum + p
            ps.append(p.astype(BF16))
        o_cmp.append(_dot(vcT_ref[0, g * HEAD_DIM:(g + 1) * HEAD_DIM, :], jnp.concatenate(ps, axis=1)))
        p_hi = psum.astype(BF16)
        p_lo = (psum - p_hi.astype(F32)).astype(BF16)
        imp = _dot(ovT_ref[...], p_hi) + _dot(ovT_ref[...], p_lo)
        score = jnp.where(j_idx <= blk_t, imp + FORCE_BONUS * forced, NEG)
        cnt = jnp.zeros((n_sel, QB), F32)
        for i in range(n_sel):
            si = score[i:i + 1, :]
            cnt = cnt + ((si > score) | ((si == score) & (j_idx > i))).astype(F32)
        sel = (cnt < float(min(TOP_N, n_sel))).astype(F32)
        for i in range(n_sel):
            sel_ref[g, i] = jnp.broadcast_to(sel[i:i + 1, :], (8, QB))

    _reset_state(m_ref, acc_ref)
    key_off = lax.broadcasted_iota(jnp.int32, (NSA_KC, QB), 0)

    def sweep(br, k_ref, vT_ref, kb, masks):
        s = _dot(k_ref[0, pl.ds(pl.multiple_of(kb * NSA_KC, NSA_KC), NSA_KC), :], qbd)
        for g in range(N_KV_GROUPS):
            ps, alphas = [], []
            for hh in range(HEADS_PER_GROUP):
                col = g * GQ + hh * QB
                p, a = _online_tile(jnp.where(masks[g], s[:, col:col + QB], MASKED), m_ref, br, slice(col, col + QB))
                ps.append(p)
                alphas.append(a)
            gc = slice(g * GQ, (g + 1) * GQ)
            vT = _with_ones(vT_ref[0, kb, g * HEAD_DIM:(g + 1) * HEAD_DIM, :])
            acc_ref[br, :, gc] = acc_ref[br, :, gc] * jnp.concatenate(alphas, axis=1) + _dot(vT, jnp.concatenate(ps, axis=1))

    def slc_masks(kb, causal):
        per_chunk = NSA_KC // L_SEL
        masks = []
        for g in range(N_KV_GROUPS):
            picked = sel_ref[g, per_chunk * kb + per_chunk - 1][0:1, :]
            for i in reversed(range(per_chunk - 1)):
                picked = jnp.where(key_off < (i + 1) * L_SEL, sel_ref[g, per_chunk * kb + i][0:1, :], picked)
            mk = picked > 0.5
            masks.append(mk & (kb * NSA_KC + key_off <= t_row) if causal else mk)
        return masks

    last = lax.shift_right_logical(c * QB, NSA_KC.bit_length() - 1)
    first_win = jnp.maximum(last - WINDOW // NSA_KC, 0)

    def far_body(kb, carry):
        sweep(0, kslc_ref, vslcT_ref, kb, slc_masks(kb, False))
        return carry

    def near_body(kb, carry):
        sweep(0, kslc_ref, vslcT_ref, kb, slc_masks(kb, True))
        key_pos = kb * NSA_KC + key_off
        wm = (key_pos <= t_row) & (t_row - key_pos < WINDOW)
        sweep(1, kwin_ref, vwinT_ref, kb, [wm, wm])
        return carry

    lax.fori_loop(0, first_win, far_body, 0)
    lax.fori_loop(first_win, last + 1, near_body, 0)
    o_slc = _normalise(acc_ref[0])
    o_win = _normalise(acc_ref[1])

    for g in range(N_KV_GROUPS):
        outs = []
        for hh in range(HEADS_PER_GROUP):
            r = 3 * (HEADS_PER_GROUP * g + hh)
            cols = slice(g * GQ + hh * QB, g * GQ + (hh + 1) * QB)
            outs.append(gates[r:r + 1] * o_cmp[g][:, hh * QB:(hh + 1) * QB] + gates[r + 1:r + 2] * o_slc[:, cols]
                        + gates[r + 2:r + 3] * o_win[:, cols])
        o_ref[0, :, g * GW:(g + 1) * GW] = jnp.concatenate(outs, axis=0).T.astype(BF16)


def _nsa_attn(qT, gT, kc, vcT, kslc, kwin, vslcT, vwinT, ovT):
    B, _, S = qT.shape
    n_sel = S // L_SEL
    W12 = N_MIX_HEADS * QB
    per_b3 = lambda a: pl.BlockSpec((1,) + a.shape[1:], lambda b, c: (b, 0, 0))
    per_b4 = lambda a: pl.BlockSpec((1,) + a.shape[1:], lambda b, c: (b, 0, 0, 0))
    return pl.pallas_call(
        _nsa_attn_kernel,
        grid=(B, S // QB),
        in_specs=[pl.BlockSpec((1, MIX_W, QB), lambda b, c: (b, 0, c)),
                  pl.BlockSpec((1, GATE_ROWS, QB), lambda b, c: (b, 0, c)),
                  per_b3(kc), per_b3(vcT), per_b3(kslc), per_b3(kwin), per_b4(vslcT), per_b4(vwinT),
                  _const_spec(ovT.shape)],
        out_specs=pl.BlockSpec((1, QB, MIX_W), lambda b, c: (b, c, 0)),
        out_shape=jax.ShapeDtypeStruct((B, S, MIX_W), BF16),
        scratch_shapes=[pltpu.VMEM((N_KV_GROUPS, n_sel, 8, QB), F32), pltpu.VMEM((2, 1, W12), F32),
                        pltpu.VMEM((2, ACC_ROWS, W12), F32)],
        compiler_params=_params(("arbitrary", "arbitrary")),
        name="nsa_attn",
    )(qT, gT, kc, vcT, kslc, kwin, vslcT, vwinT, ovT)


def _pair_q(q2, width):
    z = jnp.zeros((HEAD_DIM, width), q2.dtype)
    return jnp.concatenate([jnp.concatenate([q2[0:HEAD_DIM], z], axis=1),
                            jnp.concatenate([z, q2[HEAD_DIM:2 * HEAD_DIM]], axis=1)], axis=0)


def _fox_attn_kernel(qT_ref, k_ref, vT_ref, dcum_ref, dcumT_ref, o_ref, kbias_ref, m_ref, acc_ref):
    p = pl.program_id(1)
    c = pl.program_id(2)
    nq = FOX_TQ // QB
    per_q = FOX_TQ // FOX_KC

    @pl.when(c == 0)
    def _():
        parts = _split3(dcum_ref[0] * LOG2E)
        src = lax.broadcasted_iota(jnp.int32, (LANES, LANES), 0)
        dst = lax.broadcasted_iota(jnp.int32, (LANES, LANES), 1)
        lane = lax.broadcasted_iota(jnp.int32, (1, LANES), 1)
        acc = ((lane >= 3) & (lane < 6)).astype(F32)
        for r in range(3):
            place = ((src == 2 * p) & (dst == r)) | ((src == 2 * p + 1) & (dst == 6 + r))
            acc = acc + _dot(parts[r], place.astype(BF16))
        kbias_ref[...] = acc.astype(BF16)

    rowi = lax.broadcasted_iota(jnp.int32, (16, 2 * FOX_TQ), 0)
    head1 = lax.broadcasted_iota(jnp.int32, (16, 2 * FOX_TQ), 1) >= FOX_TQ
    bias = jnp.where(((rowi < 3) & ~head1) | ((rowi >= 6) & (rowi < 9) & head1), -1.0, 0.0)
    dq = [_split3(dcumT_ref[0, pl.ds(2 * p + hh, 1), :] * LOG2E) for hh in range(2)]
    for r in range(3):
        bias = jnp.where(rowi == 3 + r, jnp.concatenate([dq[0][r], dq[1][r]], axis=1).astype(F32), bias)
    qaug = jnp.concatenate([_pair_q(qT_ref[0], FOX_TQ), bias.astype(BF16),
                            jnp.zeros((LANES - 16, 2 * FOX_TQ), BF16)], axis=0)

    _reset_state(m_ref, acc_ref)
    lane_t =lax.broadcasted_iota(jnp.int32, (1, QB), 1)

    def sweep(kb, n_kc, qs_lo, masked):
        size = n_kc * FOX_KC
        start = pl.multiple_of(kb * FOX_KC, FOX_KC)
        kk = jnp.concatenate([k_ref[0, pl.ds(start, size), :], kbias_ref[pl.ds(start, size), :]], axis=1)
        if masked:
            key_pos = kb * FOX_KC + lax.broadcasted_iota(jnp.int32, (size, QB), 0)
        live = FOX_TQ - qs_lo * QB
        if qs_lo == 0:
            s2 = _dot(kk, qaug)
            s = [s2[:, 0:FOX_TQ], s2[:, FOX_TQ:2 * FOX_TQ]]
        else:
            s = [_dot(kk, qaug[:, (hh + 1) * FOX_TQ - live:(hh + 1) * FOX_TQ]) for hh in range(2)]
        for hh in range(2):
            cols = slice((hh + 1) * FOX_TQ - live, (hh + 1) * FOX_TQ)
            ps, alphas = [], []
            for qs in range(qs_lo, nq):
                tile = s[hh][:, (qs - qs_lo) * QB:(qs - qs_lo + 1) * QB]
                if masked:
                    tile = jnp.where(key_pos <= c * FOX_TQ + qs * QB + lane_t, tile, MASKED)
                col = hh * FOX_TQ + qs * QB
                pr, a = _online_tile(tile, m_ref, 0, slice(col, col + QB))
                ps.append(pr)
                alphas.append(a)
            vT = jnp.concatenate([vT_ref[0, kb + i, hh * HEAD_DIM:(hh + 1) * HEAD_DIM, :] for i in range(n_kc)], axis=1)
            acc_ref[0, :, cols] = (acc_ref[0, :, cols] * jnp.concatenate(alphas, axis=1)
                                   + _dot(_with_ones(vT), jnp.concatenate(ps, axis=1)))

    def full_body(i, carry):
        sweep(i * per_q, per_q, 0, False)
        return carry

    lax.fori_loop(0, c, full_body, 0)
    for i in range(per_q):
        sweep(c * per_q + i, 1, i * FOX_KC // QB, True)
    o = _normalise(acc_ref[0])
    oT =jnp.concatenate([o[:, 0:FOX_TQ], o[:, FOX_TQ:2 * FOX_TQ]], axis=0)
    o_ref[0] = oT.T.astype(BF16)


def _fox_attn(qT, k, vT, dcum, dcumT):
    B, _, S = qT.shape
    n_pairs = N_MIX_HEADS // 2
    return pl.pallas_call(
        _fox_attn_kernel,
        grid=(B, n_pairs, S // FOX_TQ),
        in_specs=[pl.BlockSpec((1, LANES, FOX_TQ), lambda b, p, c: (b, p, c)),
                  pl.BlockSpec((1, S, LANES), lambda b, p, c: (b, 0, p)),
                  pl.BlockSpec((1, S // FOX_KC, LANES, FOX_KC), lambda b, p, c: (b, 0, p, 0)),
                  pl.BlockSpec((1, S, LANES), lambda b, p, c: (b, 0, 0)),
                  pl.BlockSpec((1, 16, FOX_TQ), lambda b, p, c: (b, 0, c))],
        out_specs=pl.BlockSpec((1, FOX_TQ, LANES), lambda b, p, c: (b, c, p)),
        out_shape=jax.ShapeDtypeStruct((B, S, MIX_W), BF16),
        scratch_shapes=[pltpu.VMEM((S, LANES), BF16), pltpu.VMEM((1, 1, 2 * FOX_TQ), F32),
                        pltpu.VMEM((1, ACC_ROWS, 2 * FOX_TQ), F32)],
        compiler_params=_params(("arbitrary", "arbitrary", "arbitrary")),
        name="fox_attn",
    )(qT, k, vT, dcum, dcumT)


def _mem_attn_kernel(qT_ref, k_ref, vT_ref, o_ref):
    outs = []
    for pr in range(N_MEM_HEADS // 2):
        qbd = _pair_q(qT_ref[0, pr * LANES:(pr + 1) * LANES, :], MEM_TQ)
        s = _dot(k_ref[0, :, pr * LANES:(pr + 1) * LANES], qbd)
        for hh in range(2):
            ps = []
            for qs in range(MEM_TQ // QB):
                col = hh * MEM_TQ + qs * QB
                sh = s[:, col:col + QB]
                e = jnp.exp2(sh - jnp.max(sh, axis=0, keepdims=True))
                ps.append((e / jnp.sum(e, axis=0, keepdims=True)).astype(BF16))
            h = 2 * pr + hh
            outs.append(_dot(vT_ref[0, h * HEAD_DIM:(h + 1) * HEAD_DIM, :], jnp.concatenate(ps, axis=1)))
    o_ref[0] = jnp.concatenate(outs, axis=0).T.astype(BF16)


def _mem_attn(qmT, kmem, vmemT):
    B, _, S = qmT.shape
    M = kmem.shape[1]
    return pl.pallas_call(
        _mem_attn_kernel,
        grid=(B, S // MEM_TQ),
        in_specs=[pl.BlockSpec((1, MEM_W, MEM_TQ), lambda b, c: (b, 0, c)),
                  pl.BlockSpec((1, M, MEM_W), lambda b, c: (b, 0, 0)),
                  pl.BlockSpec((1, MEM_W, M), lambda b, c: (b, 0, 0))],
        out_specs=pl.BlockSpec((1, MEM_TQ, MEM_W), lambda b, c: (b, c, 0)),
        out_shape=jax.ShapeDtypeStruct((B, S, MEM_W), BF16),
        compiler_params=_params(("arbitrary", "arbitrary")),
        name="mem_attn",
    )(qmT, kmem, vmemT)


def _ffn_kernel(x_ref, omix_ref, omem_ref, wo1_ref, wo2_ref, g_ref, wa_ref, wb_ref, cwa_ref, cwb_ref, cba_ref,
                cbb_ref, wd_ref, gfin_ref, o_ref, ca_ref, cb_ref, gate_ref, *, final):
    j = pl.program_id(1)
    x1 = x_ref[0] + _dot(omix_ref[0], wo1_ref[...]) + _dot(omem_ref[0], wo2_ref[...])
    h = _rms(x1, g_ref[...]).astype(BF16)

    @pl.when(j == 0)
    def _():
        ca_ref[...] = jnp.zeros(ca_ref.shape, F32)
        cb_ref[...] = jnp.zeros(cb_ref.shape, F32)

    def conv(u, prev8, w, bias):
        ext = jnp.concatenate([prev8, u], axis=0)
        u1 = pltpu.roll(ext, 1, 0)[8:]
        u2 = pltpu.roll(ext, 2, 0)[8:]
        return w[0:1] * u2 + w[1:2] * u1 + w[2:3] * u + bias

    for ci in range(D_FF // FF_CHUNK):
        sl = slice(ci * FF_CHUNK, (ci + 1) * FF_CHUNK)
        a = _dot(h, wa_ref[:, sl])
        b = _dot(h, wb_ref[:, sl])
        pa = ca_ref[:, sl]
        pb = cb_ref[:, sl]
        ca_ref[:, sl] = a[FF_TM - 8:FF_TM]
        cb_ref[:, sl] = b[FF_TM - 8:FF_TM]
        ac = conv(a, pa, cwa_ref[:, sl], cba_ref[:, sl])
        bc = conv(b, pb, cwb_ref[:, sl], cbb_ref[:, sl])
        gate_ref[:, sl] = (ac * jax.nn.sigmoid(ac) * bc).astype(BF16)
    y = x1 + _dot(gate_ref[...], wd_ref[...])
    o_ref[0] = _rms(y, gfin_ref[...]) if final else y


def _ffn(x, omix, omem, params, final):
    B, S, _ = x.shape
    tok = lambda w: pl.BlockSpec((1, FF_TM, w), lambda b, j: (b, j, 0))
    arrs, specs = _split_params(params)
    return pl.pallas_call(
        functools.partial(_ffn_kernel, final=final),
        grid=(B, S // FF_TM),
        in_specs=[tok(D_MODEL), tok(MIX_W), tok(MEM_W)] + specs,
        out_specs=tok(D_MODEL),
        out_shape=jax.ShapeDtypeStruct((B, S, D_MODEL), F32),
        scratch_shapes=[pltpu.VMEM((8, D_FF), F32), pltpu.VMEM((8, D_FF), F32), pltpu.VMEM((FF_TM, D_FF), BF16)],
        compiler_params=_params(("arbitrary", "arbitrary")),
        name="outproj_convffn",
    )(x, omix, omem, *arrs)


def _rope_tables(pos):
    half = HEAD_DIM // 2
    inv = ROPE_THETA ** (-jnp.arange(half, dtype=F32) / half)
    ang = pos.astype(F32)[:, None] * inv[None, :]
    cos, sin = jnp.cos(ang), jnp.sin(ang)
    reps = LANES // half
    signs = jnp.tile(jnp.concatenate([-jnp.ones((half,), F32), jnp.ones((half,), F32)]), LANES // HEAD_DIM)
    return cos.T, sin.T, jnp.tile(cos, (1, reps)), jnp.tile(sin, (1, reps)) * signs[None, :]


def _overlap_T(S, n_cmp_pad):
    n_cmp = (S - L_CMP) // CMP_STRIDE + 1
    n_sel = S // L_SEL
    cs = np.arange(n_cmp_pad) * CMP_STRIDE
    ss = np.arange(n_sel) * L_SEL
    ov = (cs[None, :] < ss[:, None] + L_SEL) & (cs[None, :] + L_CMP > ss[:, None]) & (np.arange(n_cmp_pad)[None, :] < n_cmp)
    return jnp.asarray(ov, dtype=BF16)


def _compress_params(pos, w1, b1, w2, b2):
    half = L_CMP // 2
    w1r = w1.reshape(2, L_CMP, HEAD_DIM, CMP_HIDDEN)

    def spread(w):
        z = jnp.zeros((2, N_KV_GROUPS, half, N_KV_GROUPS, HEAD_DIM, CMP_HIDDEN), F32)
        for g in range(N_KV_GROUPS):
            z = z.at[:, g, :, g].set(w)
        return z.reshape(2, N_KV_GROUPS, half * LANES, CMP_HIDDEN).astype(BF16)

    def spread_pos(p):
        return jnp.tile(p[:, :, None, :], (1, 1, N_KV_GROUPS, 1)).reshape(2, 1, half * LANES)

    w2p = jnp.zeros((2, N_KV_GROUPS, CMP_HIDDEN, N_KV_GROUPS, HEAD_DIM), F32)
    for g in range(N_KV_GROUPS):
        w2p = w2p.at[:, g, :, g].set(w2)
    return (spread(w1r[:, :half]), spread(w1r[:, half:]), spread_pos(pos[:, :half]), spread_pos(pos[:, half:]),
            b1[:, None, :], w2p.reshape(2, N_KV_GROUPS, CMP_HIDDEN, LANES).astype(BF16),
            jnp.tile(b2, (1, N_KV_GROUPS))[:, None, :])


def _pad_cols(w, width):
    return jnp.pad(w, ((0, 0), (0, width - w.shape[1])))


def kernel(x, mem, attn_norm, ffn_norm, mem_norm, w_mem_kv, w_o, w_up, conv_w, conv_b, w_down, a_w_in, a_gate_b,
           a_cmp_pos, a_cmp_w1, a_cmp_b1, a_cmp_w2, a_cmp_b2, b_w_in, kv_norm, w_kv_shared, b_fgate, final_norm):
    B, S, _ = x.shape
    assert S % FOX_TQ == 0 and S % (CMP_STRIDE * 8) == 0
    n_chunks = S // CMP_STRIDE
    cosT, sinT, rc, rs = _rope_tables(jnp.arange(S))
    _, _, rc_cmp, rs_cmp = _rope_tables(jnp.arange(n_chunks) * CMP_STRIDE + L_CMP - 1)
    ovT = _overlap_T(S, n_chunks)

    rows3 = lambda v: v.reshape(v.shape[0], 1, -1).astype(F32)
    layer = lambda a, l, block, *idx: _w(a, (None,) + block, l, *idx)
    gain = lambda a, l: layer(a, l, (1, D_MODEL), 0, 0)
    attn_g, ffn_g, mem_g = rows3(attn_norm), rows3(ffn_norm), rows3(mem_norm)
    w_mem_b, w_o_b, w_up_b, w_down_b = (a.astype(BF16) for a in (w_mem_kv, w_o, w_up, w_down))
    a_w_b, b_w_b, w_sh_b = a_w_in.astype(BF16), b_w_in.astype(BF16), w_kv_shared.astype(BF16)
    conv_b3 = rows3(conv_b)
    a_wg = jnp.pad(a_w_b[:, :, 2 * MIX_W:2 * MIX_W + GATE_W], ((0, 0), (0, 0), (0, LANES - GATE_W)))
    a_gb = jnp.pad(rows3(a_gate_b), ((0, 0), (0, 0), (0, LANES - GATE_W)))
    a_wqm = a_w_b[:, :, 2 * MIX_W + GATE_W:]
    cmp_params = jax.vmap(_compress_params)(a_cmp_pos, a_cmp_w1, a_cmp_b1, a_cmp_w2, a_cmp_b2)
    w_f = _pad_cols(w_sh_b[:, 2 * MIX_W:], LANES)
    b_f = _pad_cols(b_fgate[None, :].astype(F32), LANES)
    fin_g = final_norm.reshape(1, -1).astype(F32)

    k_sh = vT_sh = dcum = dcumT = None
    for l in range(DEPTH):
        kmem, vmemT = _mem_kv(mem, [gain(mem_g, l), layer(w_mem_b, l, (D_MODEL, 2 * MEM_W), 0, 0)])
        if l < N_A:
            qT, kcmp, vcmp, kslc, kwin, vslcT, vwinT, gT, qmT = _proj_nsa(
                x, [gain(attn_g, l), layer(a_w_b, l, (D_MODEL, MIX_W), 0, 0), layer(a_w_b, l, (D_MODEL, MIX_W), 0, 1),
                    layer(a_wg, l, (D_MODEL, LANES), 0, 0), layer(a_gb, l, (1, LANES), 0, 0),
                    layer(a_wqm, l, (D_MODEL, MEM_W), 0, 0)],
                cosT, sinT, rc, rs)
            kc, vcT = _compress(kcmp.reshape(B, n_chunks, CMP_STRIDE * LANES), vcmp.reshape(B, n_chunks, CMP_STRIDE * LANES),
                                [layer(a, l, a.shape[1:], *(0,) * (a.ndim - 1)) for a in cmp_params], rc_cmp, rs_cmp)
            omix = _nsa_attn(qT, gT, kc, vcT, kslc, kwin, vslcT, vwinT, ovT)
        else:
            if l == N_A:
                k_sh, vT_sh, dcum, dcumT = _kv_shared(
                    x, [_w(kv_norm.reshape(1, -1).astype(F32)), _w(w_sh_b, (D_MODEL, MIX_W), 0, 0),
                        _w(w_sh_b, (D_MODEL, MIX_W), 0, 1), _w(w_f), _w(b_f)])
            qT, qmT = _proj_fox(x, [gain(attn_g, l), layer(b_w_b, l - N_A, (D_MODEL, MIX_W), 0, 0),
                                    layer(b_w_b, l - N_A, (D_MODEL, MEM_W), 0, MIX_W // MEM_W)])
            omix = _fox_attn(qT, k_sh, vT_sh, dcum, dcumT)
        omem = _mem_attn(qmT, kmem, vmemT)
        x = _ffn(x, omix, omem,
                 [layer(w_o_b, l, (MIX_W, D_MODEL), 0, 0), layer(w_o_b, l, (MEM_W, D_MODEL), MIX_W // MEM_W, 0),
                  gain(ffn_g, l), layer(w_up_b, l, (D_MODEL, D_FF), 0, 0), layer(w_up_b, l, (D_MODEL, D_FF), 0, 1),
                  layer(conv_w, l, (CONV_WIDTH, D_FF), 0, 0), layer(conv_w, l, (CONV_WIDTH, D_FF), 0, 1),
                  layer(conv_b3, l, (1, D_FF), 0, 0), layer(conv_b3, l, (1, D_FF), 0, 1),
                  layer(w_down_b, l, (D_FF, D_MODEL), 0, 0), _w(fin_g)],
                 final=(l == DEPTH - 1))
    return x
```

```python
import functools

import numpy as np
import jax
import jax.numpy as jnp
from jax import lax
from jax.experimental import pallas as pl
from jax.experimental.pallas import tpu as pltpu

D_MODEL = 1024
HEAD_DIM = 64
N_MIX_HEADS = 12
N_KV_GROUPS = 2
HEADS_PER_GROUP = N_MIX_HEADS // N_KV_GROUPS
N_MEM_HEADS = 4
L_CMP = 32
CMP_STRIDE = 16
CMP_HIDDEN = 256
L_SEL = 64
TOP_N = 16
WINDOW = 512
D_FF = 2816
CONV_WIDTH = 3
ROPE_THETA = 10000.0
EPS = 1e-6
NEG = -1e30
FORCE_BONUS = 1e4
MIX_W = N_MIX_HEADS * HEAD_DIM
MEM_W = N_MEM_HEADS * HEAD_DIM
GATE_W = 3 * N_MIX_HEADS
N_A = 2
DEPTH = 4
LOG2E = 1.4426950408889634
Q_MUL = HEAD_DIM ** -0.5 * LOG2E
ACC_ROWS = HEAD_DIM + 16

LANES = 128
TM = 512
QB = 128
NSA_KC = 256
MASKED = 2.0 * NEG
FOX_TQ = 512
FOX_KC = 256
MEM_TQ = 512
FF_TM = 512
FF_CHUNK = 256
GATE_ROWS = 40
VMEM_LIMIT = 56 * 1024 * 1024

F32 = jnp.float32
BF16 = jnp.bfloat16


def _params(sem):
    return pltpu.CompilerParams(dimension_semantics=sem, vmem_limit_bytes=VMEM_LIMIT)


def _const_spec(shape):
    n = len(shape)
    return pl.BlockSpec(shape, lambda *_: (0,) * n, pipeline_mode=pl.Buffered(1))


def _w(arr, block=None, *idx):
    if block is None:
        block, idx = arr.shape, (0,) * arr.ndim
    return arr, pl.BlockSpec(block, lambda *_: idx, pipeline_mode=pl.Buffered(1))


def _split_params(params):
    return [a for a, _ in params], [s for _, s in params]


def _rms(x, g):
    return x * lax.rsqrt(jnp.mean(x * x, axis=-1, keepdims=True) + EPS) * g


def _dot(a, b):
    return jnp.dot(a, b, preferred_element_type=F32)


def _rope_lanes(x, cos, sin_signed):
    lane = lax.broadcasted_iota(jnp.int32, x.shape, 1)
    first_half = (lane & (HEAD_DIM - 1)) < (HEAD_DIM // 2)
    partner = jnp.where(first_half, pltpu.roll(x, LANES - HEAD_DIM // 2, 1), pltpu.roll(x, HEAD_DIM // 2, 1))
    return x * cos + partner * sin_signed


def _split3(x):
    a = x.astype(BF16)
    r = x - a.astype(F32)
    b = r.astype(BF16)
    c = (r - b.astype(F32)).astype(BF16)
    return a, b, c


def _proj_nsa_kernel(x_ref, g_ref, wq_ref, wkv_ref, wg_ref, gb_ref, wqm_ref, cosT_ref, sinT_ref, rc_ref, rs_ref,
                     qT_ref, kcmp_ref, vcmp_ref, kslc_ref, kwin_ref, vslcT_ref, vwinT_ref, gT_ref, qmT_ref):
    h = _rms(x_ref[0], g_ref[...]).astype(BF16)
    qT = (_dot(h, wq_ref[...]) * Q_MUL).T
    cos = cosT_ref[...]
    sin = sinT_ref[...]
    half = HEAD_DIM // 2
    for hh in range(N_MIX_HEADS):
        x1 = qT[hh * HEAD_DIM:hh * HEAD_DIM + half]
        x2 = qT[hh * HEAD_DIM + half:(hh + 1) * HEAD_DIM]
        qT_ref[0, hh * HEAD_DIM:hh * HEAD_DIM + half, :] = (x1 * cos - x2 * sin).astype(BF16)
        qT_ref[0, hh * HEAD_DIM + half:(hh + 1) * HEAD_DIM, :] = (x2 * cos + x1 * sin).astype(BF16)
    kv = _dot(h, wkv_ref[...])
    rc = rc_ref[...]
    rs = rs_ref[...]
    kcmp_ref[0] = kv[:, 0:128]
    vcmp_ref[0] = kv[:, 128:256]
    kslc_ref[0] = _rope_lanes(kv[:, 256:384], rc, rs).astype(BF16)
    kwin_ref[0] = _rope_lanes(kv[:, 512:640], rc, rs).astype(BF16)
    vsT = kv[:, 384:512].T.astype(BF16)
    vwT = kv[:, 640:768].T.astype(BF16)
    for j in range(TM // NSA_KC):
        vslcT_ref[0, j] = vsT[:, j * NSA_KC:(j + 1) * NSA_KC]
        vwinT_ref[0, j] = vwT[:, j * NSA_KC:(j + 1) * NSA_KC]
    gates = jax.nn.sigmoid(_dot(h, wg_ref[...]) + gb_ref[...])
    gT_ref[0] = gates.T[0:GATE_ROWS]
    qmT_ref[0] = (_dot(h, wqm_ref[...]) * Q_MUL).T.astype(BF16)


def _proj_nsa(x, params, cosT, sinT, rc, rs):
    B, S, _ = x.shape
    nt = S // TM
    arrs, specs = _split_params(params)
    tok = lambda w: pl.BlockSpec((1, TM, w), lambda b, j: (b, j, 0))
    featT = lambda r: pl.BlockSpec((1, r, TM), lambda b, j: (b, 0, j))
    vT = pl.BlockSpec((1, TM // NSA_KC, LANES, NSA_KC), lambda b, j: (b, j, 0, 0))
    return pl.pallas_call(
        _proj_nsa_kernel,
        grid=(B, nt),
        in_specs=[tok(D_MODEL)] + specs + [
                  pl.BlockSpec((HEAD_DIM // 2, TM), lambda b, j: (0, j)),
                  pl.BlockSpec((HEAD_DIM // 2, TM), lambda b, j: (0, j)),
                  pl.BlockSpec((TM, LANES), lambda b, j: (j, 0)),
                  pl.BlockSpec((TM, LANES), lambda b, j: (j, 0))],
        out_specs=[featT(MIX_W), tok(LANES), tok(LANES), tok(LANES), tok(LANES), vT, vT, featT(GATE_ROWS),
                   featT(MEM_W)],
        out_shape=[jax.ShapeDtypeStruct((B, MIX_W, S), BF16),
                   jax.ShapeDtypeStruct((B, S, LANES), F32), jax.ShapeDtypeStruct((B, S, LANES), F32),
                   jax.ShapeDtypeStruct((B, S, LANES), BF16), jax.ShapeDtypeStruct((B, S, LANES), BF16),
                   jax.ShapeDtypeStruct((B, S // NSA_KC, LANES, NSA_KC), BF16),
                   jax.ShapeDtypeStruct((B, S // NSA_KC, LANES, NSA_KC), BF16),
                   jax.ShapeDtypeStruct((B, GATE_ROWS, S), F32),
                   jax.ShapeDtypeStruct((B, MEM_W, S), BF16)],
        compiler_params=_params(("arbitrary", "arbitrary")),
        name="proj_nsa",
    )(x, *arrs, cosT, sinT, rc, rs)


def _proj_fox_kernel(x_ref, g_ref, wq_ref, wqm_ref, qT_ref, qmT_ref):
    h = _rms(x_ref[0], g_ref[...]).astype(BF16)
    qT_ref[0] = (_dot(h, wq_ref[...]) * Q_MUL).T.astype(BF16)
    qmT_ref[0] = (_dot(h, wqm_ref[...]) * Q_MUL).T.astype(BF16)


def _proj_fox(x, params):
    B, S, _ = x.shape
    arrs, specs = _split_params(params)
    featT = lambda r: pl.BlockSpec((1, r, TM), lambda b, j: (b, 0, j))
    return pl.pallas_call(
        _proj_fox_kernel,
        grid=(B, S // TM),
        in_specs=[pl.BlockSpec((1, TM, D_MODEL), lambda b, j: (b, j, 0))] + specs,
        out_specs=[featT(MIX_W), featT(MEM_W)],
        out_shape=[jax.ShapeDtypeStruct((B, MIX_W, S), BF16), jax.ShapeDtypeStruct((B, MEM_W, S), BF16)],
        compiler_params=_params(("arbitrary", "arbitrary")),
        name="proj_fox",
    )(x, *arrs)


def _kv_shared_kernel(x_ref, g_ref, wk_ref, wv_ref, wf_ref, bf_ref, k_ref, vT_ref, dcum_ref, dcumT_ref, carry_ref):
    j = pl.program_id(1)
    h = _rms(x_ref[0], g_ref[...]).astype(BF16)
    k_ref[0] = _dot(h, wk_ref[...]).astype(BF16)
    vT = _dot(h, wv_ref[...]).T.astype(BF16)
    for i in range(TM // FOX_KC):
        vT_ref[0, i] = vT[:, i * FOX_KC:(i + 1) * FOX_KC]
    z = _dot(h, wf_ref[...]) + bf_ref[...]
    logf = jnp.minimum(z, 0.0) - jnp.log1p(jnp.exp(-jnp.abs(z)))
    row = lax.broadcasted_iota(jnp.int32, (TM, TM), 0)
    col = lax.broadcasted_iota(jnp.int32, (TM, TM), 1)
    tri = (col <= row).astype(BF16)
    a, b, c = _split3(logf)
    @pl.when(j == 0)
    def _():
        carry_ref[...] = jnp.zeros(carry_ref.shape, F32)

    cs = _dot(tri, a) + _dot(tri, b) + _dot(tri, c) + carry_ref[7:8, :]
    carry_ref[...] = cs[TM - 8:TM]
    dcum_ref[0] = cs
    dcumT_ref[0] = cs.T[0:16]


def _kv_shared(x, params):
    B, S, _ = x.shape
    arrs, specs = _split_params(params)
    return pl.pallas_call(
        _kv_shared_kernel,
        grid=(B, S // TM),
        in_specs=[pl.BlockSpec((1, TM, D_MODEL), lambda b, j: (b, j, 0))] + specs,
        out_specs=[pl.BlockSpec((1, TM, MIX_W), lambda b, j: (b, j, 0)),
                   pl.BlockSpec((1, TM // FOX_KC, MIX_W, FOX_KC), lambda b, j: (b, j, 0, 0)),
                   pl.BlockSpec((1, TM, LANES), lambda b, j: (b, j, 0)),
                   pl.BlockSpec((1, 16, TM), lambda b, j: (b, 0, j))],
        out_shape=[jax.ShapeDtypeStruct((B, S, MIX_W), BF16),
                   jax.ShapeDtypeStruct((B, S // FOX_KC, MIX_W, FOX_KC), BF16),
                   jax.ShapeDtypeStruct((B, S, LANES), F32),
                   jax.ShapeDtypeStruct((B, 16, S), F32)],
        scratch_shapes=[pltpu.VMEM((8, LANES), F32)],
        compiler_params=_params(("arbitrary", "arbitrary")),
        name="kv_shared",
    )(x, *arrs)


def _mem_kv_kernel(m_ref, g_ref, w_ref, k_ref, vT_ref):
    h = _rms(m_ref[0], g_ref[...]).astype(BF16)
    kv = _dot(h, w_ref[...])
    k_ref[0] = kv[:, 0:MEM_W].astype(BF16)
    vT_ref[0] = kv[:, MEM_W:2 * MEM_W].T.astype(BF16)


def _mem_kv(mem, params):
    B, M, _ = mem.shape
    (g, w), specs = _split_params(params)
    return pl.pallas_call(
        _mem_kv_kernel,
        grid=(B,),
        in_specs=[pl.BlockSpec((1, M, D_MODEL), lambda b: (b, 0, 0))] + specs,
        out_specs=[pl.BlockSpec((1, M, MEM_W), lambda b: (b, 0, 0)), pl.BlockSpec((1, MEM_W, M), lambda b: (b, 0, 0))],
        out_shape=[jax.ShapeDtypeStruct((B, M, MEM_W), BF16), jax.ShapeDtypeStruct((B, MEM_W, M), BF16)],
        compiler_params=_params(("arbitrary",)),
        name="mem_kv",
    )(mem, g, w)


def _compress_kernel(kx_ref, vx_ref, wtop_ref, wbot_ref, ptop_ref, pbot_ref, b1_ref, w2_ref, b2_ref, rc_ref, rs_ref,
                     kc_ref, vcT_ref):
    for kvi, x_ref in enumerate((kx_ref, vx_ref)):
        x = x_ref[0]
        xt = (x + ptop_ref[kvi]).astype(BF16)
        xb = (x + pbot_ref[kvi]).astype(BF16)
        out = b2_ref[kvi]
        for g in range(N_KV_GROUPS):
            top = _dot(xt, wtop_ref[kvi, g])
            bot = _dot(xb, wbot_ref[kvi, g])
            hid = top + pltpu.roll(bot, bot.shape[0] - 1, 0) + b1_ref[kvi]
            out = out + _dot(jax.nn.gelu(hid).astype(BF16), w2_ref[kvi, g])
        if kvi == 0:
            kc_ref[0] = _rope_lanes(out, rc_ref[...], rs_ref[...]).astype(BF16)
        else:
            vcT_ref[0] = out.T.astype(BF16)


def _compress(kx, vx, params, rc, rs):
    B, NC, W = kx.shape
    xs = pl.BlockSpec((1, NC, W), lambda b: (b, 0, 0))
    arrs, specs = _split_params(params)
    return pl.pallas_call(
        _compress_kernel,
        grid=(B,),
        in_specs=[xs, xs] + specs + [_const_spec(rc.shape), _const_spec(rs.shape)],
        out_specs=[pl.BlockSpec((1, NC, LANES), lambda b: (b, 0, 0)), pl.BlockSpec((1, LANES, NC), lambda b: (b, 0, 0))],
        out_shape=[jax.ShapeDtypeStruct((B, NC, LANES), BF16), jax.ShapeDtypeStruct((B, LANES, NC), BF16)],
        compiler_params=_params(("arbitrary",)),
        name="nsa_compress",
    )(kx, vx, *arrs, rc, rs)


def _online_tile(s, m_ref, lead, cols):
    m_old = m_ref[lead, :, cols]
    m_new = jnp.maximum(m_old, jnp.max(s, axis=0, keepdims=True))
    m_ref[lead, :, cols] = m_new
    return jnp.exp2(s - m_new).astype(BF16), jnp.exp2(m_old - m_new)


def _reset_state(m_ref, acc_ref):
    m_ref[...] = jnp.full(m_ref.shape, NEG, F32)
    acc_ref[...] = jnp.zeros(acc_ref.shape, F32)


def _with_ones(vT):
    return jnp.concatenate([vT, jnp.ones((ACC_ROWS - HEAD_DIM, vT.shape[1]), vT.dtype)], axis=0)


def _normalise(acc):
    return acc[0:HEAD_DIM] / jnp.maximum(acc[HEAD_DIM:HEAD_DIM + 1], 1e-30)


def _nsa_attn_kernel(qT_ref, gT_ref, kc_ref, vcT_ref, kslc_ref, kwin_ref, vslcT_ref, vwinT_ref, ovT_ref,
                     o_ref, sel_ref, m_ref, acc_ref):
    c = pl.program_id(1)
    n_sel = sel_ref.shape[1]
    n_cmp = kc_ref.shape[1]
    GQ = HEADS_PER_GROUP * QB
    GW = HEADS_PER_GROUP * HEAD_DIM
    t_row = c * QB + lax.broadcasted_iota(jnp.int32, (1, QB), 1)
    gates = gT_ref[0]

    q6 = [jnp.concatenate([qT_ref[0, (HEADS_PER_GROUP * g + hh) * HEAD_DIM:(HEADS_PER_GROUP * g + hh + 1) * HEAD_DIM, :]
                           for hh in range(HEADS_PER_GROUP)], axis=1) for g in range(N_KV_GROUPS)]
    z = jnp.zeros_like(q6[0])
    qbd = jnp.concatenate([jnp.concatenate([q6[0], z], axis=1), jnp.concatenate([z, q6[1]], axis=1)], axis=0)

    cmp_end = lax.broadcasted_iota(jnp.int32, (n_cmp, QB), 0) * CMP_STRIDE + (L_CMP - 1)
    valid_c = cmp_end <= t_row
    valid_cf = valid_c.astype(F32)
    s_cmp = _dot(kc_ref[0], qbd)
    j_idx = lax.broadcasted_iota(jnp.int32, (n_sel, QB), 0)
    blk_t = lax.shift_right_logical(t_row, 6)
    forced = ((j_idx == 0) | (j_idx == blk_t) | (j_idx == blk_t - 1)).astype(F32)
    o_cmp = []
    for g in range(N_KV_GROUPS):
        psum = jnp.zeros((n_cmp, QB), F32)
        ps = []
        for hh in range(HEADS_PER_GROUP):
            col = g * GQ + hh * QB
            sh = jnp.where(valid_c, s_cmp[:, col:col + QB], NEG)
            e = jnp.exp2(sh - jnp.max(sh, axis=0, keepdims=True)) * valid_cf
            p = e / jnp.maximum(jnp.sum(e, axis=0, keepdims=True), 1e-30)
            psum = psum + p
            ps.append(p.astype(BF16))
        o_cmp.append(_dot(vcT_ref[0, g * HEAD_DIM:(g + 1) * HEAD_DIM, :], jnp.concatenate(ps, axis=1)))
        p_hi = psum.astype(BF16)
        p_lo = (psum - p_hi.astype(F32)).astype(BF16)
        imp = _dot(ovT_ref[...], p_hi) + _dot(ovT_ref[...], p_lo)
        score = jnp.where(j_idx <= blk_t, imp + FORCE_BONUS * forced, NEG)
        cnt = jnp.zeros((n_sel, QB), F32)
        for i in range(n_sel):
            si = score[i:i + 1, :]
            cnt = cnt + ((si > score) | ((si == score) & (j_idx > i))).astype(F32)
        sel = (cnt < float(min(TOP_N, n_sel))).astype(F32)
        for i in range(n_sel):
            sel_ref[g, i] = jnp.broadcast_to(sel[i:i + 1, :], (8, QB))

    _reset_state(m_ref, acc_ref)
    key_off = lax.broadcasted_iota(jnp.int32, (NSA_KC, QB), 0)

    def scores(k_ref, kb):
        return _dot(k_ref[0, kb * NSA_KC:(kb + 1) * NSA_KC, :], qbd)

    def consume(s, br, vT_ref, kb, masks):
        for g in range(N_KV_GROUPS):
            ps, alphas = [], []
            for hh in range(HEADS_PER_GROUP):
                col = g * GQ + hh * QB
                tile = s[:, col:col + QB]
                if masks[g] is not None:
                    tile = jnp.where(masks[g], tile, MASKED)
                p, a = _online_tile(tile, m_ref, br, slice(col, col + QB))
                ps.append(p)
                alphas.append(a)
            gc = slice(g * GQ, (g + 1) * GQ)
            vT = _with_ones(vT_ref[0, kb, g * HEAD_DIM:(g + 1) * HEAD_DIM, :])
            acc_ref[br, :, gc] = acc_ref[br, :, gc] * jnp.concatenate(alphas, axis=1) + _dot(vT, jnp.concatenate(ps, axis=1))

    def slc_masks(kb, causal):
        per_chunk = NSA_KC // L_SEL
        masks = []
        for g in range(N_KV_GROUPS):
            picked = sel_ref[g, per_chunk * kb + per_chunk - 1][0:1, :]
            for i in reversed(range(per_chunk - 1)):
                picked = jnp.where(key_off < (i + 1) * L_SEL, sel_ref[g, per_chunk * kb + i][0:1, :], picked)
            mk = picked > 0.5
            masks.append(mk & (kb * NSA_KC + key_off <= t_row) if causal else mk)
        return masks

    def win_masks(kb, last):
        key_pos = kb * NSA_KC + key_off
        if kb == last:
            wm = key_pos <= t_row
        elif kb == last - WINDOW // NSA_KC:
            wm = t_row - key_pos < WINDOW
        else:
            wm = None
        return [wm, wm]

    last_dyn = lax.shift_right_logical(c * QB, NSA_KC.bit_length() - 1)
    for last in range(kslc_ref.shape[1] // NSA_KC):
        @pl.when(last_dyn == last)
        def _(last=last):
            first_win = max(last - WINDOW // NSA_KC, 0)
            plan = [(0, kslc_ref, vslcT_ref, kb, False) for kb in range(first_win)]
            for kb in range(first_win, last + 1):
                plan += [(0, kslc_ref, vslcT_ref, kb, kb == last), (1, kwin_ref, vwinT_ref, kb, None)]
            s_next = scores(plan[0][1], plan[0][3])
            for i, (br, _, vT_ref, kb, causal) in enumerate(plan):
                s_cur = s_next
                if i + 1 < len(plan):
                    s_next = scores(plan[i + 1][1], plan[i + 1][3])
                consume(s_cur, br, vT_ref, kb, slc_masks(kb, causal) if br == 0 else win_masks(kb, last))

    o_slc = _normalise(acc_ref[0])
    o_win = _normalise(acc_ref[1])

    for g in range(N_KV_GROUPS):
        outs = []
        for hh in range(HEADS_PER_GROUP):
            r = 3 * (HEADS_PER_GROUP * g + hh)
            cols = slice(g * GQ + hh * QB, g * GQ + (hh + 1) * QB)
            outs.append(gates[r:r + 1] * o_cmp[g][:, hh * QB:(hh + 1) * QB] + gates[r + 1:r + 2] * o_slc[:, cols]
                        + gates[r + 2:r + 3] * o_win[:, cols])
        o_ref[0, :, g * GW:(g + 1) * GW] = jnp.concatenate(outs, axis=0).T.astype(BF16)


def _nsa_attn(qT, gT, kc, vcT, kslc, kwin, vslcT, vwinT, ovT):
    B, _, S = qT.shape
    n_sel = S // L_SEL
    W12 = N_MIX_HEADS * QB
    per_b3 = lambda a: pl.BlockSpec((1,) + a.shape[1:], lambda b, c: (b, 0, 0))
    per_b4 = lambda a: pl.BlockSpec((1,) + a.shape[1:], lambda b, c: (b, 0, 0, 0))
    return pl.pallas_call(
        _nsa_attn_kernel,
        grid=(B, S // QB),
        in_specs=[pl.BlockSpec((1, MIX_W, QB), lambda b, c: (b, 0, c)),
                  pl.BlockSpec((1, GATE_ROWS, QB), lambda b, c: (b, 0, c)),
                  per_b3(kc), per_b3(vcT), per_b3(kslc), per_b3(kwin), per_b4(vslcT), per_b4(vwinT),
                  _const_spec(ovT.shape)],
        out_specs=pl.BlockSpec((1, QB, MIX_W), lambda b, c: (b, c, 0)),
        out_shape=jax.ShapeDtypeStruct((B, S, MIX_W), BF16),
        scratch_shapes=[pltpu.VMEM((N_KV_GROUPS, n_sel, 8, QB), F32), pltpu.VMEM((2, 1, W12), F32),
                        pltpu.VMEM((2, ACC_ROWS, W12), F32)],
        compiler_params=_params(("arbitrary", "arbitrary")),
        name="nsa_attn",
    )(qT, gT, kc, vcT, kslc, kwin, vslcT, vwinT, ovT)


def _pair_q(q2, width):
    z = jnp.zeros((HEAD_DIM, width), q2.dtype)
    return jnp.concatenate([jnp.concatenate([q2[0:HEAD_DIM], z], axis=1),
                            jnp.concatenate([z, q2[HEAD_DIM:2 * HEAD_DIM]], axis=1)], axis=0)


def _fox_attn_kernel(qT_ref, k_ref, vT_ref, dcum_ref, dcumT_ref, o_ref, kbias_ref, m_ref, acc_ref):
    p = pl.program_id(1)
    c = pl.program_id(2)
    nq = FOX_TQ // QB
    per_q = FOX_TQ // FOX_KC

    @pl.when(c == 0)
    def _():
        parts = _split3(dcum_ref[0] * LOG2E)
        src = lax.broadcasted_iota(jnp.int32, (LANES, LANES), 0)
        dst = lax.broadcasted_iota(jnp.int32, (LANES, LANES), 1)
        lane = lax.broadcasted_iota(jnp.int32, (1, LANES), 1)
        acc = ((lane >= 3) & (lane < 6)).astype(F32)
        for r in range(3):
            place = ((src == 2 * p) & (dst == r)) | ((src == 2 * p + 1) & (dst == 6 + r))
            acc = acc + _dot(parts[r], place.astype(BF16))
        kbias_ref[...] = acc.astype(BF16)

    rowi = lax.broadcasted_iota(jnp.int32, (16, 2 * FOX_TQ), 0)
    head1 = lax.broadcasted_iota(jnp.int32, (16, 2 * FOX_TQ), 1) >= FOX_TQ
    bias = jnp.where(((rowi < 3) & ~head1) | ((rowi >= 6) & (rowi < 9) & head1), -1.0, 0.0)
    dq = [_split3(dcumT_ref[0, pl.ds(2 * p + hh, 1), :] * LOG2E) for hh in range(2)]
    for r in range(3):
        bias = jnp.where(rowi == 3 + r, jnp.concatenate([dq[0][r], dq[1][r]], axis=1).astype(F32), bias)
    qaug = jnp.concatenate([_pair_q(qT_ref[0], FOX_TQ), bias.astype(BF16),
                            jnp.zeros((LANES - 16, 2 * FOX_TQ), BF16)], axis=0)

    _reset_state(m_ref, acc_ref)
    lane_t =lax.broadcasted_iota(jnp.int32, (1, QB), 1)

    def scores(kb, n_kc, qs_lo, masked):
        start, size = kb * FOX_KC, n_kc * FOX_KC
        kk = jnp.concatenate([k_ref[0, pl.ds(start, size), :], kbias_ref[pl.ds(start, size), :]], axis=1)
        live = FOX_TQ - qs_lo * QB
        if qs_lo == 0:
            s2 = _dot(kk, qaug)
            return [s2[:, 0:FOX_TQ], s2[:, FOX_TQ:2 * FOX_TQ]]
        return [_dot(kk, qaug[:, (hh + 1) * FOX_TQ - live:(hh + 1) * FOX_TQ]) for hh in range(2)]

    def consume(s, kb, n_kc, qs_lo, masked):
        live = FOX_TQ - qs_lo * QB
        if masked:
            key_pos = kb * FOX_KC + lax.broadcasted_iota(jnp.int32, (n_kc * FOX_KC, QB), 0)
        for hh in range(2):
            cols = slice((hh + 1) * FOX_TQ - live, (hh + 1) * FOX_TQ)
            ps, alphas = [], []
            for qs in range(qs_lo, nq):
                tile = s[hh][:, (qs - qs_lo) * QB:(qs - qs_lo + 1) * QB]
                if masked:
                    tile = jnp.where(key_pos <= c * FOX_TQ + qs * QB + lane_t, tile, MASKED)
                col = hh * FOX_TQ + qs * QB
                pr, a = _online_tile(tile, m_ref, 0, slice(col, col + QB))
                ps.append(pr)
                alphas.append(a)
            vT = jnp.concatenate([vT_ref[0, kb + i, hh * HEAD_DIM:(hh + 1) * HEAD_DIM, :] for i in range(n_kc)], axis=1)
            acc_ref[0, :, cols] = (acc_ref[0, :, cols] * jnp.concatenate(alphas, axis=1)
                                   + _dot(_with_ones(vT), jnp.concatenate(ps, axis=1)))

    for cc in range(dcum_ref.shape[1] // FOX_TQ):
        @pl.when(c == cc)
        def _(cc=cc):
            plan = [(i * per_q, per_q, 0, False) for i in range(cc)]
            plan += [(cc * per_q + i, 1, i * FOX_KC // QB, True) for i in range(per_q)]
            s_next = scores(*plan[0])
            for i, step in enumerate(plan):
                s_cur = s_next
                if i + 1 < len(plan):
                    s_next = scores(*plan[i + 1])
                consume(s_cur, *step)
    o = _normalise(acc_ref[0])
    oT =jnp.concatenate([o[:, 0:FOX_TQ], o[:, FOX_TQ:2 * FOX_TQ]], axis=0)
    o_ref[0] = oT.T.astype(BF16)


def _fox_attn(qT, k, vT, dcum, dcumT):
    B, _, S = qT.shape
    n_pairs = N_MIX_HEADS // 2
    return pl.pallas_call(
        _fox_attn_kernel,
        grid=(B, n_pairs, S // FOX_TQ),
        in_specs=[pl.BlockSpec((1, LANES, FOX_TQ), lambda b, p, c: (b, p, c)),
                  pl.BlockSpec((1, S, LANES), lambda b, p, c: (b, 0, p)),
                  pl.BlockSpec((1, S // FOX_KC, LANES, FOX_KC), lambda b, p, c: (b, 0, p, 0)),
                  pl.BlockSpec((1, S, LANES), lambda b, p, c: (b, 0, 0)),
                  pl.BlockSpec((1, 16, FOX_TQ), lambda b, p, c: (b, 0, c))],
        out_specs=pl.BlockSpec((1, FOX_TQ, LANES), lambda b, p, c: (b, c, p)),
        out_shape=jax.ShapeDtypeStruct((B, S, MIX_W), BF16),
        scratch_shapes=[pltpu.VMEM((S, LANES), BF16), pltpu.VMEM((1, 1, 2 * FOX_TQ), F32),
                        pltpu.VMEM((1, ACC_ROWS, 2 * FOX_TQ), F32)],
        compiler_params=_params(("arbitrary", "arbitrary", "arbitrary")),
        name="fox_attn",
    )(qT, k, vT, dcum, dcumT)


def _mem_attn_kernel(qT_ref, k_ref, vT_ref, o_ref):
    outs = []
    for pr in range(N_MEM_HEADS // 2):
        qbd = _pair_q(qT_ref[0, pr * LANES:(pr + 1) * LANES, :], MEM_TQ)
        s = _dot(k_ref[0, :, pr * LANES:(pr + 1) * LANES], qbd)
        for hh in range(2):
            ps = []
            for qs in range(MEM_TQ // QB):
                col = hh * MEM_TQ + qs * QB
                sh = s[:, col:col + QB]
                e = jnp.exp2(sh - jnp.max(sh, axis=0, keepdims=True))
                ps.append((e / jnp.sum(e, axis=0, keepdims=True)).astype(BF16))
            h = 2 * pr + hh
            outs.append(_dot(vT_ref[0, h * HEAD_DIM:(h + 1) * HEAD_DIM, :], jnp.concatenate(ps, axis=1)))
    o_ref[0] = jnp.concatenate(outs, axis=0).T.astype(BF16)


def _mem_attn(qmT, kmem, vmemT):
    B, _, S = qmT.shape
    M = kmem.shape[1]
    return pl.pallas_call(
        _mem_attn_kernel,
        grid=(B, S // MEM_TQ),
        in_specs=[pl.BlockSpec((1, MEM_W, MEM_TQ), lambda b, c: (b, 0, c)),
                  pl.BlockSpec((1, M, MEM_W), lambda b, c: (b, 0, 0)),
                  pl.BlockSpec((1, MEM_W, M), lambda b, c: (b, 0, 0))],
        out_specs=pl.BlockSpec((1, MEM_TQ, MEM_W), lambda b, c: (b, c, 0)),
        out_shape=jax.ShapeDtypeStruct((B, S, MEM_W), BF16),
        compiler_params=_params(("arbitrary", "arbitrary")),
        name="mem_attn",
    )(qmT, kmem, vmemT)


def _ffn_kernel(x_ref, omix_ref, omem_ref, wo1_ref, wo2_ref, g_ref, wa_ref, wb_ref, cwa_ref, cwb_ref, cba_ref,
                cbb_ref, wd_ref, gfin_ref, o_ref, ca_ref, cb_ref, gate_ref, *, final):
    j = pl.program_id(1)
    x1 = x_ref[0] + _dot(omix_ref[0], wo1_ref[...]) + _dot(omem_ref[0], wo2_ref[...])
    h = _rms(x1, g_ref[...]).astype(BF16)

    @pl.when(j == 0)
    def _():
        ca_ref[...] = jnp.zeros(ca_ref.shape, F32)
        cb_ref[...] = jnp.zeros(cb_ref.shape, F32)

    def conv(u, prev8, w, bias):
        ext = jnp.concatenate([prev8, u], axis=0)
        u1 = pltpu.roll(ext, 1, 0)[8:]
        u2 = pltpu.roll(ext, 2, 0)[8:]
        return w[0:1] * u2 + w[1:2] * u1 + w[2:3] * u + bias

    for ci in range(D_FF // FF_CHUNK):
        sl = slice(ci * FF_CHUNK, (ci + 1) * FF_CHUNK)
        a = _dot(h, wa_ref[:, sl])
        b = _dot(h, wb_ref[:, sl])
        pa = ca_ref[:, sl]
        pb = cb_ref[:, sl]
        ca_ref[:, sl] = a[FF_TM - 8:FF_TM]
        cb_ref[:, sl] = b[FF_TM - 8:FF_TM]
        ac = conv(a, pa, cwa_ref[:, sl], cba_ref[:, sl])
        bc = conv(b, pb, cwb_ref[:, sl], cbb_ref[:, sl])
        gate_ref[:, sl] = (ac * jax.nn.sigmoid(ac) * bc).astype(BF16)
    y = x1 + _dot(gate_ref[...], wd_ref[...])
    o_ref[0] = _rms(y, gfin_ref[...]) if final else y


def _ffn(x, omix, omem, params, final):
    B, S, _ = x.shape
    tok = lambda w: pl.BlockSpec((1, FF_TM, w), lambda b, j: (b, j, 0))
    arrs, specs = _split_params(params)
    return pl.pallas_call(
        functools.partial(_ffn_kernel, final=final),
        grid=(B, S // FF_TM),
        in_specs=[tok(D_MODEL), tok(MIX_W), tok(MEM_W)] + specs,
        out_specs=tok(D_MODEL),
        out_shape=jax.ShapeDtypeStruct((B, S, D_MODEL), F32),
        scratch_shapes=[pltpu.VMEM((8, D_FF), F32), pltpu.VMEM((8, D_FF), F32), pltpu.VMEM((FF_TM, D_FF), BF16)],
        compiler_params=_params(("arbitrary", "arbitrary")),
        name="outproj_convffn",
    )(x, omix, omem, *arrs)


def _rope_tables(pos):
    half = HEAD_DIM // 2
    inv = ROPE_THETA ** (-jnp.arange(half, dtype=F32) / half)
    ang = pos.astype(F32)[:, None] * inv[None, :]
    cos, sin = jnp.cos(ang), jnp.sin(ang)
    reps = LANES // half
    signs = jnp.tile(jnp.concatenate([-jnp.ones((half,), F32), jnp.ones((half,), F32)]), LANES // HEAD_DIM)
    return cos.T, sin.T, jnp.tile(cos, (1, reps)), jnp.tile(sin, (1, reps)) * signs[None, :]


def _overlap_T(S, n_cmp_pad):
    n_cmp = (S - L_CMP) // CMP_STRIDE + 1
    n_sel = S // L_SEL
    cs = np.arange(n_cmp_pad) * CMP_STRIDE
    ss = np.arange(n_sel) * L_SEL
    ov = (cs[None, :] < ss[:, None] + L_SEL) & (cs[None, :] + L_CMP > ss[:, None]) & (np.arange(n_cmp_pad)[None, :] < n_cmp)
    return jnp.asarray(ov, dtype=BF16)


def _compress_params(pos, w1, b1, w2, b2):
    half = L_CMP // 2
    w1r = w1.reshape(2, L_CMP, HEAD_DIM, CMP_HIDDEN)

    def spread(w):
        z = jnp.zeros((2, N_KV_GROUPS, half, N_KV_GROUPS, HEAD_DIM, CMP_HIDDEN), F32)
        for g in range(N_KV_GROUPS):
            z = z.at[:, g, :, g].set(w)
        return z.reshape(2, N_KV_GROUPS, half * LANES, CMP_HIDDEN).astype(BF16)

    def spread_pos(p):
        return jnp.tile(p[:, :, None, :], (1, 1, N_KV_GROUPS, 1)).reshape(2, 1, half * LANES)

    w2p = jnp.zeros((2, N_KV_GROUPS, CMP_HIDDEN, N_KV_GROUPS, HEAD_DIM), F32)
    for g in range(N_KV_GROUPS):
        w2p = w2p.at[:, g, :, g].set(w2)
    return (spread(w1r[:, :half]), spread(w1r[:, half:]), spread_pos(pos[:, :half]), spread_pos(pos[:, half:]),
            b1[:, None, :], w2p.reshape(2, N_KV_GROUPS, CMP_HIDDEN, LANES).astype(BF16),
            jnp.tile(b2, (1, N_KV_GROUPS))[:, None, :])


def _pad_cols(w, width):
    return jnp.pad(w, ((0, 0), (0, width - w.shape[1])))


def kernel(x, mem, attn_norm, ffn_norm, mem_norm, w_mem_kv, w_o, w_up, conv_w, conv_b, w_down, a_w_in, a_gate_b,
           a_cmp_pos, a_cmp_w1, a_cmp_b1, a_cmp_w2, a_cmp_b2, b_w_in, kv_norm, w_kv_shared, b_fgate, final_norm):
    B, S, _ = x.shape
    assert S % FOX_TQ == 0 and S % (CMP_STRIDE * 8) == 0
    n_chunks = S // CMP_STRIDE
    cosT, sinT, rc, rs = _rope_tables(jnp.arange(S))
    _, _, rc_cmp, rs_cmp = _rope_tables(jnp.arange(n_chunks) * CMP_STRIDE + L_CMP - 1)
    ovT = _overlap_T(S, n_chunks)

    rows3 = lambda v: v.reshape(v.shape[0], 1, -1).astype(F32)
    layer = lambda a, l, block, *idx: _w(a, (None,) + block, l, *idx)
    gain = lambda a, l: layer(a, l, (1, D_MODEL), 0, 0)
    attn_g, ffn_g, mem_g = rows3(attn_norm), rows3(ffn_norm), rows3(mem_norm)
    w_mem_b, w_o_b, w_up_b, w_down_b = (a.astype(BF16) for a in (w_mem_kv, w_o, w_up, w_down))
    a_w_b, b_w_b, w_sh_b = a_w_in.astype(BF16), b_w_in.astype(BF16), w_kv_shared.astype(BF16)
    conv_b3 = rows3(conv_b)
    a_wg = jnp.pad(a_w_b[:, :, 2 * MIX_W:2 * MIX_W + GATE_W], ((0, 0), (0, 0), (0, LANES - GATE_W)))
    a_gb = jnp.pad(rows3(a_gate_b), ((0, 0), (0, 0), (0, LANES - GATE_W)))
    a_wqm = a_w_b[:, :, 2 * MIX_W + GATE_W:]
    cmp_params = jax.vmap(_compress_params)(a_cmp_pos, a_cmp_w1, a_cmp_b1, a_cmp_w2, a_cmp_b2)
    w_f = _pad_cols(w_sh_b[:, 2 * MIX_W:], LANES)
    b_f = _pad_cols(b_fgate[None, :].astype(F32), LANES)
    fin_g = final_norm.reshape(1, -1).astype(F32)

    k_sh = vT_sh = dcum = dcumT = None
    for l in range(DEPTH):
        kmem, vmemT = _mem_kv(mem, [gain(mem_g, l), layer(w_mem_b, l, (D_MODEL, 2 * MEM_W), 0, 0)])
        if l < N_A:
            qT, kcmp, vcmp, kslc, kwin, vslcT, vwinT, gT, qmT = _proj_nsa(
                x, [gain(attn_g, l), layer(a_w_b, l, (D_MODEL, MIX_W), 0, 0), layer(a_w_b, l, (D_MODEL, MIX_W), 0, 1),
                    layer(a_wg, l, (D_MODEL, LANES), 0, 0), layer(a_gb, l, (1, LANES), 0, 0),
                    layer(a_wqm, l, (D_MODEL, MEM_W), 0, 0)],
                cosT, sinT, rc, rs)
            kc, vcT = _compress(kcmp.reshape(B, n_chunks, CMP_STRIDE * LANES), vcmp.reshape(B, n_chunks, CMP_STRIDE * LANES),
                                [layer(a, l, a.shape[1:], *(0,) * (a.ndim - 1)) for a in cmp_params], rc_cmp, rs_cmp)
            omix = _nsa_attn(qT, gT, kc, vcT, kslc, kwin, vslcT, vwinT, ovT)
        else:
            if l == N_A:
                k_sh, vT_sh, dcum, dcumT = _kv_shared(
                    x, [_w(kv_norm.reshape(1, -1).astype(F32)), _w(w_sh_b, (D_MODEL, MIX_W), 0, 0),
                        _w(w_sh_b, (D_MODEL, MIX_W), 0, 1), _w(w_f), _w(b_f)])
            qT, qmT = _proj_fox(x, [gain(attn_g, l), layer(b_w_b, l - N_A, (D_MODEL, MIX_W), 0, 0),
                                    layer(b_w_b, l - N_A, (D_MODEL, MEM_W), 0, MIX_W // MEM_W)])
            omix = _fox_attn(qT, k_sh, vT_sh, dcum, dcumT)
        omem = _mem_attn(qmT, kmem, vmemT)
        x = _ffn(x, omix, omem,
                 [layer(w_o_b, l, (MIX_W, D_MODEL), 0, 0), layer(w_o_b, l, (MEM_W, D_MODEL), MIX_W // MEM_W, 0),
                  gain(ffn_g, l), layer(w_up_b, l, (D_MODEL, D_FF), 0, 0), layer(w_up_b, l, (D_MODEL, D_FF), 0, 1),
                  layer(conv_w, l, (CONV_WIDTH, D_FF), 0, 0), layer(conv_w, l, (CONV_WIDTH, D_FF), 0, 1),
                  layer(conv_b3, l, (1, D_FF), 0, 0), layer(conv_b3, l, (1, D_FF), 0, 1),
                  layer(w_down_b, l, (D_FF, D_MODEL), 0, 0), _w(fin_g)],
                 final=(l == DEPTH - 1))
    return x
```

```python
import functools

import numpy as np
import jax
import jax.numpy as jnp
from jax import lax
from jax.experimental import pallas as pl
from jax.experimental.pallas import tpu as pltpu

D_MODEL = 1024
HEAD_DIM = 64
N_MIX_HEADS = 12
N_KV_GROUPS = 2
HEADS_PER_GROUP = N_MIX_HEADS // N_KV_GROUPS
N_MEM_HEADS = 4
L_CMP = 32
CMP_STRIDE = 16
CMP_HIDDEN = 256
L_SEL = 64
TOP_N = 16
WINDOW = 512
D_FF = 2816
CONV_WIDTH = 3
ROPE_THETA = 10000.0
EPS = 1e-6
NEG = -1e30
FORCE_BONUS = 1e4
MIX_W = N_MIX_HEADS * HEAD_DIM
MEM_W = N_MEM_HEADS * HEAD_DIM
GATE_W = 3 * N_MIX_HEADS
N_A = 2
DEPTH = 4
LOG2E = 1.4426950408889634
Q_MUL = HEAD_DIM ** -0.5 * LOG2E
ACC_ROWS = HEAD_DIM + 16

LANES = 128
TM = 512
QB = 128
NSA_KC = 256
MASKED = 2.0 * NEG
FOX_TQ = 512
FOX_KC = 256
FF_TM = 512
FF_CHUNK = 256
GATE_ROWS = 40
VMEM_LIMIT = 56 * 1024 * 1024

F32 = jnp.float32
BF16 = jnp.bfloat16


def _params(sem):
    return pltpu.CompilerParams(dimension_semantics=sem, vmem_limit_bytes=VMEM_LIMIT)


def _const_spec(shape):
    n = len(shape)
    return pl.BlockSpec(shape, lambda *_: (0,) * n, pipeline_mode=pl.Buffered(1))


def _w(arr, block=None, *idx):
    if block is None:
        block, idx = arr.shape, (0,) * arr.ndim
    return arr, pl.BlockSpec(block, lambda *_: idx, pipeline_mode=pl.Buffered(1))


def _split_params(params):
    return [a for a, _ in params], [s for _, s in params]


def _rms(x, g):
    return x * lax.rsqrt(jnp.mean(x * x, axis=-1, keepdims=True) + EPS) * g


def _dot(a, b):
    return jnp.dot(a, b, preferred_element_type=F32)


def _rope_lanes(x, cos, sin_signed):
    lane = lax.broadcasted_iota(jnp.int32, x.shape, 1)
    first_half = (lane & (HEAD_DIM - 1)) < (HEAD_DIM // 2)
    partner = jnp.where(first_half, pltpu.roll(x, LANES - HEAD_DIM // 2, 1), pltpu.roll(x, HEAD_DIM // 2, 1))
    return x * cos + partner * sin_signed


def _split3(x):
    a = x.astype(BF16)
    r = x - a.astype(F32)
    b = r.astype(BF16)
    c = (r - b.astype(F32)).astype(BF16)
    return a, b, c


def _proj_nsa_kernel(x_ref, g_ref, wq_ref, wkv_ref, wg_ref, gb_ref, wqm_ref, cosT_ref, sinT_ref, rc_ref, rs_ref,
                     qT_ref, kcmp_ref, vcmp_ref, kslc_ref, kwin_ref, vslcT_ref, vwinT_ref, gT_ref, qmT_ref, cmp_ref):
    h = _rms(x_ref[0], g_ref[...]).astype(BF16)
    qT = (_dot(h, wq_ref[...]) * Q_MUL).T
    cos = cosT_ref[...]
    sin = sinT_ref[...]
    half = HEAD_DIM // 2
    for hh in range(N_MIX_HEADS):
        x1 = qT[hh * HEAD_DIM:hh * HEAD_DIM + half]
        x2 = qT[hh * HEAD_DIM + half:(hh + 1) * HEAD_DIM]
        qT_ref[0, hh * HEAD_DIM:hh * HEAD_DIM + half, :] = (x1 * cos - x2 * sin).astype(BF16)
        qT_ref[0, hh * HEAD_DIM + half:(hh + 1) * HEAD_DIM, :] = (x2 * cos + x1 * sin).astype(BF16)
    kv = _dot(h, wkv_ref[...])
    rc = rc_ref[...]
    rs = rs_ref[...]
    cmp_ref[0] = kv[:, 0:128]
    cmp_ref[1] = kv[:, 128:256]
    for r in range(CMP_STRIDE):
        kcmp_ref[0, :, r * LANES:(r + 1) * LANES] = cmp_ref[0, pl.ds(r, TM // CMP_STRIDE, stride=CMP_STRIDE), :]
        vcmp_ref[0, :, r * LANES:(r + 1) * LANES] = cmp_ref[1, pl.ds(r, TM // CMP_STRIDE, stride=CMP_STRIDE), :]
    kslc_ref[0] = _rope_lanes(kv[:, 256:384], rc, rs).astype(BF16)
    kwin_ref[0] = _rope_lanes(kv[:, 512:640], rc, rs).astype(BF16)
    vsT = kv[:, 384:512].T.astype(BF16)
    vwT = kv[:, 640:768].T.astype(BF16)
    for j in range(TM // NSA_KC):
        vslcT_ref[0, j] = vsT[:, j * NSA_KC:(j + 1) * NSA_KC]
        vwinT_ref[0, j] = vwT[:, j * NSA_KC:(j + 1) * NSA_KC]
    gates = jax.nn.sigmoid(_dot(h, wg_ref[...]) + gb_ref[...])
    gT_ref[0] = gates.T[0:GATE_ROWS]
    qmT_ref[0] = (_dot(h, wqm_ref[...]) * Q_MUL).T.astype(BF16)


def _proj_nsa(x, params, cosT, sinT, rc, rs):
    B, S, _ = x.shape
    nt = S // TM
    arrs, specs = _split_params(params)
    tok = lambda w: pl.BlockSpec((1, TM, w), lambda b, j: (b, j, 0))
    featT = lambda r: pl.BlockSpec((1, r, TM), lambda b, j: (b, 0, j))
    vT = pl.BlockSpec((1, TM // NSA_KC, LANES, NSA_KC), lambda b, j: (b, j, 0, 0))
    chunked = pl.BlockSpec((1, TM // CMP_STRIDE, CMP_STRIDE * LANES), lambda b, j: (b, j, 0))
    return pl.pallas_call(
        _proj_nsa_kernel,
        grid=(B, nt),
        in_specs=[tok(D_MODEL)] + specs + [
                  pl.BlockSpec((HEAD_DIM // 2, TM), lambda b, j: (0, j)),
                  pl.BlockSpec((HEAD_DIM // 2, TM), lambda b, j: (0, j)),
                  pl.BlockSpec((TM, LANES), lambda b, j: (j, 0)),
                  pl.BlockSpec((TM, LANES), lambda b, j: (j, 0))],
        out_specs=[featT(MIX_W), chunked, chunked, tok(LANES), tok(LANES), vT, vT, featT(GATE_ROWS),
                   featT(MEM_W)],
        out_shape=[jax.ShapeDtypeStruct((B, MIX_W, S), BF16),
                   jax.ShapeDtypeStruct((B, S // CMP_STRIDE, CMP_STRIDE * LANES), F32),
                   jax.ShapeDtypeStruct((B, S // CMP_STRIDE, CMP_STRIDE * LANES), F32),
                   jax.ShapeDtypeStruct((B, S, LANES), BF16), jax.ShapeDtypeStruct((B, S, LANES), BF16),
                   jax.ShapeDtypeStruct((B, S // NSA_KC, LANES, NSA_KC), BF16),
                   jax.ShapeDtypeStruct((B, S // NSA_KC, LANES, NSA_KC), BF16),
                   jax.ShapeDtypeStruct((B, GATE_ROWS, S), F32),
                   jax.ShapeDtypeStruct((B, MEM_W, S), BF16)],
        scratch_shapes=[pltpu.VMEM((2, TM, LANES), F32)],
        compiler_params=_params(("arbitrary", "arbitrary")),
        name="proj_nsa",
    )(x, *arrs, cosT, sinT, rc, rs)


def _proj_fox_kernel(x_ref, g_ref, wq_ref, wqm_ref, qT_ref, qmT_ref):
    h = _rms(x_ref[0], g_ref[...]).astype(BF16)
    qT_ref[0] = (_dot(h, wq_ref[...]) * Q_MUL).T.astype(BF16)
    qmT_ref[0] = (_dot(h, wqm_ref[...]) * Q_MUL).T.astype(BF16)


def _proj_fox(x, params):
    B, S, _ = x.shape
    arrs, specs = _split_params(params)
    featT = lambda r: pl.BlockSpec((1, r, TM), lambda b, j: (b, 0, j))
    return pl.pallas_call(
        _proj_fox_kernel,
        grid=(B, S // TM),
        in_specs=[pl.BlockSpec((1, TM, D_MODEL), lambda b, j: (b, j, 0))] + specs,
        out_specs=[featT(MIX_W), featT(MEM_W)],
        out_shape=[jax.ShapeDtypeStruct((B, MIX_W, S), BF16), jax.ShapeDtypeStruct((B, MEM_W, S), BF16)],
        compiler_params=_params(("arbitrary", "arbitrary")),
        name="proj_fox",
    )(x, *arrs)


def _kv_shared_kernel(x_ref, g_ref, wk_ref, wv_ref, wf_ref, bf_ref, k_ref, vT_ref, dcum_ref, dcumT_ref, carry_ref):
    j = pl.program_id(1)
    h = _rms(x_ref[0], g_ref[...]).astype(BF16)
    k_ref[0] = _dot(h, wk_ref[...]).astype(BF16)
    vT = _dot(h, wv_ref[...]).T.astype(BF16)
    for i in range(TM // FOX_KC):
        vT_ref[0, i] = vT[:, i * FOX_KC:(i + 1) * FOX_KC]
    z = _dot(h, wf_ref[...]) + bf_ref[...]
    logf = jnp.minimum(z, 0.0) - jnp.log1p(jnp.exp(-jnp.abs(z)))
    row = lax.broadcasted_iota(jnp.int32, (TM, TM), 0)
    col = lax.broadcasted_iota(jnp.int32, (TM, TM), 1)
    tri = (col <= row).astype(BF16)
    a, b, c = _split3(logf)
    @pl.when(j == 0)
    def _():
        carry_ref[...] = jnp.zeros(carry_ref.shape, F32)

    cs = _dot(tri, a) + _dot(tri, b) + _dot(tri, c) + carry_ref[7:8, :]
    carry_ref[...] = cs[TM - 8:TM]
    dcum_ref[0] = cs
    dcumT_ref[0] = cs.T[0:16]


def _kv_shared(x, params):
    B, S, _ = x.shape
    arrs, specs = _split_params(params)
    return pl.pallas_call(
        _kv_shared_kernel,
        grid=(B, S // TM),
        in_specs=[pl.BlockSpec((1, TM, D_MODEL), lambda b, j: (b, j, 0))] + specs,
        out_specs=[pl.BlockSpec((1, TM, MIX_W), lambda b, j: (b, j, 0)),
                   pl.BlockSpec((1, TM // FOX_KC, MIX_W, FOX_KC), lambda b, j: (b, j, 0, 0)),
                   pl.BlockSpec((1, TM, LANES), lambda b, j: (b, j, 0)),
                   pl.BlockSpec((1, 16, TM), lambda b, j: (b, 0, j))],
        out_shape=[jax.ShapeDtypeStruct((B, S, MIX_W), BF16),
                   jax.ShapeDtypeStruct((B, S // FOX_KC, MIX_W, FOX_KC), BF16),
                   jax.ShapeDtypeStruct((B, S, LANES), F32),
                   jax.ShapeDtypeStruct((B, 16, S), F32)],
        scratch_shapes=[pltpu.VMEM((8, LANES), F32)],
        compiler_params=_params(("arbitrary", "arbitrary")),
        name="kv_shared",
    )(x, *arrs)


def _mem_kv_kernel(m_ref, g_ref, w_ref, k_ref, vT_ref):
    h = _rms(m_ref[0], g_ref[...]).astype(BF16)
    kv = _dot(h, w_ref[...])
    k_ref[0] = kv[:, 0:MEM_W].astype(BF16)
    vT_ref[0] = kv[:, MEM_W:2 * MEM_W].T.astype(BF16)


def _mem_kv(mem, params):
    B, M, _ = mem.shape
    (g, w), specs = _split_params(params)
    return pl.pallas_call(
        _mem_kv_kernel,
        grid=(B,),
        in_specs=[pl.BlockSpec((1, M, D_MODEL), lambda b: (b, 0, 0))] + specs,
        out_specs=[pl.BlockSpec((1, M, MEM_W), lambda b: (b, 0, 0)), pl.BlockSpec((1, MEM_W, M), lambda b: (b, 0, 0))],
        out_shape=[jax.ShapeDtypeStruct((B, M, MEM_W), BF16), jax.ShapeDtypeStruct((B, MEM_W, M), BF16)],
        compiler_params=_params(("arbitrary",)),
        name="mem_kv",
    )(mem, g, w)


def _compress_kernel(kx_ref, vx_ref, wtop_ref, wbot_ref, ptop_ref, pbot_ref, b1_ref, w2_ref, b2_ref, rc_ref, rs_ref,
                     kc_ref, vcT_ref):
    for kvi, x_ref in enumerate((kx_ref, vx_ref)):
        x = x_ref[0]
        xt = (x + ptop_ref[kvi]).astype(BF16)
        xb = (x + pbot_ref[kvi]).astype(BF16)
        out = b2_ref[kvi]
        for g in range(N_KV_GROUPS):
            top = _dot(xt, wtop_ref[kvi, g])
            bot = _dot(xb, wbot_ref[kvi, g])
            hid = top + pltpu.roll(bot, bot.shape[0] - 1, 0) + b1_ref[kvi]
            out = out + _dot(jax.nn.gelu(hid).astype(BF16), w2_ref[kvi, g])
        if kvi == 0:
            kc_ref[0] = _rope_lanes(out, rc_ref[...], rs_ref[...]).astype(BF16)
        else:
            vcT_ref[0] = out.T.astype(BF16)


def _compress(kx, vx, params, rc, rs):
    B, NC, W = kx.shape
    xs = pl.BlockSpec((1, NC, W), lambda b: (b, 0, 0))
    arrs, specs = _split_params(params)
    return pl.pallas_call(
        _compress_kernel,
        grid=(B,),
        in_specs=[xs, xs] + specs + [_const_spec(rc.shape), _const_spec(rs.shape)],
        out_specs=[pl.BlockSpec((1, NC, LANES), lambda b: (b, 0, 0)), pl.BlockSpec((1, LANES, NC), lambda b: (b, 0, 0))],
        out_shape=[jax.ShapeDtypeStruct((B, NC, LANES), BF16), jax.ShapeDtypeStruct((B, LANES, NC), BF16)],
        compiler_params=_params(("arbitrary",)),
        name="nsa_compress",
    )(kx, vx, *arrs, rc, rs)


def _online_tile(s, m_ref, lead, cols):
    m_old = m_ref[lead, :, cols]
    m_new = jnp.maximum(m_old, jnp.max(s, axis=0, keepdims=True))
    m_ref[lead, :, cols] = m_new
    return jnp.exp2(s - m_new).astype(BF16), jnp.exp2(m_old - m_new)


def _reset_state(m_ref, acc_ref):
    m_ref[...] = jnp.full(m_ref.shape, NEG, F32)
    acc_ref[...] = jnp.zeros(acc_ref.shape, F32)


def _with_ones(vT):
    return jnp.concatenate([vT, jnp.ones((ACC_ROWS - HEAD_DIM, vT.shape[1]), vT.dtype)], axis=0)


def _normalise(acc):
    return acc[0:HEAD_DIM] / jnp.maximum(acc[HEAD_DIM:HEAD_DIM + 1], 1e-30)


def _nsa_attn_kernel(qT_ref, gT_ref, kc_ref, vcT_ref, kslc_ref, kwin_ref, vslcT_ref, vwinT_ref, ovT_ref,
                     o_ref, sel_ref, m_ref, acc_ref):
    c = pl.program_id(1)
    n_sel = sel_ref.shape[1]
    n_cmp = kc_ref.shape[1]
    GQ = HEADS_PER_GROUP * QB
    GW = HEADS_PER_GROUP * HEAD_DIM
    t_row = c * QB + lax.broadcasted_iota(jnp.int32, (1, QB), 1)
    gates = gT_ref[0]

    q6 = [jnp.concatenate([qT_ref[0, (HEADS_PER_GROUP * g + hh) * HEAD_DIM:(HEADS_PER_GROUP * g + hh + 1) * HEAD_DIM, :]
                           for hh in range(HEADS_PER_GROUP)], axis=1) for g in range(N_KV_GROUPS)]
    z = jnp.zeros_like(q6[0])
    qbd = jnp.concatenate([jnp.concatenate([q6[0], z], axis=1), jnp.concatenate([z, q6[1]], axis=1)], axis=0)

    cmp_end = lax.broadcasted_iota(jnp.int32, (n_cmp, QB), 0) * CMP_STRIDE + (L_CMP - 1)
    valid_c = cmp_end <= t_row
    valid_cf = valid_c.astype(F32)
    s_cmp = _dot(kc_ref[0], qbd)
    j_idx = lax.broadcasted_iota(jnp.int32, (n_sel, QB), 0)
    blk_t = lax.shift_right_logical(t_row, 6)
    forced = ((j_idx == 0) | (j_idx == blk_t) | (j_idx == blk_t - 1)).astype(F32)
    o_cmp = []
    for g in range(N_KV_GROUPS):
        psum = jnp.zeros((n_cmp, QB), F32)
        ps = []
        for hh in range(HEADS_PER_GROUP):
            col = g * GQ + hh * QB
            sh = jnp.where(valid_c, s_cmp[:, col:col + QB], NEG)
            e = jnp.exp2(sh - jnp.max(sh, axis=0, keepdims=True)) * valid_cf
            p = e / jnp.maximum(jnp.sum(e, axis=0, keepdims=True), 1e-30)
            psum = psum + p
            ps.append(p.astype(BF16))
        o_cmp.append(_dot(vcT_ref[0, g * HEAD_DIM:(g + 1) * HEAD_DIM, :], jnp.concatenate(ps, axis=1)))
        p_hi = psum.astype(BF16)
        p_lo = (psum - p_hi.astype(F32)).astype(BF16)
        imp = _dot(ovT_ref[...], p_hi) + _dot(ovT_ref[...], p_lo)
        score = jnp.where(j_idx <= blk_t, imp + FORCE_BONUS * forced, NEG)
        cnt = jnp.zeros((n_sel, QB), F32)
        for i in range(n_sel):
            si = score[i:i + 1, :]
            cnt = cnt + ((si > score) | ((si == score) & (j_idx > i))).astype(F32)
        sel = (cnt < float(min(TOP_N, n_sel))).astype(F32)
        for i in range(n_sel):
            sel_ref[g, i] = jnp.broadcast_to(sel[i:i + 1, :], (8, QB))

    _reset_state(m_ref, acc_ref)
    key_off = lax.broadcasted_iota(jnp.int32, (NSA_KC, QB), 0)

    def scores(k_ref, kb):
        return _dot(k_ref[0, kb * NSA_KC:(kb + 1) * NSA_KC, :], qbd)

    def consume(s, br, vT_ref, kb, masks):
        for g in range(N_KV_GROUPS):
            ps, alphas = [], []
            for hh in range(HEADS_PER_GROUP):
                col = g * GQ + hh * QB
                tile = s[:, col:col + QB]
                if masks[g] is not None:
                    tile = jnp.where(masks[g], tile, MASKED)
                p, a = _online_tile(tile, m_ref, br, slice(col, col + QB))
                ps.append(p)
                alphas.append(a)
            gc = slice(g * GQ, (g + 1) * GQ)
            vT = _with_ones(vT_ref[0, kb, g * HEAD_DIM:(g + 1) * HEAD_DIM, :])
            acc_ref[br, :, gc] = acc_ref[br, :, gc] * jnp.concatenate(alphas, axis=1) + _dot(vT, jnp.concatenate(ps, axis=1))

    def slc_masks(kb, causal):
        per_chunk = NSA_KC // L_SEL
        masks = []
        for g in range(N_KV_GROUPS):
            picked = sel_ref[g, per_chunk * kb + per_chunk - 1][0:1, :]
            for i in reversed(range(per_chunk - 1)):
                picked = jnp.where(key_off < (i + 1) * L_SEL, sel_ref[g, per_chunk * kb + i][0:1, :], picked)
            mk = picked > 0.5
            masks.append(mk & (kb * NSA_KC + key_off <= t_row) if causal else mk)
        return masks

    def win_masks(kb, last):
        key_pos = kb * NSA_KC + key_off
        if kb == last:
            wm = key_pos <= t_row
        elif kb == last - WINDOW // NSA_KC:
            wm = t_row - key_pos < WINDOW
        else:
            wm = None
        return [wm, wm]

    last_dyn = lax.shift_right_logical(c * QB, NSA_KC.bit_length() - 1)
    for last in range(kslc_ref.shape[1] // NSA_KC):
        @pl.when(last_dyn == last)
        def _(last=last):
            first_win = max(last - WINDOW // NSA_KC, 0)
            plan = [(0, kslc_ref, vslcT_ref, kb, False) for kb in range(first_win)]
            for kb in range(first_win, last + 1):
                plan += [(0, kslc_ref, vslcT_ref, kb, kb == last), (1, kwin_ref, vwinT_ref, kb, None)]
            s_next = scores(plan[0][1], plan[0][3])
            for i, (br, _, vT_ref, kb, causal) in enumerate(plan):
                s_cur = s_next
                if i + 1 < len(plan):
                    s_next = scores(plan[i + 1][1], plan[i + 1][3])
                consume(s_cur, br, vT_ref, kb, slc_masks(kb, causal) if br == 0 else win_masks(kb, last))

    o_slc = _normalise(acc_ref[0])
    o_win = _normalise(acc_ref[1])

    for g in range(N_KV_GROUPS):
        outs = []
        for hh in range(HEADS_PER_GROUP):
            r = 3 * (HEADS_PER_GROUP * g + hh)
            cols = slice(g * GQ + hh * QB, g * GQ + (hh + 1) * QB)
            outs.append(gates[r:r + 1] * o_cmp[g][:, hh * QB:(hh + 1) * QB] + gates[r + 1:r + 2] * o_slc[:, cols]
                        + gates[r + 2:r + 3] * o_win[:, cols])
        o_ref[0, :, g * GW:(g + 1) * GW] = jnp.concatenate(outs, axis=0).T.astype(BF16)


def _nsa_attn(qT, gT, kc, vcT, kslc, kwin, vslcT, vwinT, ovT):
    B, _, S = qT.shape
    n_sel = S // L_SEL
    W12 = N_MIX_HEADS * QB
    per_b3 = lambda a: pl.BlockSpec((1,) + a.shape[1:], lambda b, c: (b, 0, 0))
    per_b4 = lambda a: pl.BlockSpec((1,) + a.shape[1:], lambda b, c: (b, 0, 0, 0))
    return pl.pallas_call(
        _nsa_attn_kernel,
        grid=(B, S // QB),
        in_specs=[pl.BlockSpec((1, MIX_W, QB), lambda b, c: (b, 0, c)),
                  pl.BlockSpec((1, GATE_ROWS, QB), lambda b, c: (b, 0, c)),
                  per_b3(kc), per_b3(vcT), per_b3(kslc), per_b3(kwin), per_b4(vslcT), per_b4(vwinT),
                  _const_spec(ovT.shape)],
        out_specs=pl.BlockSpec((1, QB, MIX_W), lambda b, c: (b, c, 0)),
        out_shape=jax.ShapeDtypeStruct((B, S, MIX_W), BF16),
        scratch_shapes=[pltpu.VMEM((N_KV_GROUPS, n_sel, 8, QB), F32), pltpu.VMEM((2, 1, W12), F32),
                        pltpu.VMEM((2, ACC_ROWS, W12), F32)],
        compiler_params=_params(("arbitrary", "arbitrary")),
        name="nsa_attn",
    )(qT, gT, kc, vcT, kslc, kwin, vslcT, vwinT, ovT)


def _pair_q(q2, width):
    z = jnp.zeros((HEAD_DIM, width), q2.dtype)
    return jnp.concatenate([jnp.concatenate([q2[0:HEAD_DIM], z], axis=1),
                            jnp.concatenate([z, q2[HEAD_DIM:2 * HEAD_DIM]], axis=1)], axis=0)


def _fox_attn_kernel(qT_ref, k_ref, vT_ref, dcum_ref, dcumT_ref, o_ref, kbias_ref, m_ref, acc_ref):
    p = pl.program_id(1)
    c = pl.program_id(2)
    nq = FOX_TQ // QB
    per_q = FOX_TQ // FOX_KC

    @pl.when(c == 0)
    def _():
        parts = _split3(dcum_ref[0] * LOG2E)
        src = lax.broadcasted_iota(jnp.int32, (LANES, LANES), 0)
        dst = lax.broadcasted_iota(jnp.int32, (LANES, LANES), 1)
        lane = lax.broadcasted_iota(jnp.int32, (1, LANES), 1)
        acc = ((lane >= 3) & (lane < 6)).astype(F32)
        for r in range(3):
            place = ((src == 2 * p) & (dst == r)) | ((src == 2 * p + 1) & (dst == 6 + r))
            acc = acc + _dot(parts[r], place.astype(BF16))
        kbias_ref[...] = acc.astype(BF16)

    rowi = lax.broadcasted_iota(jnp.int32, (16, 2 * FOX_TQ), 0)
    head1 = lax.broadcasted_iota(jnp.int32, (16, 2 * FOX_TQ), 1) >= FOX_TQ
    bias = jnp.where(((rowi < 3) & ~head1) | ((rowi >= 6) & (rowi < 9) & head1), -1.0, 0.0)
    dq = [_split3(dcumT_ref[0, pl.ds(2 * p + hh, 1), :] * LOG2E) for hh in range(2)]
    for r in range(3):
        bias = jnp.where(rowi == 3 + r, jnp.concatenate([dq[0][r], dq[1][r]], axis=1).astype(F32), bias)
    qaug = jnp.concatenate([_pair_q(qT_ref[0], FOX_TQ), bias.astype(BF16),
                            jnp.zeros((LANES - 16, 2 * FOX_TQ), BF16)], axis=0)

    _reset_state(m_ref, acc_ref)
    lane_t =lax.broadcasted_iota(jnp.int32, (1, QB), 1)

    def scores(kb, n_kc, qs_lo, masked):
        start, size = kb * FOX_KC, n_kc * FOX_KC
        kk = jnp.concatenate([k_ref[0, pl.ds(start, size), :], kbias_ref[pl.ds(start, size), :]], axis=1)
        live = FOX_TQ - qs_lo * QB
        if qs_lo == 0:
            s2 = _dot(kk, qaug)
            return [s2[:, 0:FOX_TQ], s2[:, FOX_TQ:2 * FOX_TQ]]
        return [_dot(kk, qaug[:, (hh + 1) * FOX_TQ - live:(hh + 1) * FOX_TQ]) for hh in range(2)]

    def consume(s, kb, n_kc, qs_lo, masked):
        live = FOX_TQ - qs_lo * QB
        if masked:
            key_pos = kb * FOX_KC + lax.broadcasted_iota(jnp.int32, (n_kc * FOX_KC, QB), 0)
        for hh in range(2):
            cols = slice((hh + 1) * FOX_TQ - live, (hh + 1) * FOX_TQ)
            ps, alphas = [], []
            for qs in range(qs_lo, nq):
                tile = s[hh][:, (qs - qs_lo) * QB:(qs - qs_lo + 1) * QB]
                if masked:
                    tile = jnp.where(key_pos <= c * FOX_TQ + qs * QB + lane_t, tile, MASKED)
                col = hh * FOX_TQ + qs * QB
                pr, a = _online_tile(tile, m_ref, 0, slice(col, col + QB))
                ps.append(pr)
                alphas.append(a)
            vT = jnp.concatenate([vT_ref[0, kb + i, hh * HEAD_DIM:(hh + 1) * HEAD_DIM, :] for i in range(n_kc)], axis=1)
            acc_ref[0, :, cols] = (acc_ref[0, :, cols] * jnp.concatenate(alphas, axis=1)
                                   + _dot(_with_ones(vT), jnp.concatenate(ps, axis=1)))

    for cc in range(dcum_ref.shape[1] // FOX_TQ):
        @pl.when(c == cc)
        def _(cc=cc):
            plan = [(i * per_q, per_q, 0, False) for i in range(cc)]
            plan += [(cc * per_q + i, 1, i * FOX_KC // QB, True) for i in range(per_q)]
            s_next = scores(*plan[0])
            for i, step in enumerate(plan):
                s_cur = s_next
                if i + 1 < len(plan):
                    s_next = scores(*plan[i + 1])
                consume(s_cur, *step)
    o = _normalise(acc_ref[0])
    oT =jnp.concatenate([o[:, 0:FOX_TQ], o[:, FOX_TQ:2 * FOX_TQ]], axis=0)
    o_ref[0] = oT.T.astype(BF16)


def _fox_attn(qT, k, vT, dcum, dcumT):
    B, _, S = qT.shape
    n_pairs = N_MIX_HEADS // 2
    return pl.pallas_call(
        _fox_attn_kernel,
        grid=(B, n_pairs, S // FOX_TQ),
        in_specs=[pl.BlockSpec((1, LANES, FOX_TQ), lambda b, p, c: (b, p, c)),
                  pl.BlockSpec((1, S, LANES), lambda b, p, c: (b, 0, p)),
                  pl.BlockSpec((1, S // FOX_KC, LANES, FOX_KC), lambda b, p, c: (b, 0, p, 0)),
                  pl.BlockSpec((1, S, LANES), lambda b, p, c: (b, 0, 0)),
                  pl.BlockSpec((1, 16, FOX_TQ), lambda b, p, c: (b, 0, c))],
        out_specs=pl.BlockSpec((1, FOX_TQ, LANES), lambda b, p, c: (b, c, p)),
        out_shape=jax.ShapeDtypeStruct((B, S, MIX_W), BF16),
        scratch_shapes=[pltpu.VMEM((S, LANES), BF16), pltpu.VMEM((1, 1, 2 * FOX_TQ), F32),
                        pltpu.VMEM((1, ACC_ROWS, 2 * FOX_TQ), F32)],
        compiler_params=_params(("arbitrary", "arbitrary", "arbitrary")),
        name="fox_attn",
    )(qT, k, vT, dcum, dcumT)


def _mem_attention(qT_ref, k_ref, vT_ref):
    tq = qT_ref.shape[2]
    outs = []
    for pr in range(N_MEM_HEADS // 2):
        qbd = _pair_q(qT_ref[0, pr * LANES:(pr + 1) * LANES, :], tq)
        s = _dot(k_ref[0, :, pr * LANES:(pr + 1) * LANES], qbd)
        for hh in range(2):
            ps = []
            for qs in range(tq // QB):
                col = hh * tq + qs * QB
                sh = s[:, col:col + QB]
                e = jnp.exp2(sh - jnp.max(sh, axis=0, keepdims=True))
                ps.append((e / jnp.sum(e, axis=0, keepdims=True)).astype(BF16))
            h = 2 * pr + hh
            outs.append(_dot(vT_ref[0, h * HEAD_DIM:(h + 1) * HEAD_DIM, :], jnp.concatenate(ps, axis=1)))
    return jnp.concatenate(outs, axis=0).T.astype(BF16)


def _ffn_kernel(x_ref, omix_ref, qmT_ref, kmem_ref, vmemT_ref, wo1_ref, wo2_ref, g_ref, wa_ref, wb_ref, cwa_ref,
                cwb_ref, cba_ref, cbb_ref, wd_ref, gfin_ref, o_ref, ca_ref, cb_ref, gate_ref, *, final):
    j = pl.program_id(1)
    omem = _mem_attention(qmT_ref, kmem_ref, vmemT_ref)
    x1 = x_ref[0] + _dot(omix_ref[0], wo1_ref[...]) + _dot(omem, wo2_ref[...])
    h = _rms(x1, g_ref[...]).astype(BF16)

    @pl.when(j == 0)
    def _():
        ca_ref[...] = jnp.zeros(ca_ref.shape, F32)
        cb_ref[...] = jnp.zeros(cb_ref.shape, F32)

    def conv(u, prev8, w, bias):
        ext = jnp.concatenate([prev8, u], axis=0)
        u1 = pltpu.roll(ext, 1, 0)[8:]
        u2 = pltpu.roll(ext, 2, 0)[8:]
        return w[0:1] * u2 + w[1:2] * u1 + w[2:3] * u + bias

    for ci in range(D_FF // FF_CHUNK):
        sl = slice(ci * FF_CHUNK, (ci + 1) * FF_CHUNK)
        a = _dot(h, wa_ref[:, sl])
        b = _dot(h, wb_ref[:, sl])
        pa = ca_ref[:, sl]
        pb = cb_ref[:, sl]
        ca_ref[:, sl] = a[FF_TM - 8:FF_TM]
        cb_ref[:, sl] = b[FF_TM - 8:FF_TM]
        ac = conv(a, pa, cwa_ref[:, sl], cba_ref[:, sl])
        bc = conv(b, pb, cwb_ref[:, sl], cbb_ref[:, sl])
        gate_ref[:, sl] = (ac * jax.nn.sigmoid(ac) * bc).astype(BF16)
    y = x1 + _dot(gate_ref[...], wd_ref[...])
    o_ref[0] = _rms(y, gfin_ref[...]) if final else y


def _ffn(x, omix, qmT, kmem, vmemT, params, final):
    B, S, _ = x.shape
    M = kmem.shape[1]
    tok = lambda w: pl.BlockSpec((1, FF_TM, w), lambda b, j: (b, j, 0))
    arrs, specs = _split_params(params)
    return pl.pallas_call(
        functools.partial(_ffn_kernel, final=final),
        grid=(B, S // FF_TM),
        in_specs=[tok(D_MODEL), tok(MIX_W), pl.BlockSpec((1, MEM_W, FF_TM), lambda b, j: (b, 0, j)),
                  pl.BlockSpec((1, M, MEM_W), lambda b, j: (b, 0, 0)),
                  pl.BlockSpec((1, MEM_W, M), lambda b, j: (b, 0, 0))] + specs,
        out_specs=tok(D_MODEL),
        out_shape=jax.ShapeDtypeStruct((B, S, D_MODEL), F32),
        scratch_shapes=[pltpu.VMEM((8, D_FF), F32), pltpu.VMEM((8, D_FF), F32), pltpu.VMEM((FF_TM, D_FF), BF16)],
        compiler_params=_params(("arbitrary", "arbitrary")),
        name="memattn_outproj_convffn",
    )(x, omix, qmT, kmem, vmemT, *arrs)


def _rope_tables(pos):
    half = HEAD_DIM // 2
    inv = ROPE_THETA ** (-jnp.arange(half, dtype=F32) / half)
    ang = pos.astype(F32)[:, None] * inv[None, :]
    cos, sin = jnp.cos(ang), jnp.sin(ang)
    reps = LANES // half
    signs = jnp.tile(jnp.concatenate([-jnp.ones((half,), F32), jnp.ones((half,), F32)]), LANES // HEAD_DIM)
    return cos.T, sin.T, jnp.tile(cos, (1, reps)), jnp.tile(sin, (1, reps)) * signs[None, :]


def _overlap_T(S, n_cmp_pad):
    n_cmp = (S - L_CMP) // CMP_STRIDE + 1
    n_sel = S // L_SEL
    cs = np.arange(n_cmp_pad) * CMP_STRIDE
    ss = np.arange(n_sel) * L_SEL
    ov = (cs[None, :] < ss[:, None] + L_SEL) & (cs[None, :] + L_CMP > ss[:, None]) & (np.arange(n_cmp_pad)[None, :] < n_cmp)
    return jnp.asarray(ov, dtype=BF16)


def _compress_params(pos, w1, b1, w2, b2):
    half = L_CMP // 2
    w1r = w1.reshape(2, L_CMP, HEAD_DIM, CMP_HIDDEN)

    def spread(w):
        z = jnp.zeros((2, N_KV_GROUPS, half, N_KV_GROUPS, HEAD_DIM, CMP_HIDDEN), F32)
        for g in range(N_KV_GROUPS):
            z = z.at[:, g, :, g].set(w)
        return z.reshape(2, N_KV_GROUPS, half * LANES, CMP_HIDDEN).astype(BF16)

    def spread_pos(p):
        return jnp.tile(p[:, :, None, :], (1, 1, N_KV_GROUPS, 1)).reshape(2, 1, half * LANES)

    w2p = jnp.zeros((2, N_KV_GROUPS, CMP_HIDDEN, N_KV_GROUPS, HEAD_DIM), F32)
    for g in range(N_KV_GROUPS):
        w2p = w2p.at[:, g, :, g].set(w2)
    return (spread(w1r[:, :half]), spread(w1r[:, half:]), spread_pos(pos[:, :half]), spread_pos(pos[:, half:]),
            b1[:, None, :], w2p.reshape(2, N_KV_GROUPS, CMP_HIDDEN, LANES).astype(BF16),
            jnp.tile(b2, (1, N_KV_GROUPS))[:, None, :])


def _pad_cols(w, width):
    return jnp.pad(w, ((0, 0), (0, width - w.shape[1])))


def kernel(x, mem, attn_norm, ffn_norm, mem_norm, w_mem_kv, w_o, w_up, conv_w, conv_b, w_down, a_w_in, a_gate_b,
           a_cmp_pos, a_cmp_w1, a_cmp_b1, a_cmp_w2, a_cmp_b2, b_w_in, kv_norm, w_kv_shared, b_fgate, final_norm):
    B, S, _ = x.shape
    assert S % FOX_TQ == 0 and S % (CMP_STRIDE * 8) == 0
    n_chunks = S // CMP_STRIDE
    cosT, sinT, rc, rs = _rope_tables(jnp.arange(S))
    _, _, rc_cmp, rs_cmp = _rope_tables(jnp.arange(n_chunks) * CMP_STRIDE + L_CMP - 1)
    ovT = _overlap_T(S, n_chunks)

    rows3 = lambda v: v.reshape(v.shape[0], 1, -1).astype(F32)
    layer = lambda a, l, block, *idx: _w(a, (None,) + block, l, *idx)
    gain = lambda a, l: layer(a, l, (1, D_MODEL), 0, 0)
    attn_g, ffn_g, mem_g = rows3(attn_norm), rows3(ffn_norm), rows3(mem_norm)
    w_mem_b, w_o_b, w_up_b, w_down_b = (a.astype(BF16) for a in (w_mem_kv, w_o, w_up, w_down))
    a_w_b, b_w_b, w_sh_b = a_w_in.astype(BF16), b_w_in.astype(BF16), w_kv_shared.astype(BF16)
    conv_b3 = rows3(conv_b)
    a_wg = jnp.pad(a_w_b[:, :, 2 * MIX_W:2 * MIX_W + GATE_W], ((0, 0), (0, 0), (0, LANES - GATE_W)))
    a_gb = jnp.pad(rows3(a_gate_b), ((0, 0), (0, 0), (0, LANES - GATE_W)))
    a_wqm = a_w_b[:, :, 2 * MIX_W + GATE_W:]
    cmp_params = jax.vmap(_compress_params)(a_cmp_pos, a_cmp_w1, a_cmp_b1, a_cmp_w2, a_cmp_b2)
    w_f = _pad_cols(w_sh_b[:, 2 * MIX_W:], LANES)
    b_f = _pad_cols(b_fgate[None, :].astype(F32), LANES)
    fin_g = final_norm.reshape(1, -1).astype(F32)

    k_sh = vT_sh = dcum = dcumT = None
    for l in range(DEPTH):
        kmem, vmemT = _mem_kv(mem, [gain(mem_g, l), layer(w_mem_b, l, (D_MODEL, 2 * MEM_W), 0, 0)])
        if l < N_A:
            qT, kcmp, vcmp, kslc, kwin, vslcT, vwinT, gT, qmT = _proj_nsa(
                x, [gain(attn_g, l), layer(a_w_b, l, (D_MODEL, MIX_W), 0, 0), layer(a_w_b, l, (D_MODEL, MIX_W), 0, 1),
                    layer(a_wg, l, (D_MODEL, LANES), 0, 0), layer(a_gb, l, (1, LANES), 0, 0),
                    layer(a_wqm, l, (D_MODEL, MEM_W), 0, 0)],
                cosT, sinT, rc, rs)
            kc, vcT = _compress(kcmp, vcmp, [layer(a, l, a.shape[1:], *(0,) * (a.ndim - 1)) for a in cmp_params],
                                rc_cmp, rs_cmp)
            omix = _nsa_attn(qT, gT, kc, vcT, kslc, kwin, vslcT, vwinT, ovT)
        else:
            if l == N_A:
                k_sh, vT_sh, dcum, dcumT = _kv_shared(
                    x, [_w(kv_norm.reshape(1, -1).astype(F32)), _w(w_sh_b, (D_MODEL, MIX_W), 0, 0),
                        _w(w_sh_b, (D_MODEL, MIX_W), 0, 1), _w(w_f), _w(b_f)])
            qT, qmT = _proj_fox(x, [gain(attn_g, l), layer(b_w_b, l - N_A, (D_MODEL, MIX_W), 0, 0),
                                    layer(b_w_b, l - N_A, (D_MODEL, MEM_W), 0, MIX_W // MEM_W)])
            omix = _fox_attn(qT, k_sh, vT_sh, dcum, dcumT)
        x = _ffn(x, omix, qmT, kmem, vmemT,
                 [layer(w_o_b, l, (MIX_W, D_MODEL), 0, 0), layer(w_o_b, l, (MEM_W, D_MODEL), MIX_W // MEM_W, 0),
                  gain(ffn_g, l), layer(w_up_b, l, (D_MODEL, D_FF), 0, 0), layer(w_up_b, l, (D_MODEL, D_FF), 0, 1),
                  layer(conv_w, l, (CONV_WIDTH, D_FF), 0, 0), layer(conv_w, l, (CONV_WIDTH, D_FF), 0, 1),
                  layer(conv_b3, l, (1, D_FF), 0, 0), layer(conv_b3, l, (1, D_FF), 0, 1),
                  layer(w_down_b, l, (D_FF, D_MODEL), 0, 0), _w(fin_g)],
                 final=(l == DEPTH - 1))
    return x
```

```python
import functools

import numpy as np
import jax
import jax.numpy as jnp
from jax import lax
from jax.experimental import pallas as pl
from jax.experimental.pallas import tpu as pltpu

D_MODEL = 1024
HEAD_DIM = 64
N_MIX_HEADS = 12
N_KV_GROUPS = 2
HEADS_PER_GROUP = N_MIX_HEADS // N_KV_GROUPS
N_MEM_HEADS = 4
L_CMP = 32
CMP_STRIDE = 16
CMP_HIDDEN = 256
L_SEL = 64
TOP_N = 16
WINDOW = 512
D_FF = 2816
CONV_WIDTH = 3
ROPE_THETA = 10000.0
EPS = 1e-6
NEG = -1e30
FORCE_BONUS = 1e4
MIX_W = N_MIX_HEADS * HEAD_DIM
MEM_W = N_MEM_HEADS * HEAD_DIM
GATE_W = 3 * N_MIX_HEADS
N_A = 2
DEPTH = 4
LOG2E = 1.4426950408889634
Q_MUL = HEAD_DIM ** -0.5 * LOG2E
ACC_ROWS = HEAD_DIM + 16

LANES = 128
TM = 512
QB = 128
NSA_KC = 256
MASKED = 2.0 * NEG
FOX_TQ = 512
FOX_KC = 256
FF_TM = 512
FF_CHUNK = 256
GATE_ROWS = 40
VMEM_LIMIT = 56 * 1024 * 1024

F32 = jnp.float32
BF16 = jnp.bfloat16


def _params(sem):
    return pltpu.CompilerParams(dimension_semantics=sem, vmem_limit_bytes=VMEM_LIMIT)


def _const_spec(shape):
    n = len(shape)
    return pl.BlockSpec(shape, lambda *_: (0,) * n, pipeline_mode=pl.Buffered(1))


def _w(arr, block=None, *idx):
    if block is None:
        block, idx = arr.shape, (0,) * arr.ndim
    return arr, pl.BlockSpec(block, lambda *_: idx, pipeline_mode=pl.Buffered(1))


def _split_params(params):
    return [a for a, _ in params], [s for _, s in params]


def _rms(x, g):
    return x * lax.rsqrt(jnp.mean(x * x, axis=-1, keepdims=True) + EPS) * g


def _dot(a, b):
    return jnp.dot(a, b, preferred_element_type=F32)


def _rope_lanes(x, cos, sin_signed):
    lane = lax.broadcasted_iota(jnp.int32, x.shape, 1)
    first_half = (lane & (HEAD_DIM - 1)) < (HEAD_DIM // 2)
    partner = jnp.where(first_half, pltpu.roll(x, LANES - HEAD_DIM // 2, 1), pltpu.roll(x, HEAD_DIM // 2, 1))
    return x * cos + partner * sin_signed


def _split3(x):
    a = x.astype(BF16)
    r = x - a.astype(F32)
    b = r.astype(BF16)
    c = (r - b.astype(F32)).astype(BF16)
    return a, b, c


def _proj_nsa_kernel(x_ref, g_ref, wq_ref, wkv_ref, wg_ref, gb_ref, wqm_ref, cosT_ref, sinT_ref, rc_ref, rs_ref,
                     qT_ref, kcmp_ref, vcmp_ref, kslc_ref, kwin_ref, vslcT_ref, vwinT_ref, gT_ref, qmT_ref, cmp_ref):
    h = _rms(x_ref[0], g_ref[...]).astype(BF16)
    qT = (_dot(h, wq_ref[...]) * Q_MUL).T
    cos = cosT_ref[...]
    sin = sinT_ref[...]
    half = HEAD_DIM // 2
    for hh in range(N_MIX_HEADS):
        x1 = qT[hh * HEAD_DIM:hh * HEAD_DIM + half]
        x2 = qT[hh * HEAD_DIM + half:(hh + 1) * HEAD_DIM]
        qT_ref[0, hh * HEAD_DIM:hh * HEAD_DIM + half, :] = (x1 * cos - x2 * sin).astype(BF16)
        qT_ref[0, hh * HEAD_DIM + half:(hh + 1) * HEAD_DIM, :] = (x2 * cos + x1 * sin).astype(BF16)
    kv = _dot(h, wkv_ref[...])
    rc = rc_ref[...]
    rs = rs_ref[...]
    cmp_ref[0] = kv[:, 0:128]
    cmp_ref[1] = kv[:, 128:256]
    for r in range(CMP_STRIDE):
        kcmp_ref[0, :, r * LANES:(r + 1) * LANES] = cmp_ref[0, pl.ds(r, TM // CMP_STRIDE, stride=CMP_STRIDE), :]
        vcmp_ref[0, :, r * LANES:(r + 1) * LANES] = cmp_ref[1, pl.ds(r, TM // CMP_STRIDE, stride=CMP_STRIDE), :]
    kslc_ref[0] = _rope_lanes(kv[:, 256:384], rc, rs).astype(BF16)
    kwin_ref[0] = _rope_lanes(kv[:, 512:640], rc, rs).astype(BF16)
    vsT = kv[:, 384:512].T.astype(BF16)
    vwT = kv[:, 640:768].T.astype(BF16)
    for j in range(TM // NSA_KC):
        vslcT_ref[0, j] = vsT[:, j * NSA_KC:(j + 1) * NSA_KC]
        vwinT_ref[0, j] = vwT[:, j * NSA_KC:(j + 1) * NSA_KC]
    gates = jax.nn.sigmoid(_dot(h, wg_ref[...]) + gb_ref[...])
    gT_ref[0] = gates.T[0:GATE_ROWS]
    qmT_ref[0] = (_dot(h, wqm_ref[...]) * Q_MUL).T.astype(BF16)


def _proj_nsa(x, params, cosT, sinT, rc, rs):
    B, S, _ = x.shape
    nt = S // TM
    arrs, specs = _split_params(params)
    tok = lambda w: pl.BlockSpec((1, TM, w), lambda b, j: (b, j, 0))
    featT = lambda r: pl.BlockSpec((1, r, TM), lambda b, j: (b, 0, j))
    vT = pl.BlockSpec((1, TM // NSA_KC, LANES, NSA_KC), lambda b, j: (b, j, 0, 0))
    chunked = pl.BlockSpec((1, TM // CMP_STRIDE, CMP_STRIDE * LANES), lambda b, j: (b, j, 0))
    return pl.pallas_call(
        _proj_nsa_kernel,
        grid=(B, nt),
        in_specs=[tok(D_MODEL)] + specs + [
                  pl.BlockSpec((HEAD_DIM // 2, TM), lambda b, j: (0, j)),
                  pl.BlockSpec((HEAD_DIM // 2, TM), lambda b, j: (0, j)),
                  pl.BlockSpec((TM, LANES), lambda b, j: (j, 0)),
                  pl.BlockSpec((TM, LANES), lambda b, j: (j, 0))],
        out_specs=[featT(MIX_W), chunked, chunked, tok(LANES), tok(LANES), vT, vT, featT(GATE_ROWS),
                   featT(MEM_W)],
        out_shape=[jax.ShapeDtypeStruct((B, MIX_W, S), BF16),
                   jax.ShapeDtypeStruct((B, S // CMP_STRIDE, CMP_STRIDE * LANES), F32),
                   jax.ShapeDtypeStruct((B, S // CMP_STRIDE, CMP_STRIDE * LANES), F32),
                   jax.ShapeDtypeStruct((B, S, LANES), BF16), jax.ShapeDtypeStruct((B, S, LANES), BF16),
                   jax.ShapeDtypeStruct((B, S // NSA_KC, LANES, NSA_KC), BF16),
                   jax.ShapeDtypeStruct((B, S // NSA_KC, LANES, NSA_KC), BF16),
                   jax.ShapeDtypeStruct((B, GATE_ROWS, S), F32),
                   jax.ShapeDtypeStruct((B, MEM_W, S), BF16)],
        scratch_shapes=[pltpu.VMEM((2, TM, LANES), F32)],
        compiler_params=_params(("arbitrary", "arbitrary")),
        name="proj_nsa",
    )(x, *arrs, cosT, sinT, rc, rs)


def _proj_fox_kernel(x_ref, g_ref, wq_ref, wqm_ref, qT_ref, qmT_ref):
    h = _rms(x_ref[0], g_ref[...]).astype(BF16)
    qT_ref[0] = (_dot(h, wq_ref[...]) * Q_MUL).T.astype(BF16)
    qmT_ref[0] = (_dot(h, wqm_ref[...]) * Q_MUL).T.astype(BF16)


def _proj_fox(x, params):
    B, S, _ = x.shape
    arrs, specs = _split_params(params)
    featT = lambda r: pl.BlockSpec((1, r, TM), lambda b, j: (b, 0, j))
    return pl.pallas_call(
        _proj_fox_kernel,
        grid=(B, S // TM),
        in_specs=[pl.BlockSpec((1, TM, D_MODEL), lambda b, j: (b, j, 0))] + specs,
        out_specs=[featT(MIX_W), featT(MEM_W)],
        out_shape=[jax.ShapeDtypeStruct((B, MIX_W, S), BF16), jax.ShapeDtypeStruct((B, MEM_W, S), BF16)],
        compiler_params=_params(("arbitrary", "arbitrary")),
        name="proj_fox",
    )(x, *arrs)


def _kv_shared_kernel(x_ref, g_ref, wk_ref, wv_ref, wf_ref, bf_ref, k_ref, vT_ref, dcum_ref, dcumT_ref, carry_ref):
    j = pl.program_id(1)
    h = _rms(x_ref[0], g_ref[...]).astype(BF16)
    k_ref[0] = _dot(h, wk_ref[...]).astype(BF16)
    vT = _dot(h, wv_ref[...]).T.astype(BF16)
    for i in range(TM // FOX_KC):
        vT_ref[0, i] = vT[:, i * FOX_KC:(i + 1) * FOX_KC]
    z = _dot(h, wf_ref[...]) + bf_ref[...]
    logf = jnp.minimum(z, 0.0) - jnp.log1p(jnp.exp(-jnp.abs(z)))
    row = lax.broadcasted_iota(jnp.int32, (TM, TM), 0)
    col = lax.broadcasted_iota(jnp.int32, (TM, TM), 1)
    tri = (col <= row).astype(BF16)
    a, b, c = _split3(logf)
    @pl.when(j == 0)
    def _():
        carry_ref[...] = jnp.zeros(carry_ref.shape, F32)

    cs = _dot(tri, a) + _dot(tri, b) + _dot(tri, c) + carry_ref[7:8, :]
    carry_ref[...] = cs[TM - 8:TM]
    dcum_ref[0] = cs
    dcumT_ref[0] = cs.T[0:16]


def _kv_shared(x, params):
    B, S, _ = x.shape
    arrs, specs = _split_params(params)
    return pl.pallas_call(
        _kv_shared_kernel,
        grid=(B, S // TM),
        in_specs=[pl.BlockSpec((1, TM, D_MODEL), lambda b, j: (b, j, 0))] + specs,
        out_specs=[pl.BlockSpec((1, TM, MIX_W), lambda b, j: (b, j, 0)),
                   pl.BlockSpec((1, TM // FOX_KC, MIX_W, FOX_KC), lambda b, j: (b, j, 0, 0)),
                   pl.BlockSpec((1, TM, LANES), lambda b, j: (b, j, 0)),
                   pl.BlockSpec((1, 16, TM), lambda b, j: (b, 0, j))],
        out_shape=[jax.ShapeDtypeStruct((B, S, MIX_W), BF16),
                   jax.ShapeDtypeStruct((B, S // FOX_KC, MIX_W, FOX_KC), BF16),
                   jax.ShapeDtypeStruct((B, S, LANES), F32),
                   jax.ShapeDtypeStruct((B, 16, S), F32)],
        scratch_shapes=[pltpu.VMEM((8, LANES), F32)],
        compiler_params=_params(("arbitrary", "arbitrary")),
        name="kv_shared",
    )(x, *arrs)


def _mem_kv_kernel(m_ref, g_ref, w_ref, k_ref, vT_ref):
    h = _rms(m_ref[0], g_ref[...]).astype(BF16)
    kv = _dot(h, w_ref[...])
    k_ref[0] = kv[:, 0:MEM_W].astype(BF16)
    vT_ref[0] = kv[:, MEM_W:2 * MEM_W].T.astype(BF16)


def _mem_kv(mem, params):
    B, M, _ = mem.shape
    (g, w), specs = _split_params(params)
    return pl.pallas_call(
        _mem_kv_kernel,
        grid=(B,),
        in_specs=[pl.BlockSpec((1, M, D_MODEL), lambda b: (b, 0, 0))] + specs,
        out_specs=[pl.BlockSpec((1, M, MEM_W), lambda b: (b, 0, 0)), pl.BlockSpec((1, MEM_W, M), lambda b: (b, 0, 0))],
        out_shape=[jax.ShapeDtypeStruct((B, M, MEM_W), BF16), jax.ShapeDtypeStruct((B, MEM_W, M), BF16)],
        compiler_params=_params(("arbitrary",)),
        name="mem_kv",
    )(mem, g, w)


def _compress_kernel(kx_ref, vx_ref, wtop_ref, wbot_ref, ptop_ref, pbot_ref, b1_ref, w2_ref, b2_ref, rc_ref, rs_ref,
                     kc_ref, vcT_ref):
    for kvi, x_ref in enumerate((kx_ref, vx_ref)):
        x = x_ref[0]
        xt = (x + ptop_ref[kvi]).astype(BF16)
        xb = (x + pbot_ref[kvi]).astype(BF16)
        out = b2_ref[kvi]
        for g in range(N_KV_GROUPS):
            top = _dot(xt, wtop_ref[kvi, g])
            bot = _dot(xb, wbot_ref[kvi, g])
            hid = top + pltpu.roll(bot, bot.shape[0] - 1, 0) + b1_ref[kvi]
            out = out + _dot(jax.nn.gelu(hid).astype(BF16), w2_ref[kvi, g])
        if kvi == 0:
            kc_ref[0] = _rope_lanes(out, rc_ref[...], rs_ref[...]).astype(BF16)
        else:
            vcT_ref[0] = out.T.astype(BF16)


def _compress(kx, vx, params, rc, rs):
    B, NC, W = kx.shape
    xs = pl.BlockSpec((1, NC, W), lambda b: (b, 0, 0))
    arrs, specs = _split_params(params)
    return pl.pallas_call(
        _compress_kernel,
        grid=(B,),
        in_specs=[xs, xs] + specs + [_const_spec(rc.shape), _const_spec(rs.shape)],
        out_specs=[pl.BlockSpec((1, NC, LANES), lambda b: (b, 0, 0)), pl.BlockSpec((1, LANES, NC), lambda b: (b, 0, 0))],
        out_shape=[jax.ShapeDtypeStruct((B, NC, LANES), BF16), jax.ShapeDtypeStruct((B, LANES, NC), BF16)],
        compiler_params=_params(("arbitrary",)),
        name="nsa_compress",
    )(kx, vx, *arrs, rc, rs)


def _online_tile(s, m_ref, lead, cols):
    m_old = m_ref[lead, :, cols]
    m_new = jnp.maximum(m_old, jnp.max(s, axis=0, keepdims=True))
    m_ref[lead, :, cols] = m_new
    return jnp.exp2(s - m_new).astype(BF16), jnp.exp2(m_old - m_new)


def _reset_state(m_ref, acc_ref):
    m_ref[...] = jnp.full(m_ref.shape, NEG, F32)
    acc_ref[...] = jnp.zeros(acc_ref.shape, F32)


def _with_ones(vT):
    return jnp.concatenate([vT, jnp.ones((ACC_ROWS - HEAD_DIM, vT.shape[1]), vT.dtype)], axis=0)


def _normalise(acc):
    return acc[0:HEAD_DIM] / jnp.maximum(acc[HEAD_DIM:HEAD_DIM + 1], 1e-30)


def _nsa_attn_kernel(qT_ref, gT_ref, kc_ref, vcT_ref, kslc_ref, kwin_ref, vslcT_ref, vwinT_ref, ovT_ref,
                     o_ref, sel_ref, m_ref, acc_ref):
    c = pl.program_id(1)
    n_sel = sel_ref.shape[1]
    n_cmp = kc_ref.shape[1]
    GQ = HEADS_PER_GROUP * QB
    GW = HEADS_PER_GROUP * HEAD_DIM
    t_row = c * QB + lax.broadcasted_iota(jnp.int32, (1, QB), 1)
    gates = gT_ref[0]

    q6 = [jnp.concatenate([qT_ref[0, (HEADS_PER_GROUP * g + hh) * HEAD_DIM:(HEADS_PER_GROUP * g + hh + 1) * HEAD_DIM, :]
                           for hh in range(HEADS_PER_GROUP)], axis=1) for g in range(N_KV_GROUPS)]
    z = jnp.zeros_like(q6[0])
    qbd = jnp.concatenate([jnp.concatenate([q6[0], z], axis=1), jnp.concatenate([z, q6[1]], axis=1)], axis=0)

    cmp_end = lax.broadcasted_iota(jnp.int32, (n_cmp, QB), 0) * CMP_STRIDE + (L_CMP - 1)
    valid_c = cmp_end <= t_row
    valid_cf = valid_c.astype(F32)
    s_cmp = _dot(kc_ref[0], qbd)
    j_idx = lax.broadcasted_iota(jnp.int32, (n_sel, QB), 0)
    blk_t = lax.shift_right_logical(t_row, 6)
    forced = ((j_idx == 0) | (j_idx == blk_t) | (j_idx == blk_t - 1)).astype(F32)
    o_cmp = []
    for g in range(N_KV_GROUPS):
        psum = jnp.zeros((n_cmp, QB), F32)
        ps = []
        for hh in range(HEADS_PER_GROUP):
            col = g * GQ + hh * QB
            sh = jnp.where(valid_c, s_cmp[:, col:col + QB], NEG)
            e = jnp.exp2(sh - jnp.max(sh, axis=0, keepdims=True)) * valid_cf
            p = e / jnp.maximum(jnp.sum(e, axis=0, keepdims=True), 1e-30)
            psum = psum + p
            ps.append(p.astype(BF16))
        o_cmp.append(_dot(vcT_ref[0, g * HEAD_DIM:(g + 1) * HEAD_DIM, :], jnp.concatenate(ps, axis=1)))
        p_hi = psum.astype(BF16)
        p_lo = (psum - p_hi.astype(F32)).astype(BF16)
        imp = _dot(ovT_ref[...], p_hi) + _dot(ovT_ref[...], p_lo)
        score = jnp.where(j_idx <= blk_t, imp + FORCE_BONUS * forced, NEG)
        cnt = jnp.zeros((n_sel, QB), F32)
        for i in range(n_sel):
            si = score[i:i + 1, :]
            cnt = cnt + ((si > score) | ((si == score) & (j_idx > i))).astype(F32)
        sel = (cnt < float(min(TOP_N, n_sel))).astype(F32)
        for i in range(n_sel):
            sel_ref[g, i] = jnp.broadcast_to(sel[i:i + 1, :], (8, QB))

    _reset_state(m_ref, acc_ref)
    def span(kb, half):
        return (kb * NSA_KC, NSA_KC) if half is None else (kb * NSA_KC + half * QB, QB)

    def scores(k_ref, kb, half):
        lo, n = span(kb, half)
        return _dot(k_ref[0, lo:lo + n, :], qbd)

    def consume(s, br, vT_ref, kb, half, masks):
        lo, n = span(kb, half)
        for g in range(N_KV_GROUPS):
            ps, alphas = [], []
            for hh in range(HEADS_PER_GROUP):
                col = g * GQ + hh * QB
                tile = s[:, col:col + QB]
                if masks[g] is not None:
                    tile = jnp.where(masks[g], tile, MASKED)
                p, a = _online_tile(tile, m_ref, br, slice(col, col + QB))
                ps.append(p)
                alphas.append(a)
            gc = slice(g * GQ, (g + 1) * GQ)
            vT = _with_ones(vT_ref[0, kb, g * HEAD_DIM:(g + 1) * HEAD_DIM, lo - kb * NSA_KC:lo - kb * NSA_KC + n])
            acc_ref[br, :, gc] = acc_ref[br, :, gc] * jnp.concatenate(alphas, axis=1) + _dot(vT, jnp.concatenate(ps, axis=1))

    def slc_masks(kb, half, causal):
        lo, n = span(kb, half)
        key_off = lax.broadcasted_iota(jnp.int32, (n, QB), 0)
        masks = []
        for g in range(N_KV_GROUPS):
            picked = sel_ref[g, (lo + n) // L_SEL - 1][0:1, :]
            for i in reversed(range(n // L_SEL - 1)):
                picked = jnp.where(key_off < (i + 1) * L_SEL, sel_ref[g, lo // L_SEL + i][0:1, :], picked)
            mk = picked > 0.5
            masks.append(mk & (lo + key_off <= t_row) if causal else mk)
        return masks

    def win_masks(kb, half, last):
        lo, n = span(kb, half)
        key_pos = lo + lax.broadcasted_iota(jnp.int32, (n, QB), 0)
        if kb == last:
            wm = key_pos <= t_row
        elif kb == last - WINDOW // NSA_KC:
            wm = t_row - key_pos < WINDOW
        else:
            wm = None
        return [wm, wm]

    for cc in range(kslc_ref.shape[1] // QB):
        @pl.when(c == cc)
        def _(cc=cc):
            last, odd = (cc * QB) // NSA_KC, (cc * QB) % NSA_KC != 0
            first_win = max(last - WINDOW // NSA_KC, 0)
            diag_half = None if odd else 0
            plan = [(0, kb, None) for kb in range(first_win)]
            for kb in range(first_win, last + 1):
                plan.append((0, kb, diag_half if kb == last else None))
                if kb == last:
                    plan.append((1, kb, diag_half))
                elif kb == last - WINDOW // NSA_KC and odd:
                    plan.append((1, kb, 1))
                else:
                    plan.append((1, kb, None))
            refs = ((kslc_ref, vslcT_ref), (kwin_ref, vwinT_ref))
            s_next = scores(refs[plan[0][0]][0], plan[0][1], plan[0][2])
            for i, (br, kb, half) in enumerate(plan):
                s_cur = s_next
                if i + 1 < len(plan):
                    nb, nkb, nhalf = plan[i + 1]
                    s_next = scores(refs[nb][0], nkb, nhalf)
                masks = slc_masks(kb, half, kb == last) if br == 0 else win_masks(kb, half, last)
                consume(s_cur, br, refs[br][1], kb, half, masks)

    o_slc = _normalise(acc_ref[0])
    o_win = _normalise(acc_ref[1])

    for g in range(N_KV_GROUPS):
        outs = []
        for hh in range(HEADS_PER_GROUP):
            r = 3 * (HEADS_PER_GROUP * g + hh)
            cols = slice(g * GQ + hh * QB, g * GQ + (hh + 1) * QB)
            outs.append(gates[r:r + 1] * o_cmp[g][:, hh * QB:(hh + 1) * QB] + gates[r + 1:r + 2] * o_slc[:, cols]
                        + gates[r + 2:r + 3] * o_win[:, cols])
        o_ref[0, :, g * GW:(g + 1) * GW] = jnp.concatenate(outs, axis=0).T.astype(BF16)


def _nsa_attn(qT, gT, kc, vcT, kslc, kwin, vslcT, vwinT, ovT):
    B, _, S = qT.shape
    n_sel = S // L_SEL
    W12 = N_MIX_HEADS * QB
    per_b3 = lambda a: pl.BlockSpec((1,) + a.shape[1:], lambda b, c: (b, 0, 0))
    per_b4 = lambda a: pl.BlockSpec((1,) + a.shape[1:], lambda b, c: (b, 0, 0, 0))
    return pl.pallas_call(
        _nsa_attn_kernel,
        grid=(B, S // QB),
        in_specs=[pl.BlockSpec((1, MIX_W, QB), lambda b, c: (b, 0, c)),
                  pl.BlockSpec((1, GATE_ROWS, QB), lambda b, c: (b, 0, c)),
                  per_b3(kc), per_b3(vcT), per_b3(kslc), per_b3(kwin), per_b4(vslcT), per_b4(vwinT),
                  _const_spec(ovT.shape)],
        out_specs=pl.BlockSpec((1, QB, MIX_W), lambda b, c: (b, c, 0)),
        out_shape=jax.ShapeDtypeStruct((B, S, MIX_W), BF16),
        scratch_shapes=[pltpu.VMEM((N_KV_GROUPS, n_sel, 8, QB), F32), pltpu.VMEM((2, 1, W12), F32),
                        pltpu.VMEM((2, ACC_ROWS, W12), F32)],
        compiler_params=_params(("arbitrary", "arbitrary")),
        name="nsa_attn",
    )(qT, gT, kc, vcT, kslc, kwin, vslcT, vwinT, ovT)


def _pair_q(q2, width):
    z = jnp.zeros((HEAD_DIM, width), q2.dtype)
    return jnp.concatenate([jnp.concatenate([q2[0:HEAD_DIM], z], axis=1),
                            jnp.concatenate([z, q2[HEAD_DIM:2 * HEAD_DIM]], axis=1)], axis=0)


def _fox_attn_kernel(qT_ref, k_ref, vT_ref, dcum_ref, dcumT_ref, o_ref, kbias_ref, m_ref, acc_ref):
    p = pl.program_id(1)
    c = pl.program_id(2)
    nq = FOX_TQ // QB
    per_q = FOX_TQ // FOX_KC

    @pl.when(c == 0)
    def _():
        parts = _split3(dcum_ref[0] * LOG2E)
        src = lax.broadcasted_iota(jnp.int32, (LANES, LANES), 0)
        dst = lax.broadcasted_iota(jnp.int32, (LANES, LANES), 1)
        lane = lax.broadcasted_iota(jnp.int32, (1, LANES), 1)
        acc = ((lane >= 3) & (lane < 6)).astype(F32)
        for r in range(3):
            place = ((src == 2 * p) & (dst == r)) | ((src == 2 * p + 1) & (dst == 6 + r))
            acc = acc + _dot(parts[r], place.astype(BF16))
        kbias_ref[...] = acc.astype(BF16)

    rowi = lax.broadcasted_iota(jnp.int32, (16, 2 * FOX_TQ), 0)
    head1 = lax.broadcasted_iota(jnp.int32, (16, 2 * FOX_TQ), 1) >= FOX_TQ
    bias = jnp.where(((rowi < 3) & ~head1) | ((rowi >= 6) & (rowi < 9) & head1), -1.0, 0.0)
    dq = [_split3(dcumT_ref[0, pl.ds(2 * p + hh, 1), :] * LOG2E) for hh in range(2)]
    for r in range(3):
        bias = jnp.where(rowi == 3 + r, jnp.concatenate([dq[0][r], dq[1][r]], axis=1).astype(F32), bias)
    qaug = jnp.concatenate([_pair_q(qT_ref[0], FOX_TQ), bias.astype(BF16),
                            jnp.zeros((LANES - 16, 2 * FOX_TQ), BF16)], axis=0)

    _reset_state(m_ref, acc_ref)
    lane_t =lax.broadcasted_iota(jnp.int32, (1, QB), 1)

    def scores(kb, n_kc, qs_lo, masked):
        start, size = kb * FOX_KC, n_kc * FOX_KC
        kk = jnp.concatenate([k_ref[0, pl.ds(start, size), :], kbias_ref[pl.ds(start, size), :]], axis=1)
        live = FOX_TQ - qs_lo * QB
        if qs_lo == 0:
            s2 = _dot(kk, qaug)
            return [s2[:, 0:FOX_TQ], s2[:, FOX_TQ:2 * FOX_TQ]]
        return [_dot(kk, qaug[:, (hh + 1) * FOX_TQ - live:(hh + 1) * FOX_TQ]) for hh in range(2)]

    def consume(s, kb, n_kc, qs_lo, masked):
        live = FOX_TQ - qs_lo * QB
        if masked:
            key_pos = kb * FOX_KC + lax.broadcasted_iota(jnp.int32, (n_kc * FOX_KC, QB), 0)
        for hh in range(2):
            cols = slice((hh + 1) * FOX_TQ - live, (hh + 1) * FOX_TQ)
            ps, alphas = [], []
            for qs in range(qs_lo, nq):
                tile = s[hh][:, (qs - qs_lo) * QB:(qs - qs_lo + 1) * QB]
                if masked:
                    tile = jnp.where(key_pos <= c * FOX_TQ + qs * QB + lane_t, tile, MASKED)
                col = hh * FOX_TQ + qs * QB
                pr, a = _online_tile(tile, m_ref, 0, slice(col, col + QB))
                ps.append(pr)
                alphas.append(a)
            vT = jnp.concatenate([vT_ref[0, kb + i, hh * HEAD_DIM:(hh + 1) * HEAD_DIM, :] for i in range(n_kc)], axis=1)
            acc_ref[0, :, cols] = (acc_ref[0, :, cols] * jnp.concatenate(alphas, axis=1)
                                   + _dot(_with_ones(vT), jnp.concatenate(ps, axis=1)))

    for cc in range(dcum_ref.shape[1] // FOX_TQ):
        @pl.when(c == cc)
        def _(cc=cc):
            plan = [(i * per_q, per_q, 0, False) for i in range(cc)]
            plan += [(cc * per_q + i, 1, i * FOX_KC // QB, True) for i in range(per_q)]
            s_next = scores(*plan[0])
            for i, step in enumerate(plan):
                s_cur = s_next
                if i + 1 < len(plan):
                    s_next = scores(*plan[i + 1])
                consume(s_cur, *step)
    o = _normalise(acc_ref[0])
    oT =jnp.concatenate([o[:, 0:FOX_TQ], o[:, FOX_TQ:2 * FOX_TQ]], axis=0)
    o_ref[0] = oT.T.astype(BF16)


def _fox_attn(qT, k, vT, dcum, dcumT):
    B, _, S = qT.shape
    n_pairs = N_MIX_HEADS // 2
    return pl.pallas_call(
        _fox_attn_kernel,
        grid=(B, n_pairs, S // FOX_TQ),
        in_specs=[pl.BlockSpec((1, LANES, FOX_TQ), lambda b, p, c: (b, p, c)),
                  pl.BlockSpec((1, S, LANES), lambda b, p, c: (b, 0, p)),
                  pl.BlockSpec((1, S // FOX_KC, LANES, FOX_KC), lambda b, p, c: (b, 0, p, 0)),
                  pl.BlockSpec((1, S, LANES), lambda b, p, c: (b, 0, 0)),
                  pl.BlockSpec((1, 16, FOX_TQ), lambda b, p, c: (b, 0, c))],
        out_specs=pl.BlockSpec((1, FOX_TQ, LANES), lambda b, p, c: (b, c, p)),
        out_shape=jax.ShapeDtypeStruct((B, S, MIX_W), BF16),
        scratch_shapes=[pltpu.VMEM((S, LANES), BF16), pltpu.VMEM((1, 1, 2 * FOX_TQ), F32),
                        pltpu.VMEM((1, ACC_ROWS, 2 * FOX_TQ), F32)],
        compiler_params=_params(("arbitrary", "arbitrary", "arbitrary")),
        name="fox_attn",
    )(qT, k, vT, dcum, dcumT)


def _mem_attention(qT_ref, k_ref, vT_ref):
    tq = qT_ref.shape[2]
    outs = []
    for pr in range(N_MEM_HEADS // 2):
        qbd = _pair_q(qT_ref[0, pr * LANES:(pr + 1) * LANES, :], tq)
        s = _dot(k_ref[0, :, pr * LANES:(pr + 1) * LANES], qbd)
        for hh in range(2):
            ps = []
            for qs in range(tq // QB):
                col = hh * tq + qs * QB
                sh = s[:, col:col + QB]
                e = jnp.exp2(sh - jnp.max(sh, axis=0, keepdims=True))
                ps.append((e / jnp.sum(e, axis=0, keepdims=True)).astype(BF16))
            h = 2 * pr + hh
            outs.append(_dot(vT_ref[0, h * HEAD_DIM:(h + 1) * HEAD_DIM, :], jnp.concatenate(ps, axis=1)))
    return jnp.concatenate(outs, axis=0).T.astype(BF16)


def _ffn_kernel(x_ref, omix_ref, qmT_ref, kmem_ref, vmemT_ref, wo1_ref, wo2_ref, g_ref, wa_ref, wb_ref, cwa_ref,
                cwb_ref, cba_ref, cbb_ref, wd_ref, gfin_ref, o_ref, ca_ref, cb_ref, gate_ref, *, final):
    j = pl.program_id(1)
    omem = _mem_attention(qmT_ref, kmem_ref, vmemT_ref)
    x1 = x_ref[0] + _dot(omix_ref[0], wo1_ref[...]) + _dot(omem, wo2_ref[...])
    h = _rms(x1, g_ref[...]).astype(BF16)

    @pl.when(j == 0)
    def _():
        ca_ref[...] = jnp.zeros(ca_ref.shape, F32)
        cb_ref[...] = jnp.zeros(cb_ref.shape, F32)

    def conv(u, prev8, w, bias):
        ext = jnp.concatenate([prev8, u], axis=0)
        u1 = pltpu.roll(ext, 1, 0)[8:]
        u2 = pltpu.roll(ext, 2, 0)[8:]
        return w[0:1] * u2 + w[1:2] * u1 + w[2:3] * u + bias

    for ci in range(D_FF // FF_CHUNK):
        sl = slice(ci * FF_CHUNK, (ci + 1) * FF_CHUNK)
        a = _dot(h, wa_ref[:, sl])
        b = _dot(h, wb_ref[:, sl])
        pa = ca_ref[:, sl]
        pb = cb_ref[:, sl]
        ca_ref[:, sl] = a[FF_TM - 8:FF_TM]
        cb_ref[:, sl] = b[FF_TM - 8:FF_TM]
        ac = conv(a, pa, cwa_ref[:, sl], cba_ref[:, sl])
        bc = conv(b, pb, cwb_ref[:, sl], cbb_ref[:, sl])
        gate_ref[:, sl] = (ac * jax.nn.sigmoid(ac) * bc).astype(BF16)
    y = x1 + _dot(gate_ref[...], wd_ref[...])
    o_ref[0] = _rms(y, gfin_ref[...]) if final else y


def _ffn(x, omix, qmT, kmem, vmemT, params, final):
    B, S, _ = x.shape
    M = kmem.shape[1]
    tok = lambda w: pl.BlockSpec((1, FF_TM, w), lambda b, j: (b, j, 0))
    arrs, specs = _split_params(params)
    return pl.pallas_call(
        functools.partial(_ffn_kernel, final=final),
        grid=(B, S // FF_TM),
        in_specs=[tok(D_MODEL), tok(MIX_W), pl.BlockSpec((1, MEM_W, FF_TM), lambda b, j: (b, 0, j)),
                  pl.BlockSpec((1, M, MEM_W), lambda b, j: (b, 0, 0)),
                  pl.BlockSpec((1, MEM_W, M), lambda b, j: (b, 0, 0))] + specs,
        out_specs=tok(D_MODEL),
        out_shape=jax.ShapeDtypeStruct((B, S, D_MODEL), F32),
        scratch_shapes=[pltpu.VMEM((8, D_FF), F32), pltpu.VMEM((8, D_FF), F32), pltpu.VMEM((FF_TM, D_FF), BF16)],
        compiler_params=_params(("arbitrary", "arbitrary")),
        name="memattn_outproj_convffn",
    )(x, omix, qmT, kmem, vmemT, *arrs)


def _rope_tables(pos):
    half = HEAD_DIM // 2
    inv = ROPE_THETA ** (-jnp.arange(half, dtype=F32) / half)
    ang = pos.astype(F32)[:, None] * inv[None, :]
    cos, sin = jnp.cos(ang), jnp.sin(ang)
    reps = LANES // half
    signs = jnp.tile(jnp.concatenate([-jnp.ones((half,), F32), jnp.ones((half,), F32)]), LANES // HEAD_DIM)
    return cos.T, sin.T, jnp.tile(cos, (1, reps)), jnp.tile(sin, (1, reps)) * signs[None, :]


def _overlap_T(S, n_cmp_pad):
    n_cmp = (S - L_CMP) // CMP_STRIDE + 1
    n_sel = S // L_SEL
    cs = np.arange(n_cmp_pad) * CMP_STRIDE
    ss = np.arange(n_sel) * L_SEL
    ov = (cs[None, :] < ss[:, None] + L_SEL) & (cs[None, :] + L_CMP > ss[:, None]) & (np.arange(n_cmp_pad)[None, :] < n_cmp)
    return jnp.asarray(ov, dtype=BF16)


def _compress_params(pos, w1, b1, w2, b2):
    half = L_CMP // 2
    w1r = w1.reshape(2, L_CMP, HEAD_DIM, CMP_HIDDEN)

    def spread(w):
        z = jnp.zeros((2, N_KV_GROUPS, half, N_KV_GROUPS, HEAD_DIM, CMP_HIDDEN), F32)
        for g in range(N_KV_GROUPS):
            z = z.at[:, g, :, g].set(w)
        return z.reshape(2, N_KV_GROUPS, half * LANES, CMP_HIDDEN).astype(BF16)

    def spread_pos(p):
        return jnp.tile(p[:, :, None, :], (1, 1, N_KV_GROUPS, 1)).reshape(2, 1, half * LANES)

    w2p = jnp.zeros((2, N_KV_GROUPS, CMP_HIDDEN, N_KV_GROUPS, HEAD_DIM), F32)
    for g in range(N_KV_GROUPS):
        w2p = w2p.at[:, g, :, g].set(w2)
    return (spread(w1r[:, :half]), spread(w1r[:, half:]), spread_pos(pos[:, :half]), spread_pos(pos[:, half:]),
            b1[:, None, :], w2p.reshape(2, N_KV_GROUPS, CMP_HIDDEN, LANES).astype(BF16),
            jnp.tile(b2, (1, N_KV_GROUPS))[:, None, :])


def _pad_cols(w, width):
    return jnp.pad(w, ((0, 0), (0, width - w.shape[1])))


def kernel(x, mem, attn_norm, ffn_norm, mem_norm, w_mem_kv, w_o, w_up, conv_w, conv_b, w_down, a_w_in, a_gate_b,
           a_cmp_pos, a_cmp_w1, a_cmp_b1, a_cmp_w2, a_cmp_b2, b_w_in, kv_norm, w_kv_shared, b_fgate, final_norm):
    B, S, _ = x.shape
    assert S % FOX_TQ == 0 and S % (CMP_STRIDE * 8) == 0
    n_chunks = S // CMP_STRIDE
    cosT, sinT, rc, rs = _rope_tables(jnp.arange(S))
    _, _, rc_cmp, rs_cmp = _rope_tables(jnp.arange(n_chunks) * CMP_STRIDE + L_CMP - 1)
    ovT = _overlap_T(S, n_chunks)

    rows3 = lambda v: v.reshape(v.shape[0], 1, -1).astype(F32)
    layer = lambda a, l, block, *idx: _w(a, (None,) + block, l, *idx)
    gain = lambda a, l: layer(a, l, (1, D_MODEL), 0, 0)
    attn_g, ffn_g, mem_g = rows3(attn_norm), rows3(ffn_norm), rows3(mem_norm)
    w_mem_b, w_o_b, w_up_b, w_down_b = (a.astype(BF16) for a in (w_mem_kv, w_o, w_up, w_down))
    b_w_b, w_sh_b = b_w_in.astype(BF16), w_kv_shared.astype(BF16)
    a_w_q = a_w_in[:, :, :2 * MIX_W].astype(BF16)
    a_w_b = a_w_in[:, :, 2 * MIX_W:].astype(BF16)
    conv_b3 = rows3(conv_b)
    a_wg = jnp.pad(a_w_b[:, :, :GATE_W], ((0, 0), (0, 0), (0, LANES - GATE_W)))
    a_gb = jnp.pad(rows3(a_gate_b), ((0, 0), (0, 0), (0, LANES - GATE_W)))
    a_wqm = a_w_b[:, :, GATE_W:]
    cmp_params = jax.vmap(_compress_params)(a_cmp_pos, a_cmp_w1, a_cmp_b1, a_cmp_w2, a_cmp_b2)
    w_f = _pad_cols(w_sh_b[:, 2 * MIX_W:], LANES)
    b_f = _pad_cols(b_fgate[None, :].astype(F32), LANES)
    fin_g = final_norm.reshape(1, -1).astype(F32)

    k_sh = vT_sh = dcum = dcumT = None
    for l in range(DEPTH):
        kmem, vmemT = _mem_kv(mem, [gain(mem_g, l), layer(w_mem_b, l, (D_MODEL, 2 * MEM_W), 0, 0)])
        if l < N_A:
            qT, kcmp, vcmp, kslc, kwin, vslcT, vwinT, gT, qmT = _proj_nsa(
                x, [gain(attn_g, l), layer(a_w_q, l, (D_MODEL, MIX_W), 0, 0), layer(a_w_q, l, (D_MODEL, MIX_W), 0, 1),
                    layer(a_wg, l, (D_MODEL, LANES), 0, 0), layer(a_gb, l, (1, LANES), 0, 0),
                    layer(a_wqm, l, (D_MODEL, MEM_W), 0, 0)],
                cosT, sinT, rc, rs)
            kc, vcT = _compress(kcmp, vcmp, [layer(a, l, a.shape[1:], *(0,) * (a.ndim - 1)) for a in cmp_params],
                                rc_cmp, rs_cmp)
            omix = _nsa_attn(qT, gT, kc, vcT, kslc, kwin, vslcT, vwinT, ovT)
        else:
            if l == N_A:
                k_sh, vT_sh, dcum, dcumT = _kv_shared(
                    x, [_w(kv_norm.reshape(1, -1).astype(F32)), _w(w_sh_b, (D_MODEL, MIX_W), 0, 0),
                        _w(w_sh_b, (D_MODEL, MIX_W), 0, 1), _w(w_f), _w(b_f)])
            qT, qmT = _proj_fox(x, [gain(attn_g, l), layer(b_w_b, l - N_A, (D_MODEL, MIX_W), 0, 0),
                                    layer(b_w_b, l - N_A, (D_MODEL, MEM_W), 0, MIX_W // MEM_W)])
            omix = _fox_attn(qT, k_sh, vT_sh, dcum, dcumT)
        x = _ffn(x, omix, qmT, kmem, vmemT,
                 [layer(w_o_b, l, (MIX_W, D_MODEL), 0, 0), layer(w_o_b, l, (MEM_W, D_MODEL), MIX_W // MEM_W, 0),
                  gain(ffn_g, l), layer(w_up_b, l, (D_MODEL, D_FF), 0, 0), layer(w_up_b, l, (D_MODEL, D_FF), 0, 1),
                  layer(conv_w, l, (CONV_WIDTH, D_FF), 0, 0), layer(conv_w, l, (CONV_WIDTH, D_FF), 0, 1),
                  layer(conv_b3, l, (1, D_FF), 0, 0), layer(conv_b3, l, (1, D_FF), 0, 1),
                  layer(w_down_b, l, (D_FF, D_MODEL), 0, 0), _w(fin_g)],
                 final=(l == DEPTH - 1))
    return x
```

```python
import functools

import numpy as np
import jax
import jax.numpy as jnp
from jax import lax
from jax.experimental import pallas as pl
from jax.experimental.pallas import tpu as pltpu

D_MODEL = 1024
HEAD_DIM = 64
N_MIX_HEADS = 12
N_KV_GROUPS = 2
HEADS_PER_GROUP = N_MIX_HEADS // N_KV_GROUPS
N_MEM_HEADS = 4
L_CMP = 32
CMP_STRIDE = 16
CMP_HIDDEN = 256
L_SEL = 64
TOP_N = 16
WINDOW = 512
D_FF = 2816
CONV_WIDTH = 3
ROPE_THETA = 10000.0
EPS = 1e-6
NEG = -1e30
FORCE_BONUS = 1e4
MIX_W = N_MIX_HEADS * HEAD_DIM
MEM_W = N_MEM_HEADS * HEAD_DIM
GATE_W = 3 * N_MIX_HEADS
N_A = 2
DEPTH = 4
LOG2E = 1.4426950408889634
Q_MUL = HEAD_DIM ** -0.5 * LOG2E
ACC_ROWS = HEAD_DIM + 16

LANES = 128
TM = 512
QB = 128
NSA_KC = 256
MASKED = 2.0 * NEG
FOX_TQ = 512
FOX_KC = 256
FF_TM = 512
FF_CHUNK = 256
GATE_ROWS = 40
VMEM_LIMIT = 56 * 1024 * 1024

F32 = jnp.float32
BF16 = jnp.bfloat16


def _params(sem):
    return pltpu.CompilerParams(dimension_semantics=sem, vmem_limit_bytes=VMEM_LIMIT)


def _const_spec(shape):
    n = len(shape)
    return pl.BlockSpec(shape, lambda *_: (0,) * n, pipeline_mode=pl.Buffered(1))


def _w(arr, block=None, *idx):
    if block is None:
        block, idx = arr.shape, (0,) * arr.ndim
    return arr, pl.BlockSpec(block, lambda *_: idx, pipeline_mode=pl.Buffered(1))


def _split_params(params):
    return [a for a, _ in params], [s for _, s in params]


def _rms(x, g):
    return x * lax.rsqrt(jnp.mean(x * x, axis=-1, keepdims=True) + EPS) * g


def _dot(a, b):
    return jnp.dot(a, b, preferred_element_type=F32)


def _rope_lanes(x, cos, sin_signed):
    lane = lax.broadcasted_iota(jnp.int32, x.shape, 1)
    first_half = (lane & (HEAD_DIM - 1)) < (HEAD_DIM // 2)
    partner = jnp.where(first_half, pltpu.roll(x, LANES - HEAD_DIM // 2, 1), pltpu.roll(x, HEAD_DIM // 2, 1))
    return x * cos + partner * sin_signed


def _split3(x):
    a = x.astype(BF16)
    r = x - a.astype(F32)
    b = r.astype(BF16)
    c = (r - b.astype(F32)).astype(BF16)
    return a, b, c


def _proj_nsa_kernel(x_ref, g_ref, wq_ref, wkv_ref, wg_ref, gb_ref, wqm_ref, cosT_ref, sinT_ref, rc_ref, rs_ref,
                     qT_ref, kcmp_ref, vcmp_ref, kslc_ref, kwin_ref, vslcT_ref, vwinT_ref, gT_ref, qmT_ref, cmp_ref):
    h = _rms(x_ref[0], g_ref[...]).astype(BF16)
    qT = (_dot(h, wq_ref[...]) * Q_MUL).T
    cos = cosT_ref[...]
    sin = sinT_ref[...]
    half = HEAD_DIM // 2
    for hh in range(N_MIX_HEADS):
        x1 = qT[hh * HEAD_DIM:hh * HEAD_DIM + half]
        x2 = qT[hh * HEAD_DIM + half:(hh + 1) * HEAD_DIM]
        qT_ref[0, hh * HEAD_DIM:hh * HEAD_DIM + half, :] = (x1 * cos - x2 * sin).astype(BF16)
        qT_ref[0, hh * HEAD_DIM + half:(hh + 1) * HEAD_DIM, :] = (x2 * cos + x1 * sin).astype(BF16)
    kv = _dot(h, wkv_ref[...])
    rc = rc_ref[...]
    rs = rs_ref[...]
    cmp_ref[0] = kv[:, 0:128]
    cmp_ref[1] = kv[:, 128:256]
    for r in range(CMP_STRIDE):
        kcmp_ref[0, :, r * LANES:(r + 1) * LANES] = cmp_ref[0, pl.ds(r, TM // CMP_STRIDE, stride=CMP_STRIDE), :]
        vcmp_ref[0, :, r * LANES:(r + 1) * LANES] = cmp_ref[1, pl.ds(r, TM // CMP_STRIDE, stride=CMP_STRIDE), :]
    kslc_ref[0] = _rope_lanes(kv[:, 256:384], rc, rs).astype(BF16)
    kwin_ref[0] = _rope_lanes(kv[:, 512:640], rc, rs).astype(BF16)
    vsT = kv[:, 384:512].T.astype(BF16)
    vwT = kv[:, 640:768].T.astype(BF16)
    for j in range(TM // NSA_KC):
        vslcT_ref[0, j] = vsT[:, j * NSA_KC:(j + 1) * NSA_KC]
        vwinT_ref[0, j] = vwT[:, j * NSA_KC:(j + 1) * NSA_KC]
    gates = jax.nn.sigmoid(_dot(h, wg_ref[...]) + gb_ref[...])
    gT_ref[0] = gates.T[0:GATE_ROWS]
    qmT_ref[0] = (_dot(h, wqm_ref[...]) * Q_MUL).T.astype(BF16)


def _proj_nsa(x, params, cosT, sinT, rc, rs):
    B, S, _ = x.shape
    nt = S // TM
    arrs, specs = _split_params(params)
    tok = lambda w: pl.BlockSpec((1, TM, w), lambda b, j: (b, j, 0))
    featT = lambda r: pl.BlockSpec((1, r, TM), lambda b, j: (b, 0, j))
    vT = pl.BlockSpec((1, TM // NSA_KC, LANES, NSA_KC), lambda b, j: (b, j, 0, 0))
    chunked = pl.BlockSpec((1, TM // CMP_STRIDE, CMP_STRIDE * LANES), lambda b, j: (b, j, 0))
    return pl.pallas_call(
        _proj_nsa_kernel,
        grid=(B, nt),
        in_specs=[tok(D_MODEL)] + specs + [
                  pl.BlockSpec((HEAD_DIM // 2, TM), lambda b, j: (0, j)),
                  pl.BlockSpec((HEAD_DIM // 2, TM), lambda b, j: (0, j)),
                  pl.BlockSpec((TM, LANES), lambda b, j: (j, 0)),
                  pl.BlockSpec((TM, LANES), lambda b, j: (j, 0))],
        out_specs=[featT(MIX_W), chunked, chunked, tok(LANES), tok(LANES), vT, vT, featT(GATE_ROWS),
                   featT(MEM_W)],
        out_shape=[jax.ShapeDtypeStruct((B, MIX_W, S), BF16),
                   jax.ShapeDtypeStruct((B, S // CMP_STRIDE, CMP_STRIDE * LANES), F32),
                   jax.ShapeDtypeStruct((B, S // CMP_STRIDE, CMP_STRIDE * LANES), F32),
                   jax.ShapeDtypeStruct((B, S, LANES), BF16), jax.ShapeDtypeStruct((B, S, LANES), BF16),
                   jax.ShapeDtypeStruct((B, S // NSA_KC, LANES, NSA_KC), BF16),
                   jax.ShapeDtypeStruct((B, S // NSA_KC, LANES, NSA_KC), BF16),
                   jax.ShapeDtypeStruct((B, GATE_ROWS, S), F32),
                   jax.ShapeDtypeStruct((B, MEM_W, S), BF16)],
        scratch_shapes=[pltpu.VMEM((2, TM, LANES), F32)],
        compiler_params=_params(("arbitrary", "arbitrary")),
        name="proj_nsa",
    )(x, *arrs, cosT, sinT, rc, rs)


def _proj_fox_kernel(x_ref, g_ref, wq_ref, wqm_ref, qT_ref, qmT_ref):
    h = _rms(x_ref[0], g_ref[...]).astype(BF16)
    qT_ref[0] = (_dot(h, wq_ref[...]) * Q_MUL).T.astype(BF16)
    qmT_ref[0] = (_dot(h, wqm_ref[...]) * Q_MUL).T.astype(BF16)


def _proj_fox(x, params):
    B, S, _ = x.shape
    arrs, specs = _split_params(params)
    featT = lambda r: pl.BlockSpec((1, r, TM), lambda b, j: (b, 0, j))
    return pl.pallas_call(
        _proj_fox_kernel,
        grid=(B, S // TM),
        in_specs=[pl.BlockSpec((1, TM, D_MODEL), lambda b, j: (b, j, 0))] + specs,
        out_specs=[featT(MIX_W), featT(MEM_W)],
        out_shape=[jax.ShapeDtypeStruct((B, MIX_W, S), BF16), jax.ShapeDtypeStruct((B, MEM_W, S), BF16)],
        compiler_params=_params(("arbitrary", "arbitrary")),
        name="proj_fox",
    )(x, *arrs)


def _kv_shared_kernel(x_ref, g_ref, wk_ref, wv_ref, wf_ref, bf_ref, k_ref, vT_ref, kbias_ref, dcumT_ref, carry_ref):
    j = pl.program_id(1)
    h = _rms(x_ref[0], g_ref[...]).astype(BF16)
    k_ref[0] = _dot(h, wk_ref[...]).astype(BF16)
    vT = _dot(h, wv_ref[...]).T.astype(BF16)
    for i in range(TM // FOX_KC):
        vT_ref[0, i] = vT[:, i * FOX_KC:(i + 1) * FOX_KC]
    z = _dot(h, wf_ref[...]) + bf_ref[...]
    logf = jnp.minimum(z, 0.0) - jnp.log1p(jnp.exp(-jnp.abs(z)))
    row = lax.broadcasted_iota(jnp.int32, (TM, TM), 0)
    col = lax.broadcasted_iota(jnp.int32, (TM, TM), 1)
    tri = (col <= row).astype(BF16)
    a, b, c = _split3(logf)
    @pl.when(j == 0)
    def _():
        carry_ref[...] = jnp.zeros(carry_ref.shape, F32)

    cs = _dot(tri, a) + _dot(tri, b) + _dot(tri, c) + carry_ref[7:8, :]
    carry_ref[...] = cs[TM - 8:TM]
    dcumT_ref[0] = cs.T[0:16]
    parts = _split3(cs * LOG2E)
    src = lax.broadcasted_iota(jnp.int32, (LANES, LANES), 0)
    dst = lax.broadcasted_iota(jnp.int32, (LANES, LANES), 1)
    lane = lax.broadcasted_iota(jnp.int32, (1, LANES), 1)
    ones = ((lane >= 3) & (lane < 6)).astype(F32)
    for p in range(N_MIX_HEADS // 2):
        acc = ones
        for r in range(3):
            place = ((src == 2 * p) & (dst == r)) | ((src == 2 * p + 1) & (dst == 6 + r))
            acc = acc + _dot(parts[r], place.astype(BF16))
        kbias_ref[0, p] = acc.astype(BF16)


def _kv_shared(x, params):
    B, S, _ = x.shape
    arrs, specs = _split_params(params)
    return pl.pallas_call(
        _kv_shared_kernel,
        grid=(B, S // TM),
        in_specs=[pl.BlockSpec((1, TM, D_MODEL), lambda b, j: (b, j, 0))] + specs,
        out_specs=[pl.BlockSpec((1, TM, MIX_W), lambda b, j: (b, j, 0)),
                   pl.BlockSpec((1, TM // FOX_KC, MIX_W, FOX_KC), lambda b, j: (b, j, 0, 0)),
                   pl.BlockSpec((1, N_MIX_HEADS // 2, TM, LANES), lambda b, j: (b, 0, j, 0)),
                   pl.BlockSpec((1, 16, TM), lambda b, j: (b, 0, j))],
        out_shape=[jax.ShapeDtypeStruct((B, S, MIX_W), BF16),
                   jax.ShapeDtypeStruct((B, S // FOX_KC, MIX_W, FOX_KC), BF16),
                   jax.ShapeDtypeStruct((B, N_MIX_HEADS // 2, S, LANES), BF16),
                   jax.ShapeDtypeStruct((B, 16, S), F32)],
        scratch_shapes=[pltpu.VMEM((8, LANES), F32)],
        compiler_params=_params(("arbitrary", "arbitrary")),
        name="kv_shared",
    )(x, *arrs)


def _mem_kv_kernel(m_ref, g_ref, w_ref, k_ref, vT_ref):
    h = _rms(m_ref[0], g_ref[...]).astype(BF16)
    kv = _dot(h, w_ref[...])
    k_ref[0] = kv[:, 0:MEM_W].astype(BF16)
    vT_ref[0] = kv[:, MEM_W:2 * MEM_W].T.astype(BF16)


def _mem_kv(mem, params):
    B, M, _ = mem.shape
    (g, w), specs = _split_params(params)
    return pl.pallas_call(
        _mem_kv_kernel,
        grid=(B,),
        in_specs=[pl.BlockSpec((1, M, D_MODEL), lambda b: (b, 0, 0))] + specs,
        out_specs=[pl.BlockSpec((1, M, MEM_W), lambda b: (b, 0, 0)), pl.BlockSpec((1, MEM_W, M), lambda b: (b, 0, 0))],
        out_shape=[jax.ShapeDtypeStruct((B, M, MEM_W), BF16), jax.ShapeDtypeStruct((B, MEM_W, M), BF16)],
        compiler_params=_params(("arbitrary",)),
        name="mem_kv",
    )(mem, g, w)


def _compress_kernel(kx_ref, vx_ref, wtop_ref, wbot_ref, ptop_ref, pbot_ref, b1_ref, w2_ref, b2_ref, rc_ref, rs_ref,
                     kc_ref, vcT_ref):
    for kvi, x_ref in enumerate((kx_ref, vx_ref)):
        x = x_ref[0]
        xt = (x + ptop_ref[kvi]).astype(BF16)
        xb = (x + pbot_ref[kvi]).astype(BF16)
        out = b2_ref[kvi]
        for g in range(N_KV_GROUPS):
            top = _dot(xt, wtop_ref[kvi, g])
            bot = _dot(xb, wbot_ref[kvi, g])
            hid = top + pltpu.roll(bot, bot.shape[0] - 1, 0) + b1_ref[kvi]
            out = out + _dot(jax.nn.gelu(hid).astype(BF16), w2_ref[kvi, g])
        if kvi == 0:
            kc_ref[0] = _rope_lanes(out, rc_ref[...], rs_ref[...]).astype(BF16)
        else:
            vcT_ref[0] = out.T.astype(BF16)


def _compress(kx, vx, params, rc, rs):
    B, NC, W = kx.shape
    xs = pl.BlockSpec((1, NC, W), lambda b: (b, 0, 0))
    arrs, specs = _split_params(params)
    return pl.pallas_call(
        _compress_kernel,
        grid=(B,),
        in_specs=[xs, xs] + specs + [_const_spec(rc.shape), _const_spec(rs.shape)],
        out_specs=[pl.BlockSpec((1, NC, LANES), lambda b: (b, 0, 0)), pl.BlockSpec((1, LANES, NC), lambda b: (b, 0, 0))],
        out_shape=[jax.ShapeDtypeStruct((B, NC, LANES), BF16), jax.ShapeDtypeStruct((B, LANES, NC), BF16)],
        compiler_params=_params(("arbitrary",)),
        name="nsa_compress",
    )(kx, vx, *arrs, rc, rs)


def _online_tile(s, m_ref, lead, cols):
    m_old = m_ref[lead, :, cols]
    m_new = jnp.maximum(m_old, jnp.max(s, axis=0, keepdims=True))
    m_ref[lead, :, cols] = m_new
    return jnp.exp2(s - m_new).astype(BF16), jnp.exp2(m_old - m_new)


def _reset_state(m_ref, acc_ref):
    m_ref[...] = jnp.full(m_ref.shape, NEG, F32)
    acc_ref[...] = jnp.zeros(acc_ref.shape, F32)


def _with_ones(vT):
    return jnp.concatenate([vT, jnp.ones((ACC_ROWS - HEAD_DIM, vT.shape[1]), vT.dtype)], axis=0)


def _normalise(acc):
    return acc[0:HEAD_DIM] / jnp.maximum(acc[HEAD_DIM:HEAD_DIM + 1], 1e-30)


def _nsa_attn_kernel(qT_ref, gT_ref, kc_ref, vcT_ref, kslc_ref, kwin_ref, vslcT_ref, vwinT_ref, ovT_ref,
                     o_ref, sel_ref, m_ref, acc_ref):
    c = pl.program_id(1)
    n_sel = sel_ref.shape[1]
    n_cmp = kc_ref.shape[1]
    GQ = HEADS_PER_GROUP * QB
    GW = HEADS_PER_GROUP * HEAD_DIM
    t_row = c * QB + lax.broadcasted_iota(jnp.int32, (1, QB), 1)
    gates = gT_ref[0]

    q6 = [jnp.concatenate([qT_ref[0, (HEADS_PER_GROUP * g + hh) * HEAD_DIM:(HEADS_PER_GROUP * g + hh + 1) * HEAD_DIM, :]
                           for hh in range(HEADS_PER_GROUP)], axis=1) for g in range(N_KV_GROUPS)]
    z = jnp.zeros_like(q6[0])
    qbd = jnp.concatenate([jnp.concatenate([q6[0], z], axis=1), jnp.concatenate([z, q6[1]], axis=1)], axis=0)

    cmp_end = lax.broadcasted_iota(jnp.int32, (n_cmp, QB), 0) * CMP_STRIDE + (L_CMP - 1)
    valid_c = cmp_end <= t_row
    valid_cf = valid_c.astype(F32)
    s_cmp = _dot(kc_ref[0], qbd)
    j_idx = lax.broadcasted_iota(jnp.int32, (n_sel, QB), 0)
    blk_t = lax.shift_right_logical(t_row, 6)
    forced = ((j_idx == 0) | (j_idx == blk_t) | (j_idx == blk_t - 1)).astype(F32)
    o_cmp = []
    for g in range(N_KV_GROUPS):
        psum = jnp.zeros((n_cmp, QB), F32)
        ps = []
        for hh in range(HEADS_PER_GROUP):
            col = g * GQ + hh * QB
            sh = jnp.where(valid_c, s_cmp[:, col:col + QB], NEG)
            e = jnp.exp2(sh - jnp.max(sh, axis=0, keepdims=True)) * valid_cf
            p = e / jnp.maximum(jnp.sum(e, axis=0, keepdims=True), 1e-30)
            psum = psum + p
            ps.append(p.astype(BF16))
        o_cmp.append(_dot(vcT_ref[0, g * HEAD_DIM:(g + 1) * HEAD_DIM, :], jnp.concatenate(ps, axis=1)))
        p_hi = psum.astype(BF16)
        p_lo = (psum - p_hi.astype(F32)).astype(BF16)
        imp = _dot(ovT_ref[...], p_hi) + _dot(ovT_ref[...], p_lo)
        score = jnp.where(j_idx <= blk_t, imp + FORCE_BONUS * forced, NEG)
        cnt = jnp.zeros((n_sel, QB), F32)
        for i in range(n_sel):
            si = score[i:i + 1, :]
            cnt = cnt + ((si > score) | ((si == score) & (j_idx > i))).astype(F32)
        sel = (cnt < float(min(TOP_N, n_sel))).astype(F32)
        for i in range(n_sel):
            sel_ref[g, i] = jnp.broadcast_to(sel[i:i + 1, :], (8, QB))

    _reset_state(m_ref, acc_ref)
    key_off = lax.broadcasted_iota(jnp.int32, (NSA_KC, QB), 0)

    def scores(k_ref, kb):
        return _dot(k_ref[0, kb * NSA_KC:(kb + 1) * NSA_KC, :], qbd)

    def consume(s, br, vT_ref, kb, masks):
        for g in range(N_KV_GROUPS):
            ps, alphas = [], []
            for hh in range(HEADS_PER_GROUP):
                col = g * GQ + hh * QB
                tile = s[:, col:col + QB]
                if masks[g] is not None:
                    tile = jnp.where(masks[g], tile, MASKED)
                p, a = _online_tile(tile, m_ref, br, slice(col, col + QB))
                ps.append(p)
                alphas.append(a)
            gc = slice(g * GQ, (g + 1) * GQ)
            vT = _with_ones(vT_ref[0, kb, g * HEAD_DIM:(g + 1) * HEAD_DIM, :])
            acc_ref[br, :, gc] = acc_ref[br, :, gc] * jnp.concatenate(alphas, axis=1) + _dot(vT, jnp.concatenate(ps, axis=1))

    def slc_masks(kb, causal):
        per_chunk = NSA_KC // L_SEL
        masks = []
        for g in range(N_KV_GROUPS):
            picked = sel_ref[g, per_chunk * kb + per_chunk - 1][0:1, :]
            for i in reversed(range(per_chunk - 1)):
                picked = jnp.where(key_off < (i + 1) * L_SEL, sel_ref[g, per_chunk * kb + i][0:1, :], picked)
            mk = picked > 0.5
            masks.append(mk & (kb * NSA_KC + key_off <= t_row) if causal else mk)
        return masks

    def win_masks(kb, last):
        key_pos = kb * NSA_KC + key_off
        if kb == last:
            wm = key_pos <= t_row
        elif kb == last - WINDOW // NSA_KC:
            wm = t_row - key_pos < WINDOW
        else:
            wm = None
        return [wm, wm]

    last_dyn = lax.shift_right_logical(c * QB, NSA_KC.bit_length() - 1)
    for last in range(kslc_ref.shape[1] // NSA_KC):
        @pl.when(last_dyn == last)
        def _(last=last):
            first_win = max(last - WINDOW // NSA_KC, 0)
            plan = [(0, kslc_ref, vslcT_ref, kb, False) for kb in range(first_win)]
            for kb in range(first_win, last + 1):
                plan += [(0, kslc_ref, vslcT_ref, kb, kb == last), (1, kwin_ref, vwinT_ref, kb, None)]
            s_next = scores(plan[0][1], plan[0][3])
            for i, (br, _, vT_ref, kb, causal) in enumerate(plan):
                s_cur = s_next
                if i + 1 < len(plan):
                    s_next = scores(plan[i + 1][1], plan[i + 1][3])
                consume(s_cur, br, vT_ref, kb, slc_masks(kb, causal) if br == 0 else win_masks(kb, last))

    o_slc = _normalise(acc_ref[0])
    o_win = _normalise(acc_ref[1])

    for g in range(N_KV_GROUPS):
        outs = []
        for hh in range(HEADS_PER_GROUP):
            r = 3 * (HEADS_PER_GROUP * g + hh)
            cols = slice(g * GQ + hh * QB, g * GQ + (hh + 1) * QB)
            outs.append(gates[r:r + 1] * o_cmp[g][:, hh * QB:(hh + 1) * QB] + gates[r + 1:r + 2] * o_slc[:, cols]
                        + gates[r + 2:r + 3] * o_win[:, cols])
        o_ref[0, :, g * GW:(g + 1) * GW] = jnp.concatenate(outs, axis=0).T.astype(BF16)


def _nsa_attn(qT, gT, kc, vcT, kslc, kwin, vslcT, vwinT, ovT):
    B, _, S = qT.shape
    n_sel = S // L_SEL
    W12 = N_MIX_HEADS * QB
    per_b3 = lambda a: pl.BlockSpec((1,) + a.shape[1:], lambda b, c: (b, 0, 0))
    per_b4 = lambda a: pl.BlockSpec((1,) + a.shape[1:], lambda b, c: (b, 0, 0, 0))
    return pl.pallas_call(
        _nsa_attn_kernel,
        grid=(B, S // QB),
        in_specs=[pl.BlockSpec((1, MIX_W, QB), lambda b, c: (b, 0, c)),
                  pl.BlockSpec((1, GATE_ROWS, QB), lambda b, c: (b, 0, c)),
                  per_b3(kc), per_b3(vcT), per_b3(kslc), per_b3(kwin), per_b4(vslcT), per_b4(vwinT),
                  _const_spec(ovT.shape)],
        out_specs=pl.BlockSpec((1, QB, MIX_W), lambda b, c: (b, c, 0)),
        out_shape=jax.ShapeDtypeStruct((B, S, MIX_W), BF16),
        scratch_shapes=[pltpu.VMEM((N_KV_GROUPS, n_sel, 8, QB), F32), pltpu.VMEM((2, 1, W12), F32),
                        pltpu.VMEM((2, ACC_ROWS, W12), F32)],
        compiler_params=_params(("arbitrary", "arbitrary")),
        name="nsa_attn",
    )(qT, gT, kc, vcT, kslc, kwin, vslcT, vwinT, ovT)


def _pair_q(q2, width):
    z = jnp.zeros((HEAD_DIM, width), q2.dtype)
    return jnp.concatenate([jnp.concatenate([q2[0:HEAD_DIM], z], axis=1),
                            jnp.concatenate([z, q2[HEAD_DIM:2 * HEAD_DIM]], axis=1)], axis=0)


def _fox_attn_kernel(qT_ref, k_ref, vT_ref, kbias_ref, dcumT_ref, o_ref, m_ref, acc_ref):
    p = pl.program_id(1)
    c = pl.program_id(2)
    nq = FOX_TQ // QB
    per_q = FOX_TQ // FOX_KC

    rowi = lax.broadcasted_iota(jnp.int32, (16, 2 * FOX_TQ), 0)
    head1 = lax.broadcasted_iota(jnp.int32, (16, 2 * FOX_TQ), 1) >= FOX_TQ
    bias = jnp.where(((rowi < 3) & ~head1) | ((rowi >= 6) & (rowi < 9) & head1), -1.0, 0.0)
    dq = [_split3(dcumT_ref[0, pl.ds(2 * p + hh, 1), :] * LOG2E) for hh in range(2)]
    for r in range(3):
        bias = jnp.where(rowi == 3 + r, jnp.concatenate([dq[0][r], dq[1][r]], axis=1).astype(F32), bias)
    qaug = jnp.concatenate([_pair_q(qT_ref[0], FOX_TQ), bias.astype(BF16),
                            jnp.zeros((LANES - 16, 2 * FOX_TQ), BF16)], axis=0)

    _reset_state(m_ref, acc_ref)
    lane_t =lax.broadcasted_iota(jnp.int32, (1, QB), 1)

    def scores(kb, n_kc, qs_lo, masked):
        start, size = kb * FOX_KC, n_kc * FOX_KC
        kk = jnp.concatenate([k_ref[0, pl.ds(start, size), :], kbias_ref[0, 0, pl.ds(start, size), :]], axis=1)
        live = FOX_TQ - qs_lo * QB
        if qs_lo == 0:
            s2 = _dot(kk, qaug)
            return [s2[:, 0:FOX_TQ], s2[:, FOX_TQ:2 * FOX_TQ]]
        return [_dot(kk, qaug[:, (hh + 1) * FOX_TQ - live:(hh + 1) * FOX_TQ]) for hh in range(2)]

    def consume(s, kb, n_kc, qs_lo, masked):
        live = FOX_TQ - qs_lo * QB
        if masked:
            key_pos = kb * FOX_KC + lax.broadcasted_iota(jnp.int32, (n_kc * FOX_KC, QB), 0)
        for hh in range(2):
            cols = slice((hh + 1) * FOX_TQ - live, (hh + 1) * FOX_TQ)
            ps, alphas = [], []
            for qs in range(qs_lo, nq):
                tile = s[hh][:, (qs - qs_lo) * QB:(qs - qs_lo + 1) * QB]
                if masked:
                    tile = jnp.where(key_pos <= c * FOX_TQ + qs * QB + lane_t, tile, MASKED)
                col = hh * FOX_TQ + qs * QB
                pr, a = _online_tile(tile, m_ref, 0, slice(col, col + QB))
                ps.append(pr)
                alphas.append(a)
            vT = jnp.concatenate([vT_ref[0, kb + i, hh * HEAD_DIM:(hh + 1) * HEAD_DIM, :] for i in range(n_kc)], axis=1)
            acc_ref[0, :, cols] = (acc_ref[0, :, cols] * jnp.concatenate(alphas, axis=1)
                                   + _dot(_with_ones(vT), jnp.concatenate(ps, axis=1)))

    for cc in range(k_ref.shape[1] // FOX_TQ):
        @pl.when(c == cc)
        def _(cc=cc):
            plan = [(i * per_q, per_q, 0, False) for i in range(cc)]
            plan += [(cc * per_q + i, 1, i * FOX_KC // QB, True) for i in range(per_q)]
            s_next = scores(*plan[0])
            for i, step in enumerate(plan):
                s_cur = s_next
                if i + 1 < len(plan):
                    s_next = scores(*plan[i + 1])
                consume(s_cur, *step)
    o = _normalise(acc_ref[0])
    oT =jnp.concatenate([o[:, 0:FOX_TQ], o[:, FOX_TQ:2 * FOX_TQ]], axis=0)
    o_ref[0] = oT.T.astype(BF16)


def _fox_attn(qT, k, vT, kbias, dcumT):
    B, _, S = qT.shape
    n_pairs = N_MIX_HEADS // 2
    return pl.pallas_call(
        _fox_attn_kernel,
        grid=(B, n_pairs, S // FOX_TQ),
        in_specs=[pl.BlockSpec((1, LANES, FOX_TQ), lambda b, p, c: (b, p, c)),
                  pl.BlockSpec((1, S, LANES), lambda b, p, c: (b, 0, p)),
                  pl.BlockSpec((1, S // FOX_KC, LANES, FOX_KC), lambda b, p, c: (b, 0, p, 0)),
                  pl.BlockSpec((1, 1, S, LANES), lambda b, p, c: (b, p, 0, 0)),
                  pl.BlockSpec((1, 16, FOX_TQ), lambda b, p, c: (b, 0, c))],
        out_specs=pl.BlockSpec((1, FOX_TQ, LANES), lambda b, p, c: (b, c, p)),
        out_shape=jax.ShapeDtypeStruct((B, S, MIX_W), BF16),
        scratch_shapes=[pltpu.VMEM((1, 1, 2 * FOX_TQ), F32), pltpu.VMEM((1, ACC_ROWS, 2 * FOX_TQ), F32)],
        compiler_params=_params(("arbitrary", "arbitrary", "arbitrary")),
        name="fox_attn",
    )(qT, k, vT, kbias, dcumT)


def _mem_scores(qT_ref, k_ref):
    tq = qT_ref.shape[2]
    return [_dot(k_ref[0, :, pr * LANES:(pr + 1) * LANES], _pair_q(qT_ref[0, pr * LANES:(pr + 1) * LANES, :], tq))
            for pr in range(N_MEM_HEADS // 2)]


def _mem_finish(scores, vT_ref):
    tq = scores[0].shape[1] // 2
    outs = []
    for pr, s in enumerate(scores):
        for hh in range(2):
            ps = []
            for qs in range(tq // QB):
                col = hh * tq + qs * QB
                sh = s[:, col:col + QB]
                e = jnp.exp2(sh - jnp.max(sh, axis=0, keepdims=True))
                ps.append((e / jnp.sum(e, axis=0, keepdims=True)).astype(BF16))
            h = 2 * pr + hh
            outs.append(_dot(vT_ref[0, h * HEAD_DIM:(h + 1) * HEAD_DIM, :], jnp.concatenate(ps, axis=1)))
    return jnp.concatenate(outs, axis=0).T.astype(BF16)


def _ffn_kernel(x_ref, omix_ref, qmT_ref, kmem_ref, vmemT_ref, qmT_nx_ref, kmem_nx_ref, vmemT_nx_ref, wo1_ref, wo2_ref,
                g_ref, wa_ref, wb_ref, cwa_ref, cwb_ref, cba_ref, cbb_ref, wd_ref, gfin_ref, o_ref, ca_ref, cb_ref,
                gate_ref, omem_ref, *, final):
    j = pl.program_id(1)
    slot = j % 2

    @pl.when((pl.program_id(0) == 0) & (j == 0))
    def _():
        omem_ref[0] = _mem_finish(_mem_scores(qmT_ref, kmem_ref), vmemT_ref)

    x1 = x_ref[0] + _dot(omix_ref[0], wo1_ref[...]) + _dot(omem_ref[slot], wo2_ref[...])
    h = _rms(x1, g_ref[...]).astype(BF16)

    @pl.when(j == 0)
    def _():
        ca_ref[...] = jnp.zeros(ca_ref.shape, F32)
        cb_ref[...] = jnp.zeros(cb_ref.shape, F32)

    def conv(u, prev8, w, bias):
        ext = jnp.concatenate([prev8, u], axis=0)
        u1 = pltpu.roll(ext, 1, 0)[8:]
        u2 = pltpu.roll(ext, 2, 0)[8:]
        return w[0:1] * u2 + w[1:2] * u1 + w[2:3] * u + bias

    next_scores = _mem_scores(qmT_nx_ref, kmem_nx_ref)
    for ci in range(D_FF // FF_CHUNK):
        sl = slice(ci * FF_CHUNK, (ci + 1) * FF_CHUNK)
        a = _dot(h, wa_ref[:, sl])
        b = _dot(h, wb_ref[:, sl])
        if ci == 2:
            omem_ref[1 - slot] = _mem_finish(next_scores, vmemT_nx_ref)
        pa = ca_ref[:, sl]
        pb = cb_ref[:, sl]
        ca_ref[:, sl] = a[FF_TM - 8:FF_TM]
        cb_ref[:, sl] = b[FF_TM - 8:FF_TM]
        ac = conv(a, pa, cwa_ref[:, sl], cba_ref[:, sl])
        bc = conv(b, pb, cwb_ref[:, sl], cbb_ref[:, sl])
        gate_ref[:, sl] = (ac * jax.nn.sigmoid(ac) * bc).astype(BF16)
    y = x1 + _dot(gate_ref[...], wd_ref[...])
    o_ref[0] = _rms(y, gfin_ref[...]) if final else y


def _ffn(x, omix, qmT, kmem, vmemT, params, final):
    B, S, _ = x.shape
    M = kmem.shape[1]
    nt = S // FF_TM
    assert nt % 2 == 0
    tok = lambda w: pl.BlockSpec((1, FF_TM, w), lambda b, j: (b, j, 0))
    nxt = lambda b, j: jnp.minimum(b * nt + j + 1, B * nt - 1)
    arrs, specs = _split_params(params)
    return pl.pallas_call(
        functools.partial(_ffn_kernel, final=final),
        grid=(B, nt),
        in_specs=[tok(D_MODEL), tok(MIX_W), pl.BlockSpec((1, MEM_W, FF_TM), lambda b, j: (b, 0, j)),
                  pl.BlockSpec((1, M, MEM_W), lambda b, j: (b, 0, 0)),
                  pl.BlockSpec((1, MEM_W, M), lambda b, j: (b, 0, 0)),
                  pl.BlockSpec((1, MEM_W, FF_TM), lambda b, j: (nxt(b, j) // nt, 0, nxt(b, j) % nt)),
                  pl.BlockSpec((1, M, MEM_W), lambda b, j: (nxt(b, j) // nt, 0, 0)),
                  pl.BlockSpec((1, MEM_W, M), lambda b, j: (nxt(b, j) // nt, 0, 0))] + specs,
        out_specs=tok(D_MODEL),
        out_shape=jax.ShapeDtypeStruct((B, S, D_MODEL), F32),
        scratch_shapes=[pltpu.VMEM((8, D_FF), F32), pltpu.VMEM((8, D_FF), F32), pltpu.VMEM((FF_TM, D_FF), BF16),
                        pltpu.VMEM((2, FF_TM, MEM_W), BF16)],
        compiler_params=_params(("arbitrary", "arbitrary")),
        name="memattn_outproj_convffn",
    )(x, omix, qmT, kmem, vmemT, qmT, kmem, vmemT, *arrs)


def _rope_tables(pos):
    half = HEAD_DIM // 2
    inv = ROPE_THETA ** (-jnp.arange(half, dtype=F32) / half)
    ang = pos.astype(F32)[:, None] * inv[None, :]
    cos, sin = jnp.cos(ang), jnp.sin(ang)
    reps = LANES // half
    signs = jnp.tile(jnp.concatenate([-jnp.ones((half,), F32), jnp.ones((half,), F32)]), LANES // HEAD_DIM)
    return cos.T, sin.T, jnp.tile(cos, (1, reps)), jnp.tile(sin, (1, reps)) * signs[None, :]


def _overlap_T(S, n_cmp_pad):
    n_cmp = (S - L_CMP) // CMP_STRIDE + 1
    n_sel = S // L_SEL
    cs = np.arange(n_cmp_pad) * CMP_STRIDE
    ss = np.arange(n_sel) * L_SEL
    ov = (cs[None, :] < ss[:, None] + L_SEL) & (cs[None, :] + L_CMP > ss[:, None]) & (np.arange(n_cmp_pad)[None, :] < n_cmp)
    return jnp.asarray(ov, dtype=BF16)


def _compress_params(pos, w1, b1, w2, b2):
    half = L_CMP // 2
    w1r = w1.reshape(2, L_CMP, HEAD_DIM, CMP_HIDDEN)

    def spread(w):
        z = jnp.zeros((2, N_KV_GROUPS, half, N_KV_GROUPS, HEAD_DIM, CMP_HIDDEN), F32)
        for g in range(N_KV_GROUPS):
            z = z.at[:, g, :, g].set(w)
        return z.reshape(2, N_KV_GROUPS, half * LANES, CMP_HIDDEN).astype(BF16)

    def spread_pos(p):
        return jnp.tile(p[:, :, None, :], (1, 1, N_KV_GROUPS, 1)).reshape(2, 1, half * LANES)

    w2p = jnp.zeros((2, N_KV_GROUPS, CMP_HIDDEN, N_KV_GROUPS, HEAD_DIM), F32)
    for g in range(N_KV_GROUPS):
        w2p = w2p.at[:, g, :, g].set(w2)
    return (spread(w1r[:, :half]), spread(w1r[:, half:]), spread_pos(pos[:, :half]), spread_pos(pos[:, half:]),
            b1[:, None, :], w2p.reshape(2, N_KV_GROUPS, CMP_HIDDEN, LANES).astype(BF16),
            jnp.tile(b2, (1, N_KV_GROUPS))[:, None, :])


def _pad_cols(w, width):
    return jnp.pad(w, ((0, 0), (0, width - w.shape[1])))


def kernel(x, mem, attn_norm, ffn_norm, mem_norm, w_mem_kv, w_o, w_up, conv_w, conv_b, w_down, a_w_in, a_gate_b,
           a_cmp_pos, a_cmp_w1, a_cmp_b1, a_cmp_w2, a_cmp_b2, b_w_in, kv_norm, w_kv_shared, b_fgate, final_norm):
    B, S, _ = x.shape
    assert S % FOX_TQ == 0 and S % (CMP_STRIDE * 8) == 0
    n_chunks = S // CMP_STRIDE
    cosT, sinT, rc, rs = _rope_tables(jnp.arange(S))
    _, _, rc_cmp, rs_cmp = _rope_tables(jnp.arange(n_chunks) * CMP_STRIDE + L_CMP - 1)
    ovT = _overlap_T(S, n_chunks)

    rows3 = lambda v: v.reshape(v.shape[0], 1, -1).astype(F32)
    layer = lambda a, l, block, *idx: _w(a, (None,) + block, l, *idx)
    gain = lambda a, l: layer(a, l, (1, D_MODEL), 0, 0)
    attn_g, ffn_g, mem_g = rows3(attn_norm), rows3(ffn_norm), rows3(mem_norm)
    w_mem_b, w_o_b, w_up_b, w_down_b = (a.astype(BF16) for a in (w_mem_kv, w_o, w_up, w_down))
    b_w_b, w_sh_b = b_w_in.astype(BF16), w_kv_shared.astype(BF16)
    a_w_q = a_w_in[:, :, :2 * MIX_W].astype(BF16)
    a_w_b = a_w_in[:, :, 2 * MIX_W:].astype(BF16)
    conv_b3 = rows3(conv_b)
    a_wg = jnp.pad(a_w_b[:, :, :GATE_W], ((0, 0), (0, 0), (0, LANES - GATE_W)))
    a_gb = jnp.pad(rows3(a_gate_b), ((0, 0), (0, 0), (0, LANES - GATE_W)))
    a_wqm = a_w_b[:, :, GATE_W:]
    cmp_params = jax.vmap(_compress_params)(a_cmp_pos, a_cmp_w1, a_cmp_b1, a_cmp_w2, a_cmp_b2)
    w_f = _pad_cols(w_sh_b[:, 2 * MIX_W:], LANES)
    b_f = _pad_cols(b_fgate[None, :].astype(F32), LANES)
    fin_g = final_norm.reshape(1, -1).astype(F32)

    k_sh = vT_sh = kbias = dcumT = None
    for l in range(DEPTH):
        kmem, vmemT = _mem_kv(mem, [gain(mem_g, l), layer(w_mem_b, l, (D_MODEL, 2 * MEM_W), 0, 0)])
        if l < N_A:
            qT, kcmp, vcmp, kslc, kwin, vslcT, vwinT, gT, qmT = _proj_nsa(
                x, [gain(attn_g, l), layer(a_w_q, l, (D_MODEL, MIX_W), 0, 0), layer(a_w_q, l, (D_MODEL, MIX_W), 0, 1),
                    layer(a_wg, l, (D_MODEL, LANES), 0, 0), layer(a_gb, l, (1, LANES), 0, 0),
                    layer(a_wqm, l, (D_MODEL, MEM_W), 0, 0)],
                cosT, sinT, rc, rs)
            kc, vcT = _compress(kcmp, vcmp, [layer(a, l, a.shape[1:], *(0,) * (a.ndim - 1)) for a in cmp_params],
                                rc_cmp, rs_cmp)
            omix = _nsa_attn(qT, gT, kc, vcT, kslc, kwin, vslcT, vwinT, ovT)
        else:
            if l == N_A:
                k_sh, vT_sh, kbias, dcumT = _kv_shared(
                    x, [_w(kv_norm.reshape(1, -1).astype(F32)), _w(w_sh_b, (D_MODEL, MIX_W), 0, 0),
                        _w(w_sh_b, (D_MODEL, MIX_W), 0, 1), _w(w_f), _w(b_f)])
            qT, qmT = _proj_fox(x, [gain(attn_g, l), layer(b_w_b, l - N_A, (D_MODEL, MIX_W), 0, 0),
                                    layer(b_w_b, l - N_A, (D_MODEL, MEM_W), 0, MIX_W // MEM_W)])
            omix = _fox_attn(qT, k_sh, vT_sh, kbias, dcumT)
        x = _ffn(x, omix, qmT, kmem, vmemT,
                 [layer(w_o_b, l, (MIX_W, D_MODEL), 0, 0), layer(w_o_b, l, (MEM_W, D_MODEL), MIX_W // MEM_W, 0),
                  gain(ffn_g, l), layer(w_up_b, l, (D_MODEL, D_FF), 0, 0), layer(w_up_b, l, (D_MODEL, D_FF), 0, 1),
                  layer(conv_w, l, (CONV_WIDTH, D_FF), 0, 0), layer(conv_w, l, (CONV_WIDTH, D_FF), 0, 1),
                  layer(conv_b3, l, (1, D_FF), 0, 0), layer(conv_b3, l, (1, D_FF), 0, 1),
                  layer(w_down_b, l, (D_FF, D_MODEL), 0, 0), _w(fin_g)],
                 final=(l == DEPTH - 1))
    return x
```

```python
import functools

import numpy as np
import jax
import jax.numpy as jnp
from jax import lax
from jax.experimental import pallas as pl
from jax.experimental.pallas import tpu as pltpu

D_MODEL = 1024
HEAD_DIM = 64
N_MIX_HEADS = 12
N_KV_GROUPS = 2
HEADS_PER_GROUP = N_MIX_HEADS // N_KV_GROUPS
N_MEM_HEADS = 4
L_CMP = 32
CMP_STRIDE = 16
CMP_HIDDEN = 256
L_SEL = 64
TOP_N = 16
WINDOW = 512
D_FF = 2816
CONV_WIDTH = 3
ROPE_THETA = 10000.0
EPS = 1e-6
NEG = -1e30
FORCE_BONUS = 1e4
MIX_W = N_MIX_HEADS * HEAD_DIM
MEM_W = N_MEM_HEADS * HEAD_DIM
GATE_W = 3 * N_MIX_HEADS
N_A = 2
DEPTH = 4
LOG2E = 1.4426950408889634
Q_MUL = HEAD_DIM ** -0.5 * LOG2E
ACC_ROWS = HEAD_DIM + 16

LANES = 128
TM = 1024
KV_TM = 512
QB = 128
NSA_KC = 256
MASKED = 2.0 * NEG
FOX_TQ = 512
FOX_KC = 256
FF_TM = 512
FF_CHUNK = 256
GATE_ROWS = 40
VMEM_LIMIT = 56 * 1024 * 1024

F32 = jnp.float32
BF16 = jnp.bfloat16


def _params(sem):
    return pltpu.CompilerParams(dimension_semantics=sem, vmem_limit_bytes=VMEM_LIMIT)


def _const_spec(shape):
    n = len(shape)
    return pl.BlockSpec(shape, lambda *_: (0,) * n, pipeline_mode=pl.Buffered(1))


def _w(arr, block=None, *idx):
    if block is None:
        block, idx = arr.shape, (0,) * arr.ndim
    return arr, pl.BlockSpec(block, lambda *_: idx, pipeline_mode=pl.Buffered(1))


def _split_params(params):
    return [a for a, _ in params], [s for _, s in params]


def _rms(x, g):
    return x * lax.rsqrt(jnp.mean(x * x, axis=-1, keepdims=True) + EPS) * g


def _dot(a, b):
    return jnp.dot(a, b, preferred_element_type=F32)


def _rope_lanes(x, cos, sin_signed):
    lane = lax.broadcasted_iota(jnp.int32, x.shape, 1)
    first_half = (lane & (HEAD_DIM - 1)) < (HEAD_DIM // 2)
    partner = jnp.where(first_half, pltpu.roll(x, LANES - HEAD_DIM // 2, 1), pltpu.roll(x, HEAD_DIM // 2, 1))
    return x * cos + partner * sin_signed


def _split3(x):
    a = x.astype(BF16)
    r = x - a.astype(F32)
    b = r.astype(BF16)
    c = (r - b.astype(F32)).astype(BF16)
    return a, b, c


def _proj_nsa_kernel(x_ref, g_ref, wq_ref, wkv_ref, wg_ref, gb_ref, wqm_ref, cosT_ref, sinT_ref, rc_ref, rs_ref,
                     qT_ref, kcmp_ref, vcmp_ref, kslc_ref, kwin_ref, vslcT_ref, vwinT_ref, gT_ref, qmT_ref, cmp_ref):
    h = _rms(x_ref[0], g_ref[...]).astype(BF16)
    qT = (_dot(h, wq_ref[...]) * Q_MUL).T
    cos = cosT_ref[...]
    sin = sinT_ref[...]
    half = HEAD_DIM // 2
    for hh in range(N_MIX_HEADS):
        x1 = qT[hh * HEAD_DIM:hh * HEAD_DIM + half]
        x2 = qT[hh * HEAD_DIM + half:(hh + 1) * HEAD_DIM]
        qT_ref[0, hh * HEAD_DIM:hh * HEAD_DIM + half, :] = (x1 * cos - x2 * sin).astype(BF16)
        qT_ref[0, hh * HEAD_DIM + half:(hh + 1) * HEAD_DIM, :] = (x2 * cos + x1 * sin).astype(BF16)
    kv = _dot(h, wkv_ref[...])
    rc = rc_ref[...]
    rs = rs_ref[...]
    cmp_ref[0] = kv[:, 0:128]
    cmp_ref[1] = kv[:, 128:256]
    for r in range(CMP_STRIDE):
        kcmp_ref[0, :, r * LANES:(r + 1) * LANES] = cmp_ref[0, pl.ds(r, TM // CMP_STRIDE, stride=CMP_STRIDE), :]
        vcmp_ref[0, :, r * LANES:(r + 1) * LANES] = cmp_ref[1, pl.ds(r, TM // CMP_STRIDE, stride=CMP_STRIDE), :]
    kslc_ref[0] = _rope_lanes(kv[:, 256:384], rc, rs).astype(BF16)
    kwin_ref[0] = _rope_lanes(kv[:, 512:640], rc, rs).astype(BF16)
    vsT = kv[:, 384:512].T.astype(BF16)
    vwT = kv[:, 640:768].T.astype(BF16)
    for j in range(TM // NSA_KC):
        vslcT_ref[0, j] = vsT[:, j * NSA_KC:(j + 1) * NSA_KC]
        vwinT_ref[0, j] = vwT[:, j * NSA_KC:(j + 1) * NSA_KC]
    gates = jax.nn.sigmoid(_dot(h, wg_ref[...]) + gb_ref[...])
    gT_ref[0] = gates.T[0:GATE_ROWS]
    qmT_ref[0] = (_dot(h, wqm_ref[...]) * Q_MUL).T.astype(BF16)


def _proj_nsa(x, params, cosT, sinT, rc, rs):
    B, S, _ = x.shape
    nt = S // TM
    arrs, specs = _split_params(params)
    tok = lambda w: pl.BlockSpec((1, TM, w), lambda b, j: (b, j, 0))
    featT = lambda r: pl.BlockSpec((1, r, TM), lambda b, j: (b, 0, j))
    vT = pl.BlockSpec((1, TM // NSA_KC, LANES, NSA_KC), lambda b, j: (b, j, 0, 0))
    chunked = pl.BlockSpec((1, TM // CMP_STRIDE, CMP_STRIDE * LANES), lambda b, j: (b, j, 0))
    return pl.pallas_call(
        _proj_nsa_kernel,
        grid=(B, nt),
        in_specs=[tok(D_MODEL)] + specs + [
                  pl.BlockSpec((HEAD_DIM // 2, TM), lambda b, j: (0, j)),
                  pl.BlockSpec((HEAD_DIM // 2, TM), lambda b, j: (0, j)),
                  pl.BlockSpec((TM, LANES), lambda b, j: (j, 0)),
                  pl.BlockSpec((TM, LANES), lambda b, j: (j, 0))],
        out_specs=[featT(MIX_W), chunked, chunked, tok(LANES), tok(LANES), vT, vT, featT(GATE_ROWS),
                   featT(MEM_W)],
        out_shape=[jax.ShapeDtypeStruct((B, MIX_W, S), BF16),
                   jax.ShapeDtypeStruct((B, S // CMP_STRIDE, CMP_STRIDE * LANES), F32),
                   jax.ShapeDtypeStruct((B, S // CMP_STRIDE, CMP_STRIDE * LANES), F32),
                   jax.ShapeDtypeStruct((B, S, LANES), BF16), jax.ShapeDtypeStruct((B, S, LANES), BF16),
                   jax.ShapeDtypeStruct((B, S // NSA_KC, LANES, NSA_KC), BF16),
                   jax.ShapeDtypeStruct((B, S // NSA_KC, LANES, NSA_KC), BF16),
                   jax.ShapeDtypeStruct((B, GATE_ROWS, S), F32),
                   jax.ShapeDtypeStruct((B, MEM_W, S), BF16)],
        scratch_shapes=[pltpu.VMEM((2, TM, LANES), F32)],
        compiler_params=_params(("arbitrary", "arbitrary")),
        name="proj_nsa",
    )(x, *arrs, cosT, sinT, rc, rs)


def _proj_fox_kernel(x_ref, g_ref, wq_ref, wqm_ref, qT_ref, qmT_ref):
    h = _rms(x_ref[0], g_ref[...]).astype(BF16)
    qT_ref[0] = (_dot(h, wq_ref[...]) * Q_MUL).T.astype(BF16)
    qmT_ref[0] = (_dot(h, wqm_ref[...]) * Q_MUL).T.astype(BF16)


def _proj_fox(x, params):
    B, S, _ = x.shape
    arrs, specs = _split_params(params)
    featT = lambda r: pl.BlockSpec((1, r, TM), lambda b, j: (b, 0, j))
    return pl.pallas_call(
        _proj_fox_kernel,
        grid=(B, S // TM),
        in_specs=[pl.BlockSpec((1, TM, D_MODEL), lambda b, j: (b, j, 0))] + specs,
        out_specs=[featT(MIX_W), featT(MEM_W)],
        out_shape=[jax.ShapeDtypeStruct((B, MIX_W, S), BF16), jax.ShapeDtypeStruct((B, MEM_W, S), BF16)],
        compiler_params=_params(("arbitrary", "arbitrary")),
        name="proj_fox",
    )(x, *arrs)


def _kv_shared_kernel(x_ref, g_ref, wk_ref, wv_ref, wf_ref, bf_ref, k_ref, vT_ref, kbias_ref, dcumT_ref, carry_ref):
    j = pl.program_id(1)
    h = _rms(x_ref[0], g_ref[...]).astype(BF16)
    k_ref[0] = _dot(h, wk_ref[...]).astype(BF16)
    vT = _dot(h, wv_ref[...]).T.astype(BF16)
    for i in range(KV_TM // FOX_KC):
        vT_ref[0, i] = vT[:, i * FOX_KC:(i + 1) * FOX_KC]
    z = _dot(h, wf_ref[...]) + bf_ref[...]
    logf = jnp.minimum(z, 0.0) - jnp.log1p(jnp.exp(-jnp.abs(z)))
    row = lax.broadcasted_iota(jnp.int32, (KV_TM, KV_TM), 0)
    col = lax.broadcasted_iota(jnp.int32, (KV_TM, KV_TM), 1)
    tri = (col <= row).astype(BF16)
    a, b, c = _split3(logf)
    @pl.when(j == 0)
    def _():
        carry_ref[...] = jnp.zeros(carry_ref.shape, F32)

    cs = _dot(tri, a) + _dot(tri, b) + _dot(tri, c) + carry_ref[7:8, :]
    carry_ref[...] = cs[KV_TM - 8:KV_TM]
    dcumT_ref[0] = cs.T[0:16]
    parts = _split3(cs * LOG2E)
    src = lax.broadcasted_iota(jnp.int32, (LANES, LANES), 0)
    dst = lax.broadcasted_iota(jnp.int32, (LANES, LANES), 1)
    lane = lax.broadcasted_iota(jnp.int32, (1, LANES), 1)
    ones = ((lane >= 3) & (lane < 6)).astype(F32)
    for p in range(N_MIX_HEADS // 2):
        acc = ones
        for r in range(3):
            place = ((src == 2 * p) & (dst == r)) | ((src == 2 * p + 1) & (dst == 6 + r))
            acc = acc + _dot(parts[r], place.astype(BF16))
        kbias_ref[0, p] = acc.astype(BF16)


def _kv_shared(x, params):
    B, S, _ = x.shape
    arrs, specs = _split_params(params)
    return pl.pallas_call(
        _kv_shared_kernel,
        grid=(B, S // KV_TM),
        in_specs=[pl.BlockSpec((1, KV_TM, D_MODEL), lambda b, j: (b, j, 0))] + specs,
        out_specs=[pl.BlockSpec((1, KV_TM, MIX_W), lambda b, j: (b, j, 0)),
                   pl.BlockSpec((1, KV_TM // FOX_KC, MIX_W, FOX_KC), lambda b, j: (b, j, 0, 0)),
                   pl.BlockSpec((1, N_MIX_HEADS // 2, KV_TM, LANES), lambda b, j: (b, 0, j, 0)),
                   pl.BlockSpec((1, 16, KV_TM), lambda b, j: (b, 0, j))],
        out_shape=[jax.ShapeDtypeStruct((B, S, MIX_W), BF16),
                   jax.ShapeDtypeStruct((B, S // FOX_KC, MIX_W, FOX_KC), BF16),
                   jax.ShapeDtypeStruct((B, N_MIX_HEADS // 2, S, LANES), BF16),
                   jax.ShapeDtypeStruct((B, 16, S), F32)],
        scratch_shapes=[pltpu.VMEM((8, LANES), F32)],
        compiler_params=_params(("arbitrary", "arbitrary")),
        name="kv_shared",
    )(x, *arrs)


def _mem_kv_kernel(m_ref, g_ref, w_ref, k_ref, vT_ref):
    h = _rms(m_ref[0], g_ref[...]).astype(BF16)
    kv = _dot(h, w_ref[...])
    k_ref[0] = kv[:, 0:MEM_W].astype(BF16)
    vT_ref[0] = kv[:, MEM_W:2 * MEM_W].T.astype(BF16)


def _mem_kv(mem, params):
    B, M, _ = mem.shape
    (g, w), specs = _split_params(params)
    return pl.pallas_call(
        _mem_kv_kernel,
        grid=(B,),
        in_specs=[pl.BlockSpec((1, M, D_MODEL), lambda b: (b, 0, 0))] + specs,
        out_specs=[pl.BlockSpec((1, M, MEM_W), lambda b: (b, 0, 0)), pl.BlockSpec((1, MEM_W, M), lambda b: (b, 0, 0))],
        out_shape=[jax.ShapeDtypeStruct((B, M, MEM_W), BF16), jax.ShapeDtypeStruct((B, MEM_W, M), BF16)],
        compiler_params=_params(("arbitrary",)),
        name="mem_kv",
    )(mem, g, w)


def _compress_kernel(kx_ref, vx_ref, wtop_ref, wbot_ref, ptop_ref, pbot_ref, b1_ref, w2_ref, b2_ref, rc_ref, rs_ref,
                     kc_ref, vcT_ref):
    for kvi, x_ref in enumerate((kx_ref, vx_ref)):
        x = x_ref[0]
        xt = (x + ptop_ref[kvi]).astype(BF16)
        xb = (x + pbot_ref[kvi]).astype(BF16)
        out = b2_ref[kvi]
        for g in range(N_KV_GROUPS):
            top = _dot(xt, wtop_ref[kvi, g])
            bot = _dot(xb, wbot_ref[kvi, g])
            hid = top + pltpu.roll(bot, bot.shape[0] - 1, 0) + b1_ref[kvi]
            out = out + _dot(jax.nn.gelu(hid).astype(BF16), w2_ref[kvi, g])
        if kvi == 0:
            kc_ref[0] = _rope_lanes(out, rc_ref[...], rs_ref[...]).astype(BF16)
        else:
            vcT_ref[0] = out.T.astype(BF16)


def _compress(kx, vx, params, rc, rs):
    B, NC, W = kx.shape
    xs = pl.BlockSpec((1, NC, W), lambda b: (b, 0, 0))
    arrs, specs = _split_params(params)
    return pl.pallas_call(
        _compress_kernel,
        grid=(B,),
        in_specs=[xs, xs] + specs + [_const_spec(rc.shape), _const_spec(rs.shape)],
        out_specs=[pl.BlockSpec((1, NC, LANES), lambda b: (b, 0, 0)), pl.BlockSpec((1, LANES, NC), lambda b: (b, 0, 0))],
        out_shape=[jax.ShapeDtypeStruct((B, NC, LANES), BF16), jax.ShapeDtypeStruct((B, LANES, NC), BF16)],
        compiler_params=_params(("arbitrary",)),
        name="nsa_compress",
    )(kx, vx, *arrs, rc, rs)


def _online_tile(s, m_ref, lead, cols):
    m_old = m_ref[lead, :, cols]
    m_new = jnp.maximum(m_old, jnp.max(s, axis=0, keepdims=True))
    m_ref[lead, :, cols] = m_new
    return jnp.exp2(s - m_new).astype(BF16), jnp.exp2(m_old - m_new)


def _reset_state(m_ref, acc_ref):
    m_ref[...] = jnp.full(m_ref.shape, NEG, F32)
    acc_ref[...] = jnp.zeros(acc_ref.shape, F32)


def _with_ones(vT):
    return jnp.concatenate([vT, jnp.ones((ACC_ROWS - HEAD_DIM, vT.shape[1]), vT.dtype)], axis=0)


def _normalise(acc):
    return acc[0:HEAD_DIM] / jnp.maximum(acc[HEAD_DIM:HEAD_DIM + 1], 1e-30)


def _nsa_attn_kernel(qT_ref, gT_ref, kc_ref, vcT_ref, kslc_ref, kwin_ref, vslcT_ref, vwinT_ref, ovT_ref,
                     o_ref, sel_ref, m_ref, acc_ref):
    c = pl.program_id(1)
    n_sel = sel_ref.shape[1]
    n_cmp = kc_ref.shape[1]
    GQ = HEADS_PER_GROUP * QB
    GW = HEADS_PER_GROUP * HEAD_DIM
    t_row = c * QB + lax.broadcasted_iota(jnp.int32, (1, QB), 1)
    gates = gT_ref[0]

    q6 = [jnp.concatenate([qT_ref[0, (HEADS_PER_GROUP * g + hh) * HEAD_DIM:(HEADS_PER_GROUP * g + hh + 1) * HEAD_DIM, :]
                           for hh in range(HEADS_PER_GROUP)], axis=1) for g in range(N_KV_GROUPS)]
    z = jnp.zeros_like(q6[0])
    qbd = jnp.concatenate([jnp.concatenate([q6[0], z], axis=1), jnp.concatenate([z, q6[1]], axis=1)], axis=0)

    cmp_end = lax.broadcasted_iota(jnp.int32, (n_cmp, QB), 0) * CMP_STRIDE + (L_CMP - 1)
    valid_c = cmp_end <= t_row
    valid_cf = valid_c.astype(F32)
    s_cmp = _dot(kc_ref[0], qbd)
    j_idx = lax.broadcasted_iota(jnp.int32, (n_sel, QB), 0)
    blk_t = lax.shift_right_logical(t_row, 6)
    forced = ((j_idx == 0) | (j_idx == blk_t) | (j_idx == blk_t - 1)).astype(F32)
    o_cmp = []
    for g in range(N_KV_GROUPS):
        psum = jnp.zeros((n_cmp, QB), F32)
        ps = []
        for hh in range(HEADS_PER_GROUP):
            col = g * GQ + hh * QB
            sh = jnp.where(valid_c, s_cmp[:, col:col + QB], NEG)
            e = jnp.exp2(sh - jnp.max(sh, axis=0, keepdims=True)) * valid_cf
            p = e / jnp.maximum(jnp.sum(e, axis=0, keepdims=True), 1e-30)
            psum = psum + p
            ps.append(p.astype(BF16))
        o_cmp.append(_dot(vcT_ref[0, g * HEAD_DIM:(g + 1) * HEAD_DIM, :], jnp.concatenate(ps, axis=1)))
        p_hi = psum.astype(BF16)
        p_lo = (psum - p_hi.astype(F32)).astype(BF16)
        imp = _dot(ovT_ref[...], p_hi) + _dot(ovT_ref[...], p_lo)
        score = jnp.where(j_idx <= blk_t, imp + FORCE_BONUS * forced, NEG)
        cnt = jnp.zeros((n_sel, QB), F32)
        for i in range(n_sel):
            si = score[i:i + 1, :]
            cnt = cnt + ((si > score) | ((si == score) & (j_idx > i))).astype(F32)
        sel = (cnt < float(min(TOP_N, n_sel))).astype(F32)
        for i in range(n_sel):
            sel_ref[g, i] = jnp.broadcast_to(sel[i:i + 1, :], (8, QB))

    _reset_state(m_ref, acc_ref)
    key_off = lax.broadcasted_iota(jnp.int32, (NSA_KC, QB), 0)

    def scores(k_ref, kb):
        return _dot(k_ref[0, kb * NSA_KC:(kb + 1) * NSA_KC, :], qbd)

    def consume(s, br, vT_ref, kb, masks):
        for g in range(N_KV_GROUPS):
            ps, alphas = [], []
            for hh in range(HEADS_PER_GROUP):
                col = g * GQ + hh * QB
                tile = s[:, col:col + QB]
                if masks[g] is not None:
                    tile = jnp.where(masks[g], tile, MASKED)
                p, a = _online_tile(tile, m_ref, br, slice(col, col + QB))
                ps.append(p)
                alphas.append(a)
            gc = slice(g * GQ, (g + 1) * GQ)
            vT = _with_ones(vT_ref[0, kb, g * HEAD_DIM:(g + 1) * HEAD_DIM, :])
            acc_ref[br, :, gc] = acc_ref[br, :, gc] * jnp.concatenate(alphas, axis=1) + _dot(vT, jnp.concatenate(ps, axis=1))

    def slc_masks(kb, causal):
        per_chunk = NSA_KC // L_SEL
        masks = []
        for g in range(N_KV_GROUPS):
            picked = sel_ref[g, per_chunk * kb + per_chunk - 1][0:1, :]
            for i in reversed(range(per_chunk - 1)):
                picked = jnp.where(key_off < (i + 1) * L_SEL, sel_ref[g, per_chunk * kb + i][0:1, :], picked)
            mk = picked > 0.5
            masks.append(mk & (kb * NSA_KC + key_off <= t_row) if causal else mk)
        return masks

    def win_masks(kb, last):
        key_pos = kb * NSA_KC + key_off
        if kb == last:
            wm = key_pos <= t_row
        elif kb == last - WINDOW // NSA_KC:
            wm = t_row - key_pos < WINDOW
        else:
            wm = None
        return [wm, wm]

    last_dyn = lax.shift_right_logical(c * QB, NSA_KC.bit_length() - 1)
    for last in range(kslc_ref.shape[1] // NSA_KC):
        @pl.when(last_dyn == last)
        def _(last=last):
            first_win = max(last - WINDOW // NSA_KC, 0)
            plan = [(0, kslc_ref, vslcT_ref, kb, False) for kb in range(first_win)]
            for kb in range(first_win, last + 1):
                plan += [(0, kslc_ref, vslcT_ref, kb, kb == last), (1, kwin_ref, vwinT_ref, kb, None)]
            s_next = scores(plan[0][1], plan[0][3])
            for i, (br, _, vT_ref, kb, causal) in enumerate(plan):
                s_cur = s_next
                if i + 1 < len(plan):
                    s_next = scores(plan[i + 1][1], plan[i + 1][3])
                consume(s_cur, br, vT_ref, kb, slc_masks(kb, causal) if br == 0 else win_masks(kb, last))

    o_slc = _normalise(acc_ref[0])
    o_win = _normalise(acc_ref[1])

    for g in range(N_KV_GROUPS):
        outs = []
        for hh in range(HEADS_PER_GROUP):
            r = 3 * (HEADS_PER_GROUP * g + hh)
            cols = slice(g * GQ + hh * QB, g * GQ + (hh + 1) * QB)
            outs.append(gates[r:r + 1] * o_cmp[g][:, hh * QB:(hh + 1) * QB] + gates[r + 1:r + 2] * o_slc[:, cols]
                        + gates[r + 2:r + 3] * o_win[:, cols])
        o_ref[0, :, g * GW:(g + 1) * GW] = jnp.concatenate(outs, axis=0).T.astype(BF16)


def _nsa_attn(qT, gT, kc, vcT, kslc, kwin, vslcT, vwinT, ovT):
    B, _, S = qT.shape
    n_sel = S // L_SEL
    W12 = N_MIX_HEADS * QB
    per_b3 = lambda a: pl.BlockSpec((1,) + a.shape[1:], lambda b, c: (b, 0, 0))
    per_b4 = lambda a: pl.BlockSpec((1,) + a.shape[1:], lambda b, c: (b, 0, 0, 0))
    return pl.pallas_call(
        _nsa_attn_kernel,
        grid=(B, S // QB),
        in_specs=[pl.BlockSpec((1, MIX_W, QB), lambda b, c: (b, 0, c)),
                  pl.BlockSpec((1, GATE_ROWS, QB), lambda b, c: (b, 0, c)),
                  per_b3(kc), per_b3(vcT), per_b3(kslc), per_b3(kwin), per_b4(vslcT), per_b4(vwinT),
                  _const_spec(ovT.shape)],
        out_specs=pl.BlockSpec((1, QB, MIX_W), lambda b, c: (b, c, 0)),
        out_shape=jax.ShapeDtypeStruct((B, S, MIX_W), BF16),
        scratch_shapes=[pltpu.VMEM((N_KV_GROUPS, n_sel, 8, QB), F32), pltpu.VMEM((2, 1, W12), F32),
                        pltpu.VMEM((2, ACC_ROWS, W12), F32)],
        compiler_params=_params(("arbitrary", "arbitrary")),
        name="nsa_attn",
    )(qT, gT, kc, vcT, kslc, kwin, vslcT, vwinT, ovT)


def _pair_q(q2, width):
    z = jnp.zeros((HEAD_DIM, width), q2.dtype)
    return jnp.concatenate([jnp.concatenate([q2[0:HEAD_DIM], z], axis=1),
                            jnp.concatenate([z, q2[HEAD_DIM:2 * HEAD_DIM]], axis=1)], axis=0)


def _fox_attn_kernel(qT_ref, k_ref, vT_ref, kbias_ref, dcumT_ref, o_ref, m_ref, acc_ref):
    p = pl.program_id(1)
    c = pl.program_id(2)
    nq = FOX_TQ // QB
    per_q = FOX_TQ // FOX_KC

    rowi = lax.broadcasted_iota(jnp.int32, (16, 2 * FOX_TQ), 0)
    head1 = lax.broadcasted_iota(jnp.int32, (16, 2 * FOX_TQ), 1) >= FOX_TQ
    bias = jnp.where(((rowi < 3) & ~head1) | ((rowi >= 6) & (rowi < 9) & head1), -1.0, 0.0)
    dq = [_split3(dcumT_ref[0, pl.ds(2 * p + hh, 1), :] * LOG2E) for hh in range(2)]
    for r in range(3):
        bias = jnp.where(rowi == 3 + r, jnp.concatenate([dq[0][r], dq[1][r]], axis=1).astype(F32), bias)
    qaug = jnp.concatenate([_pair_q(qT_ref[0], FOX_TQ), bias.astype(BF16),
                            jnp.zeros((LANES - 16, 2 * FOX_TQ), BF16)], axis=0)

    _reset_state(m_ref, acc_ref)
    lane_t =lax.broadcasted_iota(jnp.int32, (1, QB), 1)

    def scores(kb, n_kc, qs_lo, masked):
        start, size = kb * FOX_KC, n_kc * FOX_KC
        kk = jnp.concatenate([k_ref[0, pl.ds(start, size), :], kbias_ref[0, 0, pl.ds(start, size), :]], axis=1)
        live = FOX_TQ - qs_lo * QB
        if qs_lo == 0:
            s2 = _dot(kk, qaug)
            return [s2[:, 0:FOX_TQ], s2[:, FOX_TQ:2 * FOX_TQ]]
        return [_dot(kk, qaug[:, (hh + 1) * FOX_TQ - live:(hh + 1) * FOX_TQ]) for hh in range(2)]

    def consume(s, kb, n_kc, qs_lo, masked):
        live = FOX_TQ - qs_lo * QB
        if masked:
            key_pos = kb * FOX_KC + lax.broadcasted_iota(jnp.int32, (n_kc * FOX_KC, QB), 0)
        for hh in range(2):
            cols = slice((hh + 1) * FOX_TQ - live, (hh + 1) * FOX_TQ)
            ps, alphas = [], []
            for qs in range(qs_lo, nq):
                tile = s[hh][:, (qs - qs_lo) * QB:(qs - qs_lo + 1) * QB]
                if masked:
                    tile = jnp.where(key_pos <= c * FOX_TQ + qs * QB + lane_t, tile, MASKED)
                col = hh * FOX_TQ + qs * QB
                pr, a = _online_tile(tile, m_ref, 0, slice(col, col + QB))
                ps.append(pr)
                alphas.append(a)
            vT = jnp.concatenate([vT_ref[0, kb + i, hh * HEAD_DIM:(hh + 1) * HEAD_DIM, :] for i in range(n_kc)], axis=1)
            acc_ref[0, :, cols] = (acc_ref[0, :, cols] * jnp.concatenate(alphas, axis=1)
                                   + _dot(_with_ones(vT), jnp.concatenate(ps, axis=1)))

    for cc in range(k_ref.shape[1] // FOX_TQ):
        @pl.when(c == cc)
        def _(cc=cc):
            plan = [(i * per_q, per_q, 0, False) for i in range(cc)]
            plan += [(cc * per_q + i, 1, i * FOX_KC // QB, True) for i in range(per_q)]
            s_next = scores(*plan[0])
            for i, step in enumerate(plan):
                s_cur = s_next
                if i + 1 < len(plan):
                    s_next = scores(*plan[i + 1])
                consume(s_cur, *step)
    o = _normalise(acc_ref[0])
    oT =jnp.concatenate([o[:, 0:FOX_TQ], o[:, FOX_TQ:2 * FOX_TQ]], axis=0)
    o_ref[0] = oT.T.astype(BF16)


def _fox_attn(qT, k, vT, kbias, dcumT):
    B, _, S = qT.shape
    n_pairs = N_MIX_HEADS // 2
    return pl.pallas_call(
        _fox_attn_kernel,
        grid=(B, n_pairs, S // FOX_TQ),
        in_specs=[pl.BlockSpec((1, LANES, FOX_TQ), lambda b, p, c: (b, p, c)),
                  pl.BlockSpec((1, S, LANES), lambda b, p, c: (b, 0, p)),
                  pl.BlockSpec((1, S // FOX_KC, LANES, FOX_KC), lambda b, p, c: (b, 0, p, 0)),
                  pl.BlockSpec((1, 1, S, LANES), lambda b, p, c: (b, p, 0, 0)),
                  pl.BlockSpec((1, 16, FOX_TQ), lambda b, p, c: (b, 0, c))],
        out_specs=pl.BlockSpec((1, FOX_TQ, LANES), lambda b, p, c: (b, c, p)),
        out_shape=jax.ShapeDtypeStruct((B, S, MIX_W), BF16),
        scratch_shapes=[pltpu.VMEM((1, 1, 2 * FOX_TQ), F32), pltpu.VMEM((1, ACC_ROWS, 2 * FOX_TQ), F32)],
        compiler_params=_params(("arbitrary", "arbitrary", "arbitrary")),
        name="fox_attn",
    )(qT, k, vT, kbias, dcumT)


def _mem_scores(qT_ref, k_ref):
    tq = qT_ref.shape[2]
    return [_dot(k_ref[0, :, pr * LANES:(pr + 1) * LANES], _pair_q(qT_ref[0, pr * LANES:(pr + 1) * LANES, :], tq))
            for pr in range(N_MEM_HEADS // 2)]


def _mem_finish(scores, vT_ref):
    tq = scores[0].shape[1] // 2
    outs = []
    for pr, s in enumerate(scores):
        for hh in range(2):
            ps = []
            for qs in range(tq // QB):
                col = hh * tq + qs * QB
                sh = s[:, col:col + QB]
                e = jnp.exp2(sh - jnp.max(sh, axis=0, keepdims=True))
                ps.append((e / jnp.sum(e, axis=0, keepdims=True)).astype(BF16))
            h = 2 * pr + hh
            outs.append(_dot(vT_ref[0, h * HEAD_DIM:(h + 1) * HEAD_DIM, :], jnp.concatenate(ps, axis=1)))
    return jnp.concatenate(outs, axis=0).T.astype(BF16)


def _ffn_kernel(x_ref, omix_ref, qmT_ref, kmem_ref, vmemT_ref, wo1_ref, wo2_ref, g_ref, wa_ref, wb_ref, cwa_ref,
                cwb_ref, cba_ref, cbb_ref, wd_ref, gfin_ref, o_ref, ca_ref, cb_ref, gate_ref, *, final):
    j = pl.program_id(1)
    omem = _mem_finish(_mem_scores(qmT_ref, kmem_ref), vmemT_ref)
    x1 = x_ref[0] + _dot(omix_ref[0], wo1_ref[...]) + _dot(omem, wo2_ref[...])
    h = _rms(x1, g_ref[...]).astype(BF16)

    @pl.when(j == 0)
    def _():
        ca_ref[...] = jnp.zeros(ca_ref.shape, F32)
        cb_ref[...] = jnp.zeros(cb_ref.shape, F32)

    def conv(u, prev8, w, bias):
        ext = jnp.concatenate([prev8, u], axis=0)
        u1 = pltpu.roll(ext, 1, 0)[8:]
        u2 = pltpu.roll(ext, 2, 0)[8:]
        return w[0:1] * u2 + w[1:2] * u1 + w[2:3] * u + bias

    for ci in range(D_FF // FF_CHUNK):
        sl = slice(ci * FF_CHUNK, (ci + 1) * FF_CHUNK)
        a = _dot(h, wa_ref[:, sl])
        b = _dot(h, wb_ref[:, sl])
        pa = ca_ref[:, sl]
        pb = cb_ref[:, sl]
        ca_ref[:, sl] = a[FF_TM - 8:FF_TM]
        cb_ref[:, sl] = b[FF_TM - 8:FF_TM]
        ac = conv(a, pa, cwa_ref[:, sl], cba_ref[:, sl])
        bc = conv(b, pb, cwb_ref[:, sl], cbb_ref[:, sl])
        gate_ref[:, sl] = (ac * jax.nn.sigmoid(ac) * bc).astype(BF16)
    y = x1 + _dot(gate_ref[...], wd_ref[...])
    o_ref[0] = _rms(y, gfin_ref[...]) if final else y


def _ffn(x, omix, qmT, kmem, vmemT, params, final):
    B, S, _ = x.shape
    M = kmem.shape[1]
    tok = lambda w: pl.BlockSpec((1, FF_TM, w), lambda b, j: (b, j, 0))
    arrs, specs = _split_params(params)
    return pl.pallas_call(
        functools.partial(_ffn_kernel, final=final),
        grid=(B, S // FF_TM),
        in_specs=[tok(D_MODEL), tok(MIX_W), pl.BlockSpec((1, MEM_W, FF_TM), lambda b, j: (b, 0, j)),
                  pl.BlockSpec((1, M, MEM_W), lambda b, j: (b, 0, 0)),
                  pl.BlockSpec((1, MEM_W, M), lambda b, j: (b, 0, 0))] + specs,
        out_specs=tok(D_MODEL),
        out_shape=jax.ShapeDtypeStruct((B, S, D_MODEL), F32),
        scratch_shapes=[pltpu.VMEM((8, D_FF), F32), pltpu.VMEM((8, D_FF), F32), pltpu.VMEM((FF_TM, D_FF), BF16)],
        compiler_params=_params(("arbitrary", "arbitrary")),
        name="memattn_outproj_convffn",
    )(x, omix, qmT, kmem, vmemT, *arrs)


def _rope_tables(pos):
    half = HEAD_DIM // 2
    inv = ROPE_THETA ** (-jnp.arange(half, dtype=F32) / half)
    ang = pos.astype(F32)[:, None] * inv[None, :]
    cos, sin = jnp.cos(ang), jnp.sin(ang)
    reps = LANES // half
    signs = jnp.tile(jnp.concatenate([-jnp.ones((half,), F32), jnp.ones((half,), F32)]), LANES // HEAD_DIM)
    return cos.T, sin.T, jnp.tile(cos, (1, reps)), jnp.tile(sin, (1, reps)) * signs[None, :]


def _overlap_T(S, n_cmp_pad):
    n_cmp = (S - L_CMP) // CMP_STRIDE + 1
    n_sel = S // L_SEL
    cs = np.arange(n_cmp_pad) * CMP_STRIDE
    ss = np.arange(n_sel) * L_SEL
    ov = (cs[None, :] < ss[:, None] + L_SEL) & (cs[None, :] + L_CMP > ss[:, None]) & (np.arange(n_cmp_pad)[None, :] < n_cmp)
    return jnp.asarray(ov, dtype=BF16)


def _compress_params(pos, w1, b1, w2, b2):
    half = L_CMP // 2
    w1r = w1.reshape(2, L_CMP, HEAD_DIM, CMP_HIDDEN)

    def spread(w):
        z = jnp.zeros((2, N_KV_GROUPS, half, N_KV_GROUPS, HEAD_DIM, CMP_HIDDEN), F32)
        for g in range(N_KV_GROUPS):
            z = z.at[:, g, :, g].set(w)
        return z.reshape(2, N_KV_GROUPS, half * LANES, CMP_HIDDEN).astype(BF16)

    def spread_pos(p):
        return jnp.tile(p[:, :, None, :], (1, 1, N_KV_GROUPS, 1)).reshape(2, 1, half * LANES)

    w2p = jnp.zeros((2, N_KV_GROUPS, CMP_HIDDEN, N_KV_GROUPS, HEAD_DIM), F32)
    for g in range(N_KV_GROUPS):
        w2p = w2p.at[:, g, :, g].set(w2)
    return (spread(w1r[:, :half]), spread(w1r[:, half:]), spread_pos(pos[:, :half]), spread_pos(pos[:, half:]),
            b1[:, None, :], w2p.reshape(2, N_KV_GROUPS, CMP_HIDDEN, LANES).astype(BF16),
            jnp.tile(b2, (1, N_KV_GROUPS))[:, None, :])


def _pad_cols(w, width):
    return jnp.pad(w, ((0, 0), (0, width - w.shape[1])))


def kernel(x, mem, attn_norm, ffn_norm, mem_norm, w_mem_kv, w_o, w_up, conv_w, conv_b, w_down, a_w_in, a_gate_b,
           a_cmp_pos, a_cmp_w1, a_cmp_b1, a_cmp_w2, a_cmp_b2, b_w_in, kv_norm, w_kv_shared, b_fgate, final_norm):
    B, S, _ = x.shape
    assert S % FOX_TQ == 0 and S % (CMP_STRIDE * 8) == 0
    n_chunks = S // CMP_STRIDE
    cosT, sinT, rc, rs = _rope_tables(jnp.arange(S))
    _, _, rc_cmp, rs_cmp = _rope_tables(jnp.arange(n_chunks) * CMP_STRIDE + L_CMP - 1)
    ovT = _overlap_T(S, n_chunks)

    rows3 = lambda v: v.reshape(v.shape[0], 1, -1).astype(F32)
    layer = lambda a, l, block, *idx: _w(a, (None,) + block, l, *idx)
    gain = lambda a, l: layer(a, l, (1, D_MODEL), 0, 0)
    attn_g, ffn_g, mem_g = rows3(attn_norm), rows3(ffn_norm), rows3(mem_norm)
    w_mem_b, w_o_b, w_up_b, w_down_b = (a.astype(BF16) for a in (w_mem_kv, w_o, w_up, w_down))
    b_w_b, w_sh_b = b_w_in.astype(BF16), w_kv_shared.astype(BF16)
    a_w_q = a_w_in[:, :, :2 * MIX_W].astype(BF16)
    a_w_b = a_w_in[:, :, 2 * MIX_W:].astype(BF16)
    conv_b3 = rows3(conv_b)
    a_wg = jnp.pad(a_w_b[:, :, :GATE_W], ((0, 0), (0, 0), (0, LANES - GATE_W)))
    a_gb = jnp.pad(rows3(a_gate_b), ((0, 0), (0, 0), (0, LANES - GATE_W)))
    a_wqm = a_w_b[:, :, GATE_W:]
    cmp_params = jax.vmap(_compress_params)(a_cmp_pos, a_cmp_w1, a_cmp_b1, a_cmp_w2, a_cmp_b2)
    w_f = _pad_cols(w_sh_b[:, 2 * MIX_W:], LANES)
    b_f = _pad_cols(b_fgate[None, :].astype(F32), LANES)
    fin_g = final_norm.reshape(1, -1).astype(F32)

    k_sh = vT_sh = kbias = dcumT = None
    for l in range(DEPTH):
        kmem, vmemT = _mem_kv(mem, [gain(mem_g, l), layer(w_mem_b, l, (D_MODEL, 2 * MEM_W), 0, 0)])
        if l < N_A:
            qT, kcmp, vcmp, kslc, kwin, vslcT, vwinT, gT, qmT = _proj_nsa(
                x, [gain(attn_g, l), layer(a_w_q, l, (D_MODEL, MIX_W), 0, 0), layer(a_w_q, l, (D_MODEL, MIX_W), 0, 1),
                    layer(a_wg, l, (D_MODEL, LANES), 0, 0), layer(a_gb, l, (1, LANES), 0, 0),
                    layer(a_wqm, l, (D_MODEL, MEM_W), 0, 0)],
                cosT, sinT, rc, rs)
            kc, vcT = _compress(kcmp, vcmp, [layer(a, l, a.shape[1:], *(0,) * (a.ndim - 1)) for a in cmp_params],
                                rc_cmp, rs_cmp)
            omix = _nsa_attn(qT, gT, kc, vcT, kslc, kwin, vslcT, vwinT, ovT)
        else:
            if l == N_A:
                k_sh, vT_sh, kbias, dcumT = _kv_shared(
                    x, [_w(kv_norm.reshape(1, -1).astype(F32)), _w(w_sh_b, (D_MODEL, MIX_W), 0, 0),
                        _w(w_sh_b, (D_MODEL, MIX_W), 0, 1), _w(w_f), _w(b_f)])
            qT, qmT = _proj_fox(x, [gain(attn_g, l), layer(b_w_b, l - N_A, (D_MODEL, MIX_W), 0, 0),
                                    layer(b_w_b, l - N_A, (D_MODEL, MEM_W), 0, MIX_W // MEM_W)])
            omix = _fox_attn(qT, k_sh, vT_sh, kbias, dcumT)
        x = _ffn(x, omix, qmT, kmem, vmemT,
                 [layer(w_o_b, l, (MIX_W, D_MODEL), 0, 0), layer(w_o_b, l, (MEM_W, D_MODEL), MIX_W // MEM_W, 0),
                  gain(ffn_g, l), layer(w_up_b, l, (D_MODEL, D_FF), 0, 0), layer(w_up_b, l, (D_MODEL, D_FF), 0, 1),
                  layer(conv_w, l, (CONV_WIDTH, D_FF), 0, 0), layer(conv_w, l, (CONV_WIDTH, D_FF), 0, 1),
                  layer(conv_b3, l, (1, D_FF), 0, 0), layer(conv_b3, l, (1, D_FF), 0, 1),
                  layer(w_down_b, l, (D_FF, D_MODEL), 0, 0), _w(fin_g)],
                 final=(l == DEPTH - 1))
    return x
```

```python
import functools

import numpy as np
import jax
import jax.numpy as jnp
from jax import lax
from jax.experimental import pallas as pl
from jax.experimental.pallas import tpu as pltpu

D_MODEL = 1024
HEAD_DIM = 64
N_MIX_HEADS = 12
N_KV_GROUPS = 2
HEADS_PER_GROUP = N_MIX_HEADS // N_KV_GROUPS
N_MEM_HEADS = 4
L_CMP = 32
CMP_STRIDE = 16
CMP_HIDDEN = 256
L_SEL = 64
TOP_N = 16
WINDOW = 512
D_FF = 2816
CONV_WIDTH = 3
ROPE_THETA = 10000.0
EPS = 1e-6
NEG = -1e30
FORCE_BONUS = 1e4
MIX_W = N_MIX_HEADS * HEAD_DIM
MEM_W = N_MEM_HEADS * HEAD_DIM
GATE_W = 3 * N_MIX_HEADS
N_A = 2
DEPTH = 4
LOG2E = 1.4426950408889634
Q_MUL = HEAD_DIM ** -0.5 * LOG2E
ACC_ROWS = HEAD_DIM + 16

LANES = 128
TM = 1024
KV_TM = 512
QB = 128
NSA_KC = 256
MASKED = 2.0 * NEG
FOX_TQ = 512
FOX_KC = 256
FF_TM = 512
FF_CHUNK = 256
GATE_ROWS = 40
VMEM_LIMIT = 56 * 1024 * 1024

F32 = jnp.float32
BF16 = jnp.bfloat16


def _params(sem):
    return pltpu.CompilerParams(dimension_semantics=sem, vmem_limit_bytes=VMEM_LIMIT)


def _const_spec(shape):
    n = len(shape)
    return pl.BlockSpec(shape, lambda *_: (0,) * n, pipeline_mode=pl.Buffered(1))


def _w(arr, block=None, *idx):
    if block is None:
        block, idx = arr.shape, (0,) * arr.ndim
    return arr, pl.BlockSpec(block, lambda *_: idx, pipeline_mode=pl.Buffered(1))


def _split_params(params):
    return [a for a, _ in params], [s for _, s in params]


def _rms(x, g):
    return x * lax.rsqrt(jnp.mean(x * x, axis=-1, keepdims=True) + EPS) * g


def _dot(a, b):
    return jnp.dot(a, b, preferred_element_type=F32)


def _rope_lanes(x, cos, sin_signed):
    lane = lax.broadcasted_iota(jnp.int32, x.shape, 1)
    first_half = (lane & (HEAD_DIM - 1)) < (HEAD_DIM // 2)
    partner = jnp.where(first_half, pltpu.roll(x, LANES - HEAD_DIM // 2, 1), pltpu.roll(x, HEAD_DIM // 2, 1))
    return x * cos + partner * sin_signed


def _split3(x):
    a = x.astype(BF16)
    r = x - a.astype(F32)
    b = r.astype(BF16)
    c = (r - b.astype(F32)).astype(BF16)
    return a, b, c


def _proj_nsa_kernel(x_ref, g_ref, wq_ref, wkv_ref, wg_ref, gb_ref, wqm_ref, cosT_ref, sinT_ref, rc_ref, rs_ref,
                     qT_ref, kcmp_ref, vcmp_ref, kslc_ref, kwin_ref, vslcT_ref, vwinT_ref, gT_ref, qmT_ref, cmp_ref):
    h = _rms(x_ref[0], g_ref[...]).astype(BF16)
    qT = (_dot(h, wq_ref[...]) * Q_MUL).T
    cos = cosT_ref[...]
    sin = sinT_ref[...]
    half = HEAD_DIM // 2
    for hh in range(N_MIX_HEADS):
        x1 = qT[hh * HEAD_DIM:hh * HEAD_DIM + half]
        x2 = qT[hh * HEAD_DIM + half:(hh + 1) * HEAD_DIM]
        qT_ref[0, hh * HEAD_DIM:hh * HEAD_DIM + half, :] = (x1 * cos - x2 * sin).astype(BF16)
        qT_ref[0, hh * HEAD_DIM + half:(hh + 1) * HEAD_DIM, :] = (x2 * cos + x1 * sin).astype(BF16)
    kv = _dot(h, wkv_ref[...])
    rc = rc_ref[...]
    rs = rs_ref[...]
    cmp_ref[0] = kv[:, 0:128]
    cmp_ref[1] = kv[:, 128:256]
    for r in range(CMP_STRIDE):
        kcmp_ref[0, :, r * LANES:(r + 1) * LANES] = cmp_ref[0, pl.ds(r, TM // CMP_STRIDE, stride=CMP_STRIDE), :]
        vcmp_ref[0, :, r * LANES:(r + 1) * LANES] = cmp_ref[1, pl.ds(r, TM // CMP_STRIDE, stride=CMP_STRIDE), :]
    kslc_ref[0] = _rope_lanes(kv[:, 256:384], rc, rs).astype(BF16)
    kwin_ref[0] = _rope_lanes(kv[:, 512:640], rc, rs).astype(BF16)
    vsT = kv[:, 384:512].T.astype(BF16)
    vwT = kv[:, 640:768].T.astype(BF16)
    for j in range(TM // NSA_KC):
        vslcT_ref[0, j] = vsT[:, j * NSA_KC:(j + 1) * NSA_KC]
        vwinT_ref[0, j] = vwT[:, j * NSA_KC:(j + 1) * NSA_KC]
    gates = jax.nn.sigmoid(_dot(h, wg_ref[...]) + gb_ref[...])
    gT_ref[0] = gates.T[0:GATE_ROWS]
    qmT_ref[0] = (_dot(h, wqm_ref[...]) * Q_MUL).T.astype(BF16)


def _proj_nsa(x, params, cosT, sinT, rc, rs):
    B, S, _ = x.shape
    nt = S // TM
    arrs, specs = _split_params(params)
    tok = lambda w: pl.BlockSpec((1, TM, w), lambda b, j: (b, j, 0))
    featT = lambda r: pl.BlockSpec((1, r, TM), lambda b, j: (b, 0, j))
    vT = pl.BlockSpec((1, TM // NSA_KC, LANES, NSA_KC), lambda b, j: (b, j, 0, 0))
    chunked = pl.BlockSpec((1, TM // CMP_STRIDE, CMP_STRIDE * LANES), lambda b, j: (b, j, 0))
    return pl.pallas_call(
        _proj_nsa_kernel,
        grid=(B, nt),
        in_specs=[tok(D_MODEL)] + specs + [
                  pl.BlockSpec((HEAD_DIM // 2, TM), lambda b, j: (0, j)),
                  pl.BlockSpec((HEAD_DIM // 2, TM), lambda b, j: (0, j)),
                  pl.BlockSpec((TM, LANES), lambda b, j: (j, 0)),
                  pl.BlockSpec((TM, LANES), lambda b, j: (j, 0))],
        out_specs=[featT(MIX_W), chunked, chunked, tok(LANES), tok(LANES), vT, vT, featT(GATE_ROWS),
                   featT(MEM_W)],
        out_shape=[jax.ShapeDtypeStruct((B, MIX_W, S), BF16),
                   jax.ShapeDtypeStruct((B, S // CMP_STRIDE, CMP_STRIDE * LANES), F32),
                   jax.ShapeDtypeStruct((B, S // CMP_STRIDE, CMP_STRIDE * LANES), F32),
                   jax.ShapeDtypeStruct((B, S, LANES), BF16), jax.ShapeDtypeStruct((B, S, LANES), BF16),
                   jax.ShapeDtypeStruct((B, S // NSA_KC, LANES, NSA_KC), BF16),
                   jax.ShapeDtypeStruct((B, S // NSA_KC, LANES, NSA_KC), BF16),
                   jax.ShapeDtypeStruct((B, GATE_ROWS, S), F32),
                   jax.ShapeDtypeStruct((B, MEM_W, S), BF16)],
        scratch_shapes=[pltpu.VMEM((2, TM, LANES), F32)],
        compiler_params=_params(("arbitrary", "arbitrary")),
        name="proj_nsa",
    )(x, *arrs, cosT, sinT, rc, rs)


def _proj_fox_kernel(x_ref, g_ref, wq_ref, wqm_ref, qT_ref, qmT_ref):
    h = _rms(x_ref[0], g_ref[...]).astype(BF16)
    qT_ref[0] = (_dot(h, wq_ref[...]) * Q_MUL).T.astype(BF16)
    qmT_ref[0] = (_dot(h, wqm_ref[...]) * Q_MUL).T.astype(BF16)


def _proj_fox(x, params):
    B, S, _ = x.shape
    arrs, specs = _split_params(params)
    featT = lambda r: pl.BlockSpec((1, r, TM), lambda b, j: (b, 0, j))
    return pl.pallas_call(
        _proj_fox_kernel,
        grid=(B, S // TM),
        in_specs=[pl.BlockSpec((1, TM, D_MODEL), lambda b, j: (b, j, 0))] + specs,
        out_specs=[featT(MIX_W), featT(MEM_W)],
        out_shape=[jax.ShapeDtypeStruct((B, MIX_W, S), BF16), jax.ShapeDtypeStruct((B, MEM_W, S), BF16)],
        compiler_params=_params(("arbitrary", "arbitrary")),
        name="proj_fox",
    )(x, *arrs)


def _kv_shared_kernel(x_ref, g_ref, wk_ref, wv_ref, wf_ref, bf_ref, k_ref, vT_ref, kbias_ref, dcumT_ref, carry_ref):
    j = pl.program_id(1)
    h = _rms(x_ref[0], g_ref[...]).astype(BF16)
    k_ref[0] = _dot(h, wk_ref[...]).astype(BF16)
    vT = _dot(h, wv_ref[...]).T.astype(BF16)
    for i in range(KV_TM // FOX_KC):
        vT_ref[0, i] = vT[:, i * FOX_KC:(i + 1) * FOX_KC]
    z = _dot(h, wf_ref[...]) + bf_ref[...]
    logf = jnp.minimum(z, 0.0) - jnp.log1p(jnp.exp(-jnp.abs(z)))
    row = lax.broadcasted_iota(jnp.int32, (KV_TM, KV_TM), 0)
    col = lax.broadcasted_iota(jnp.int32, (KV_TM, KV_TM), 1)
    tri = (col <= row).astype(BF16)
    a, b, c = _split3(logf)
    @pl.when(j == 0)
    def _():
        carry_ref[...] = jnp.zeros(carry_ref.shape, F32)

    cs = _dot(tri, a) + _dot(tri, b) + _dot(tri, c) + carry_ref[7:8, :]
    carry_ref[...] = cs[KV_TM - 8:KV_TM]
    dcumT_ref[0] = cs.T[0:16]
    parts = _split3(cs * LOG2E)
    src = lax.broadcasted_iota(jnp.int32, (LANES, LANES), 0)
    dst = lax.broadcasted_iota(jnp.int32, (LANES, LANES), 1)
    lane = lax.broadcasted_iota(jnp.int32, (1, LANES), 1)
    ones = ((lane >= 3) & (lane < 6)).astype(F32)
    for p in range(N_MIX_HEADS // 2):
        acc = ones
        for r in range(3):
            place = ((src == 2 * p) & (dst == r)) | ((src == 2 * p + 1) & (dst == 6 + r))
            acc = acc + _dot(parts[r], place.astype(BF16))
        kbias_ref[0, p] = acc.astype(BF16)


def _kv_shared(x, params):
    B, S, _ = x.shape
    arrs, specs = _split_params(params)
    return pl.pallas_call(
        _kv_shared_kernel,
        grid=(B, S // KV_TM),
        in_specs=[pl.BlockSpec((1, KV_TM, D_MODEL), lambda b, j: (b, j, 0))] + specs,
        out_specs=[pl.BlockSpec((1, KV_TM, MIX_W), lambda b, j: (b, j, 0)),
                   pl.BlockSpec((1, KV_TM // FOX_KC, MIX_W, FOX_KC), lambda b, j: (b, j, 0, 0)),
                   pl.BlockSpec((1, N_MIX_HEADS // 2, KV_TM, LANES), lambda b, j: (b, 0, j, 0)),
                   pl.BlockSpec((1, 16, KV_TM), lambda b, j: (b, 0, j))],
        out_shape=[jax.ShapeDtypeStruct((B, S, MIX_W), BF16),
                   jax.ShapeDtypeStruct((B, S // FOX_KC, MIX_W, FOX_KC), BF16),
                   jax.ShapeDtypeStruct((B, N_MIX_HEADS // 2, S, LANES), BF16),
                   jax.ShapeDtypeStruct((B, 16, S), F32)],
        scratch_shapes=[pltpu.VMEM((8, LANES), F32)],
        compiler_params=_params(("arbitrary", "arbitrary")),
        name="kv_shared",
    )(x, *arrs)


def _mem_kv_kernel(m_ref, g_ref, w_ref, k_ref, vT_ref):
    h = _rms(m_ref[0], g_ref[...]).astype(BF16)
    kv = _dot(h, w_ref[...])
    k_ref[0] = kv[:, 0:MEM_W].astype(BF16)
    vT_ref[0] = kv[:, MEM_W:2 * MEM_W].T.astype(BF16)


def _mem_kv(mem, params):
    B, M, _ = mem.shape
    (g, w), specs = _split_params(params)
    return pl.pallas_call(
        _mem_kv_kernel,
        grid=(B,),
        in_specs=[pl.BlockSpec((1, M, D_MODEL), lambda b: (b, 0, 0))] + specs,
        out_specs=[pl.BlockSpec((1, M, MEM_W), lambda b: (b, 0, 0)), pl.BlockSpec((1, MEM_W, M), lambda b: (b, 0, 0))],
        out_shape=[jax.ShapeDtypeStruct((B, M, MEM_W), BF16), jax.ShapeDtypeStruct((B, MEM_W, M), BF16)],
        compiler_params=_params(("arbitrary",)),
        name="mem_kv",
    )(mem, g, w)


def _compress_kernel(kx_ref, vx_ref, wtop_ref, wbot_ref, ptop_ref, pbot_ref, b1_ref, w2_ref, b2_ref, rc_ref, rs_ref,
                     kc_ref, vcT_ref):
    for kvi, x_ref in enumerate((kx_ref, vx_ref)):
        x = x_ref[0]
        xt = (x + ptop_ref[kvi]).astype(BF16)
        xb = (x + pbot_ref[kvi]).astype(BF16)
        out = b2_ref[kvi]
        for g in range(N_KV_GROUPS):
            top = _dot(xt, wtop_ref[kvi, g])
            bot = _dot(xb, wbot_ref[kvi, g])
            hid = top + pltpu.roll(bot, bot.shape[0] - 1, 0) + b1_ref[kvi]
            out = out + _dot(jax.nn.gelu(hid).astype(BF16), w2_ref[kvi, g])
        if kvi == 0:
            kc_ref[0] = _rope_lanes(out, rc_ref[...], rs_ref[...]).astype(BF16)
        else:
            vcT_ref[0] = out.T.astype(BF16)


def _compress(kx, vx, params, rc, rs):
    B, NC, W = kx.shape
    xs = pl.BlockSpec((1, NC, W), lambda b: (b, 0, 0))
    arrs, specs = _split_params(params)
    return pl.pallas_call(
        _compress_kernel,
        grid=(B,),
        in_specs=[xs, xs] + specs + [_const_spec(rc.shape), _const_spec(rs.shape)],
        out_specs=[pl.BlockSpec((1, NC, LANES), lambda b: (b, 0, 0)), pl.BlockSpec((1, LANES, NC), lambda b: (b, 0, 0))],
        out_shape=[jax.ShapeDtypeStruct((B, NC, LANES), BF16), jax.ShapeDtypeStruct((B, LANES, NC), BF16)],
        compiler_params=_params(("arbitrary",)),
        name="nsa_compress",
    )(kx, vx, *arrs, rc, rs)


def _online_tile(s, m_ref, lead, cols):
    m_old = m_ref[lead, :, cols]
    m_new = jnp.maximum(m_old, jnp.max(s, axis=0, keepdims=True))
    m_ref[lead, :, cols] = m_new
    return jnp.exp2(s - m_new).astype(BF16), jnp.exp2(m_old - m_new)


def _reset_state(m_ref, acc_ref):
    m_ref[...] = jnp.full(m_ref.shape, NEG, F32)
    acc_ref[...] = jnp.zeros(acc_ref.shape, F32)


def _with_ones(vT):
    return jnp.concatenate([vT, jnp.ones((ACC_ROWS - HEAD_DIM, vT.shape[1]), vT.dtype)], axis=0)


def _normalise(acc):
    return acc[0:HEAD_DIM] * (1.0 / jnp.maximum(acc[HEAD_DIM:HEAD_DIM + 1], 1e-30))


def _nsa_attn_kernel(qT_ref, gT_ref, kc_ref, vcT_ref, kslc_ref, kwin_ref, vslcT_ref, vwinT_ref, ovT_ref,
                     o_ref, sel_ref, m_ref, acc_ref):
    c = pl.program_id(1)
    n_sel = sel_ref.shape[1]
    n_cmp = kc_ref.shape[1]
    GQ = HEADS_PER_GROUP * QB
    GW = HEADS_PER_GROUP * HEAD_DIM
    t_row = c * QB + lax.broadcasted_iota(jnp.int32, (1, QB), 1)
    gates = gT_ref[0]

    q6 = [jnp.concatenate([qT_ref[0, (HEADS_PER_GROUP * g + hh) * HEAD_DIM:(HEADS_PER_GROUP * g + hh + 1) * HEAD_DIM, :]
                           for hh in range(HEADS_PER_GROUP)], axis=1) for g in range(N_KV_GROUPS)]
    z = jnp.zeros_like(q6[0])
    qbd = jnp.concatenate([jnp.concatenate([q6[0], z], axis=1), jnp.concatenate([z, q6[1]], axis=1)], axis=0)

    cmp_end = lax.broadcasted_iota(jnp.int32, (n_cmp, QB), 0) * CMP_STRIDE + (L_CMP - 1)
    valid_c = cmp_end <= t_row
    s_cmp = _dot(kc_ref[0], qbd)
    j_idx = lax.broadcasted_iota(jnp.int32, (n_sel, QB), 0)
    blk_t = lax.shift_right_logical(t_row, 6)
    forced = ((j_idx == 0) | (j_idx == blk_t) | (j_idx == blk_t - 1)).astype(F32)
    o_cmp = []
    for g in range(N_KV_GROUPS):
        psum = jnp.zeros((n_cmp, QB), F32)
        ps = []
        for hh in range(HEADS_PER_GROUP):
            col = g * GQ + hh * QB
            sh = jnp.where(valid_c, s_cmp[:, col:col + QB], MASKED)
            e = jnp.exp2(sh - jnp.maximum(jnp.max(sh, axis=0, keepdims=True), NEG))
            p = e * (1.0 / jnp.maximum(jnp.sum(e, axis=0, keepdims=True), 1e-30))
            psum = psum + p
            ps.append(p.astype(BF16))
        o_cmp.append(_dot(vcT_ref[0, g * HEAD_DIM:(g + 1) * HEAD_DIM, :], jnp.concatenate(ps, axis=1)))
        p_hi = psum.astype(BF16)
        p_lo = (psum - p_hi.astype(F32)).astype(BF16)
        imp = _dot(ovT_ref[...], p_hi) + _dot(ovT_ref[...], p_lo)
        score = jnp.where(j_idx <= blk_t, imp + FORCE_BONUS * forced, NEG)
        cnt = jnp.zeros((n_sel, QB), F32)
        for i in range(n_sel):
            si = score[i:i + 1, :]
            cnt = cnt + ((si > score) | ((si == score) & (j_idx > i))).astype(F32)
        sel = (cnt < float(min(TOP_N, n_sel))).astype(F32)
        for i in range(n_sel):
            sel_ref[g, i] = jnp.broadcast_to(sel[i:i + 1, :], (8, QB))

    _reset_state(m_ref, acc_ref)
    key_off = lax.broadcasted_iota(jnp.int32, (NSA_KC, QB), 0)

    def scores(k_ref, kb):
        return _dot(k_ref[0, kb * NSA_KC:(kb + 1) * NSA_KC, :], qbd)

    def consume(s, br, vT_ref, kb, masks):
        for g in range(N_KV_GROUPS):
            ps, alphas = [], []
            for hh in range(HEADS_PER_GROUP):
                col = g * GQ + hh * QB
                tile = s[:, col:col + QB]
                if masks[g] is not None:
                    tile = jnp.where(masks[g], tile, MASKED)
                p, a = _online_tile(tile, m_ref, br, slice(col, col + QB))
                ps.append(p)
                alphas.append(a)
            gc = slice(g * GQ, (g + 1) * GQ)
            vT = _with_ones(vT_ref[0, kb, g * HEAD_DIM:(g + 1) * HEAD_DIM, :])
            acc_ref[br, :, gc] = acc_ref[br, :, gc] * jnp.concatenate(alphas, axis=1) + _dot(vT, jnp.concatenate(ps, axis=1))

    def slc_masks(kb, causal):
        per_chunk = NSA_KC // L_SEL
        masks = []
        for g in range(N_KV_GROUPS):
            picked = sel_ref[g, per_chunk * kb + per_chunk - 1][0:1, :]
            for i in reversed(range(per_chunk - 1)):
                picked = jnp.where(key_off < (i + 1) * L_SEL, sel_ref[g, per_chunk * kb + i][0:1, :], picked)
            mk = picked > 0.5
            masks.append(mk & (kb * NSA_KC + key_off <= t_row) if causal else mk)
        return masks

    def win_masks(kb, last):
        key_pos = kb * NSA_KC + key_off
        if kb == last:
            wm = key_pos <= t_row
        elif kb == last - WINDOW // NSA_KC:
            wm = t_row - key_pos < WINDOW
        else:
            wm = None
        return [wm, wm]

    last_dyn = lax.shift_right_logical(c * QB, NSA_KC.bit_length() - 1)
    for last in range(kslc_ref.shape[1] // NSA_KC):
        @pl.when(last_dyn == last)
        def _(last=last):
            first_win = max(last - WINDOW // NSA_KC, 0)
            plan = [(0, kslc_ref, vslcT_ref, kb, False) for kb in range(first_win)]
            for kb in range(first_win, last + 1):
                plan += [(0, kslc_ref, vslcT_ref, kb, kb == last), (1, kwin_ref, vwinT_ref, kb, None)]
            s_next = scores(plan[0][1], plan[0][3])
            for i, (br, _, vT_ref, kb, causal) in enumerate(plan):
                s_cur = s_next
                if i + 1 < len(plan):
                    s_next = scores(plan[i + 1][1], plan[i + 1][3])
                consume(s_cur, br, vT_ref, kb, slc_masks(kb, causal) if br == 0 else win_masks(kb, last))

    o_slc = _normalise(acc_ref[0])
    o_win = _normalise(acc_ref[1])

    for g in range(N_KV_GROUPS):
        outs = []
        for hh in range(HEADS_PER_GROUP):
            r = 3 * (HEADS_PER_GROUP * g + hh)
            cols = slice(g * GQ + hh * QB, g * GQ + (hh + 1) * QB)
            outs.append(gates[r:r + 1] * o_cmp[g][:, hh * QB:(hh + 1) * QB] + gates[r + 1:r + 2] * o_slc[:, cols]
                        + gates[r + 2:r + 3] * o_win[:, cols])
        o_ref[0, :, g * GW:(g + 1) * GW] = jnp.concatenate(outs, axis=0).T.astype(BF16)


def _nsa_attn(qT, gT, kc, vcT, kslc, kwin, vslcT, vwinT, ovT):
    B, _, S = qT.shape
    n_sel = S // L_SEL
    W12 = N_MIX_HEADS * QB
    per_b3 = lambda a: pl.BlockSpec((1,) + a.shape[1:], lambda b, c: (b, 0, 0))
    per_b4 = lambda a: pl.BlockSpec((1,) + a.shape[1:], lambda b, c: (b, 0, 0, 0))
    return pl.pallas_call(
        _nsa_attn_kernel,
        grid=(B, S // QB),
        in_specs=[pl.BlockSpec((1, MIX_W, QB), lambda b, c: (b, 0, c)),
                  pl.BlockSpec((1, GATE_ROWS, QB), lambda b, c: (b, 0, c)),
                  per_b3(kc), per_b3(vcT), per_b3(kslc), per_b3(kwin), per_b4(vslcT), per_b4(vwinT),
                  _const_spec(ovT.shape)],
        out_specs=pl.BlockSpec((1, QB, MIX_W), lambda b, c: (b, c, 0)),
        out_shape=jax.ShapeDtypeStruct((B, S, MIX_W), BF16),
        scratch_shapes=[pltpu.VMEM((N_KV_GROUPS, n_sel, 8, QB), F32), pltpu.VMEM((2, 1, W12), F32),
                        pltpu.VMEM((2, ACC_ROWS, W12), F32)],
        compiler_params=_params(("arbitrary", "arbitrary")),
        name="nsa_attn",
    )(qT, gT, kc, vcT, kslc, kwin, vslcT, vwinT, ovT)


def _pair_q(q2, width):
    z = jnp.zeros((HEAD_DIM, width), q2.dtype)
    return jnp.concatenate([jnp.concatenate([q2[0:HEAD_DIM], z], axis=1),
                            jnp.concatenate([z, q2[HEAD_DIM:2 * HEAD_DIM]], axis=1)], axis=0)


def _fox_attn_kernel(qT_ref, k_ref, vT_ref, kbias_ref, dcumT_ref, o_ref, m_ref, acc_ref):
    p = pl.program_id(1)
    c = pl.program_id(2)
    nq = FOX_TQ // QB
    per_q = FOX_TQ // FOX_KC

    rowi = lax.broadcasted_iota(jnp.int32, (16, 2 * FOX_TQ), 0)
    head1 = lax.broadcasted_iota(jnp.int32, (16, 2 * FOX_TQ), 1) >= FOX_TQ
    bias = jnp.where(((rowi < 3) & ~head1) | ((rowi >= 6) & (rowi < 9) & head1), -1.0, 0.0)
    dq = [_split3(dcumT_ref[0, pl.ds(2 * p + hh, 1), :] * LOG2E) for hh in range(2)]
    for r in range(3):
        bias = jnp.where(rowi == 3 + r, jnp.concatenate([dq[0][r], dq[1][r]], axis=1).astype(F32), bias)
    qaug = jnp.concatenate([_pair_q(qT_ref[0], FOX_TQ), bias.astype(BF16),
                            jnp.zeros((LANES - 16, 2 * FOX_TQ), BF16)], axis=0)

    _reset_state(m_ref, acc_ref)
    lane_t =lax.broadcasted_iota(jnp.int32, (1, QB), 1)

    def scores(kb, n_kc, qs_lo, masked):
        start, size = kb * FOX_KC, n_kc * FOX_KC
        kk = jnp.concatenate([k_ref[0, pl.ds(start, size), :], kbias_ref[0, 0, pl.ds(start, size), :]], axis=1)
        live = FOX_TQ - qs_lo * QB
        if qs_lo == 0:
            s2 = _dot(kk, qaug)
            return [s2[:, 0:FOX_TQ], s2[:, FOX_TQ:2 * FOX_TQ]]
        return [_dot(kk, qaug[:, (hh + 1) * FOX_TQ - live:(hh + 1) * FOX_TQ]) for hh in range(2)]

    def consume(s, kb, n_kc, qs_lo, masked):
        live = FOX_TQ - qs_lo * QB
        if masked:
            key_pos = kb * FOX_KC + lax.broadcasted_iota(jnp.int32, (n_kc * FOX_KC, QB), 0)
        for hh in range(2):
            cols = slice((hh + 1) * FOX_TQ - live, (hh + 1) * FOX_TQ)
            ps, alphas = [], []
            for qs in range(qs_lo, nq):
                tile = s[hh][:, (qs - qs_lo) * QB:(qs - qs_lo + 1) * QB]
                if masked:
                    tile = jnp.where(key_pos <= c * FOX_TQ + qs * QB + lane_t, tile, MASKED)
                col = hh * FOX_TQ + qs * QB
                pr, a = _online_tile(tile, m_ref, 0, slice(col, col + QB))
                ps.append(pr)
                alphas.append(a)
            vT = jnp.concatenate([vT_ref[0, kb + i, hh * HEAD_DIM:(hh + 1) * HEAD_DIM, :] for i in range(n_kc)], axis=1)
            acc_ref[0, :, cols] = (acc_ref[0, :, cols] * jnp.concatenate(alphas, axis=1)
                                   + _dot(_with_ones(vT), jnp.concatenate(ps, axis=1)))

    for cc in range(k_ref.shape[1] // FOX_TQ):
        @pl.when(c == cc)
        def _(cc=cc):
            plan = [(i * per_q, per_q, 0, False) for i in range(cc)]
            plan += [(cc * per_q + i, 1, i * FOX_KC // QB, True) for i in range(per_q)]
            s_next = scores(*plan[0])
            for i, step in enumerate(plan):
                s_cur = s_next
                if i + 1 < len(plan):
                    s_next = scores(*plan[i + 1])
                consume(s_cur, *step)
    o = _normalise(acc_ref[0])
    oT =jnp.concatenate([o[:, 0:FOX_TQ], o[:, FOX_TQ:2 * FOX_TQ]], axis=0)
    o_ref[0] = oT.T.astype(BF16)


def _fox_attn(qT, k, vT, kbias, dcumT):
    B, _, S = qT.shape
    n_pairs = N_MIX_HEADS // 2
    return pl.pallas_call(
        _fox_attn_kernel,
        grid=(B, n_pairs, S // FOX_TQ),
        in_specs=[pl.BlockSpec((1, LANES, FOX_TQ), lambda b, p, c: (b, p, c)),
                  pl.BlockSpec((1, S, LANES), lambda b, p, c: (b, 0, p)),
                  pl.BlockSpec((1, S // FOX_KC, LANES, FOX_KC), lambda b, p, c: (b, 0, p, 0)),
                  pl.BlockSpec((1, 1, S, LANES), lambda b, p, c: (b, p, 0, 0)),
                  pl.BlockSpec((1, 16, FOX_TQ), lambda b, p, c: (b, 0, c))],
        out_specs=pl.BlockSpec((1, FOX_TQ, LANES), lambda b, p, c: (b, c, p)),
        out_shape=jax.ShapeDtypeStruct((B, S, MIX_W), BF16),
        scratch_shapes=[pltpu.VMEM((1, 1, 2 * FOX_TQ), F32), pltpu.VMEM((1, ACC_ROWS, 2 * FOX_TQ), F32)],
        compiler_params=_params(("arbitrary", "arbitrary", "arbitrary")),
        name="fox_attn",
    )(qT, k, vT, kbias, dcumT)


def _mem_scores(qT_ref, k_ref):
    tq = qT_ref.shape[2]
    return [_dot(k_ref[0, :, pr * LANES:(pr + 1) * LANES], _pair_q(qT_ref[0, pr * LANES:(pr + 1) * LANES, :], tq))
            for pr in range(N_MEM_HEADS // 2)]


def _mem_finish(scores, vT_ref):
    tq = scores[0].shape[1] // 2
    outs = []
    for pr, s in enumerate(scores):
        for hh in range(2):
            ps = []
            for qs in range(tq // QB):
                col = hh * tq + qs * QB
                sh = s[:, col:col + QB]
                e = jnp.exp2(sh - jnp.max(sh, axis=0, keepdims=True))
                ps.append((e * (1.0 / jnp.sum(e, axis=0, keepdims=True))).astype(BF16))
            h = 2 * pr + hh
            outs.append(_dot(vT_ref[0, h * HEAD_DIM:(h + 1) * HEAD_DIM, :], jnp.concatenate(ps, axis=1)))
    return jnp.concatenate(outs, axis=0).T.astype(BF16)


def _ffn_kernel(x_ref, omix_ref, qmT_ref, kmem_ref, vmemT_ref, wo1_ref, wo2_ref, g_ref, wa_ref, wb_ref, cwa_ref,
                cwb_ref, cba_ref, cbb_ref, wd_ref, gfin_ref, o_ref, ca_ref, cb_ref, gate_ref, *, final):
    j = pl.program_id(1)
    omem = _mem_finish(_mem_scores(qmT_ref, kmem_ref), vmemT_ref)
    x1 = x_ref[0] + _dot(omix_ref[0], wo1_ref[...]) + _dot(omem, wo2_ref[...])
    h = _rms(x1, g_ref[...]).astype(BF16)

    @pl.when(j == 0)
    def _():
        ca_ref[...] = jnp.zeros(ca_ref.shape, F32)
        cb_ref[...] = jnp.zeros(cb_ref.shape, F32)

    def conv(u, prev8, w, bias):
        ext = jnp.concatenate([prev8, u], axis=0)
        u1 = pltpu.roll(ext, 1, 0)[8:]
        u2 = pltpu.roll(ext, 2, 0)[8:]
        return w[0:1] * u2 + w[1:2] * u1 + w[2:3] * u + bias

    for ci in range(D_FF // FF_CHUNK):
        sl = slice(ci * FF_CHUNK, (ci + 1) * FF_CHUNK)
        a = _dot(h, wa_ref[:, sl])
        b = _dot(h, wb_ref[:, sl])
        pa = ca_ref[:, sl]
        pb = cb_ref[:, sl]
        ca_ref[:, sl] = a[FF_TM - 8:FF_TM]
        cb_ref[:, sl] = b[FF_TM - 8:FF_TM]
        ac = conv(a, pa, cwa_ref[:, sl], cba_ref[:, sl])
        bc = conv(b, pb, cwb_ref[:, sl], cbb_ref[:, sl])
        gate_ref[:, sl] = (ac * jax.nn.sigmoid(ac) * bc).astype(BF16)
    y = x1 + _dot(gate_ref[...], wd_ref[...])
    o_ref[0] = _rms(y, gfin_ref[...]) if final else y


def _ffn(x, omix, qmT, kmem, vmemT, params, final):
    B, S, _ = x.shape
    M = kmem.shape[1]
    tok = lambda w: pl.BlockSpec((1, FF_TM, w), lambda b, j: (b, j, 0))
    arrs, specs = _split_params(params)
    return pl.pallas_call(
        functools.partial(_ffn_kernel, final=final),
        grid=(B, S // FF_TM),
        in_specs=[tok(D_MODEL), tok(MIX_W), pl.BlockSpec((1, MEM_W, FF_TM), lambda b, j: (b, 0, j)),
                  pl.BlockSpec((1, M, MEM_W), lambda b, j: (b, 0, 0)),
                  pl.BlockSpec((1, MEM_W, M), lambda b, j: (b, 0, 0))] + specs,
        out_specs=tok(D_MODEL),
        out_shape=jax.ShapeDtypeStruct((B, S, D_MODEL), F32),
        scratch_shapes=[pltpu.VMEM((8, D_FF), F32), pltpu.VMEM((8, D_FF), F32), pltpu.VMEM((FF_TM, D_FF), BF16)],
        compiler_params=_params(("arbitrary", "arbitrary")),
        name="memattn_outproj_convffn",
    )(x, omix, qmT, kmem, vmemT, *arrs)


def _rope_tables(pos):
    half = HEAD_DIM // 2
    inv = ROPE_THETA ** (-jnp.arange(half, dtype=F32) / half)
    ang = pos.astype(F32)[:, None] * inv[None, :]
    cos, sin = jnp.cos(ang), jnp.sin(ang)
    reps = LANES // half
    signs = jnp.tile(jnp.concatenate([-jnp.ones((half,), F32), jnp.ones((half,), F32)]), LANES // HEAD_DIM)
    return cos.T, sin.T, jnp.tile(cos, (1, reps)), jnp.tile(sin, (1, reps)) * signs[None, :]


def _overlap_T(S, n_cmp_pad):
    n_cmp = (S - L_CMP) // CMP_STRIDE + 1
    n_sel = S // L_SEL
    cs = np.arange(n_cmp_pad) * CMP_STRIDE
    ss = np.arange(n_sel) * L_SEL
    ov = (cs[None, :] < ss[:, None] + L_SEL) & (cs[None, :] + L_CMP > ss[:, None]) & (np.arange(n_cmp_pad)[None, :] < n_cmp)
    return jnp.asarray(ov, dtype=BF16)


def _compress_params(pos, w1, b1, w2, b2):
    half = L_CMP // 2
    w1r = w1.reshape(2, L_CMP, HEAD_DIM, CMP_HIDDEN)

    def spread(w):
        z = jnp.zeros((2, N_KV_GROUPS, half, N_KV_GROUPS, HEAD_DIM, CMP_HIDDEN), F32)
        for g in range(N_KV_GROUPS):
            z = z.at[:, g, :, g].set(w)
        return z.reshape(2, N_KV_GROUPS, half * LANES, CMP_HIDDEN).astype(BF16)

    def spread_pos(p):
        return jnp.tile(p[:, :, None, :], (1, 1, N_KV_GROUPS, 1)).reshape(2, 1, half * LANES)

    w2p = jnp.zeros((2, N_KV_GROUPS, CMP_HIDDEN, N_KV_GROUPS, HEAD_DIM), F32)
    for g in range(N_KV_GROUPS):
        w2p = w2p.at[:, g, :, g].set(w2)
    return (spread(w1r[:, :half]), spread(w1r[:, half:]), spread_pos(pos[:, :half]), spread_pos(pos[:, half:]),
            b1[:, None, :], w2p.reshape(2, N_KV_GROUPS, CMP_HIDDEN, LANES).astype(BF16),
            jnp.tile(b2, (1, N_KV_GROUPS))[:, None, :])


def _pad_cols(w, width):
    return jnp.pad(w, ((0, 0), (0, width - w.shape[1])))


def kernel(x, mem, attn_norm, ffn_norm, mem_norm, w_mem_kv, w_o, w_up, conv_w, conv_b, w_down, a_w_in, a_gate_b,
           a_cmp_pos, a_cmp_w1, a_cmp_b1, a_cmp_w2, a_cmp_b2, b_w_in, kv_norm, w_kv_shared, b_fgate, final_norm):
    B, S, _ = x.shape
    assert S % FOX_TQ == 0 and S % (CMP_STRIDE * 8) == 0
    n_chunks = S // CMP_STRIDE
    cosT, sinT, rc, rs = _rope_tables(jnp.arange(S))
    _, _, rc_cmp, rs_cmp = _rope_tables(jnp.arange(n_chunks) * CMP_STRIDE + L_CMP - 1)
    ovT = _overlap_T(S, n_chunks)

    rows3 = lambda v: v.reshape(v.shape[0], 1, -1).astype(F32)
    layer = lambda a, l, block, *idx: _w(a, (None,) + block, l, *idx)
    gain = lambda a, l: layer(a, l, (1, D_MODEL), 0, 0)
    attn_g, ffn_g, mem_g = rows3(attn_norm), rows3(ffn_norm), rows3(mem_norm)
    w_mem_b, w_o_b, w_up_b, w_down_b = (a.astype(BF16) for a in (w_mem_kv, w_o, w_up, w_down))
    b_w_b, w_sh_b = b_w_in.astype(BF16), w_kv_shared.astype(BF16)
    a_w_q = a_w_in[:, :, :2 * MIX_W].astype(BF16)
    a_w_b = a_w_in[:, :, 2 * MIX_W:].astype(BF16)
    conv_b3 = rows3(conv_b)
    a_wg = jnp.pad(a_w_b[:, :, :GATE_W], ((0, 0), (0, 0), (0, LANES - GATE_W)))
    a_gb = jnp.pad(rows3(a_gate_b), ((0, 0), (0, 0), (0, LANES - GATE_W)))
    a_wqm = a_w_b[:, :, GATE_W:]
    cmp_params = jax.vmap(_compress_params)(a_cmp_pos, a_cmp_w1, a_cmp_b1, a_cmp_w2, a_cmp_b2)
    w_f = _pad_cols(w_sh_b[:, 2 * MIX_W:], LANES)
    b_f = _pad_cols(b_fgate[None, :].astype(F32), LANES)
    fin_g = final_norm.reshape(1, -1).astype(F32)

    k_sh = vT_sh = kbias = dcumT = None
    for l in range(DEPTH):
        kmem, vmemT = _mem_kv(mem, [gain(mem_g, l), layer(w_mem_b, l, (D_MODEL, 2 * MEM_W), 0, 0)])
        if l < N_A:
            qT, kcmp, vcmp, kslc, kwin, vslcT, vwinT, gT, qmT = _proj_nsa(
                x, [gain(attn_g, l), layer(a_w_q, l, (D_MODEL, MIX_W), 0, 0), layer(a_w_q, l, (D_MODEL, MIX_W), 0, 1),
                    layer(a_wg, l, (D_MODEL, LANES), 0, 0), layer(a_gb, l, (1, LANES), 0, 0),
                    layer(a_wqm, l, (D_MODEL, MEM_W), 0, 0)],
                cosT, sinT, rc, rs)
            kc, vcT = _compress(kcmp, vcmp, [layer(a, l, a.shape[1:], *(0,) * (a.ndim - 1)) for a in cmp_params],
                                rc_cmp, rs_cmp)
            omix = _nsa_attn(qT, gT, kc, vcT, kslc, kwin, vslcT, vwinT, ovT)
        else:
            if l == N_A:
                k_sh, vT_sh, kbias, dcumT = _kv_shared(
                    x, [_w(kv_norm.reshape(1, -1).astype(F32)), _w(w_sh_b, (D_MODEL, MIX_W), 0, 0),
                        _w(w_sh_b, (D_MODEL, MIX_W), 0, 1), _w(w_f), _w(b_f)])
            qT, qmT = _proj_fox(x, [gain(attn_g, l), layer(b_w_b, l - N_A, (D_MODEL, MIX_W), 0, 0),
                                    layer(b_w_b, l - N_A, (D_MODEL, MEM_W), 0, MIX_W // MEM_W)])
            omix = _fox_attn(qT, k_sh, vT_sh, kbias, dcumT)
        x = _ffn(x, omix, qmT, kmem, vmemT,
                 [layer(w_o_b, l, (MIX_W, D_MODEL), 0, 0), layer(w_o_b, l, (MEM_W, D_MODEL), MIX_W // MEM_W, 0),
                  gain(ffn_g, l), layer(w_up_b, l, (D_MODEL, D_FF), 0, 0), layer(w_up_b, l, (D_MODEL, D_FF), 0, 1),
                  layer(conv_w, l, (CONV_WIDTH, D_FF), 0, 0), layer(conv_w, l, (CONV_WIDTH, D_FF), 0, 1),
                  layer(conv_b3, l, (1, D_FF), 0, 0), layer(conv_b3, l, (1, D_FF), 0, 1),
                  layer(w_down_b, l, (D_FF, D_MODEL), 0, 0), _w(fin_g)],
                 final=(l == DEPTH - 1))
    return x
```
